```python
import math
import jax, jax.numpy as jnp
from jax import lax
import numpy as np

D_MODEL = 1024
BATCH = 32
SEQ = 256
DEPTH = 4
DEC_BATCH = 2
DEC_SEQ = 4096
PAST_LEN = 256

GRID_W = 64
HEAD_DIM = 64
ATT_HEADS = 8
ATT_KV_HEADS = 2
ATT_GROUP = ATT_HEADS // ATT_KV_HEADS
ATT_WIDTH = ATT_HEADS * HEAD_DIM
Q_BLOCK = 128
ROPE_THETA = 10000.0
ROPE_AXIS_FREQS = HEAD_DIM // 4
RET_HEADS = 4
RET_DK = 64
RET_DV = 64
RET_WIDTH = RET_HEADS * RET_DV
SSD_HEADS = 4
SSD_HEAD_DIM = 64
SSD_WIDTH = SSD_HEADS * SSD_HEAD_DIM
SSD_GROUPS = 2
SSD_STATE = 64
SSD_CONV_W = 3
SSD_CONV_CH = SSD_WIDTH + 2 * SSD_GROUPS * SSD_STATE
SCAN_CHUNK = 128
MIX_WIDTH = ATT_WIDTH + RET_WIDTH + SSD_WIDTH
IN_SIZES = (ATT_WIDTH, ATT_KV_HEADS * HEAD_DIM, ATT_KV_HEADS * HEAD_DIM,
            RET_HEADS * RET_DK, RET_HEADS * RET_DK, RET_WIDTH, RET_WIDTH,
            SSD_WIDTH, SSD_WIDTH, SSD_GROUPS * SSD_STATE, SSD_GROUPS * SSD_STATE, 2 * SSD_HEADS)
IN_WIDTH = sum(IN_SIZES)
N_EXPERTS = 256
TOP_K = 8
ROUTER_GROUPS = 8
ROUTER_TOPK_GROUPS = 4
EXPERT_FF = 256
SHARED_FF = 256
ROUTED_SCALE = 2.5
MOE_BLOCK = 128
DEEPNORM_ALPHA = (2 * DEPTH) ** 0.25
DEEPNORM_BETA = (8 * DEPTH) ** -0.25
MOD_SCALE = 0.5
F32 = jnp.float32

kernel_name = 'hybrid_prefix_diffusion_trunk_step'


def layer_norm(x, w, b, eps=1e-5):
    xf = x.astype(F32)
    mu = xf.mean(-1, keepdims=True)
    var = jnp.square(xf - mu).mean(-1, keepdims=True)
    return ((xf - mu) * lax.rsqrt(var + eps) * w.astype(F32) + b.astype(F32)).astype(x.dtype)


def rms_norm(x, w, eps=1e-6):
    xf = x.astype(F32)
    y = xf * lax.rsqrt(jnp.mean(xf * xf, -1, keepdims=True) + eps)
    return (y * w.astype(F32)).astype(x.dtype)


def head_group_norm(o, eps=1e-5):
    mu = o.mean(-1, keepdims=True)
    var = jnp.square(o - mu).mean(-1, keepdims=True)
    return (o - mu) * lax.rsqrt(var + eps)


def modulation(cond, w, b):
    m = jax.nn.silu(cond) @ w + b
    return jnp.split(m[:, None, :], 6, axis=-1)


def grid_rope_tables(n_tokens, dtype):
    n_rows = n_tokens // GRID_W
    rows = jnp.repeat(jnp.arange(n_rows, dtype=F32), GRID_W)
    cols = jnp.tile(jnp.arange(GRID_W, dtype=F32), n_rows)
    inv = ROPE_THETA ** (-jnp.arange(ROPE_AXIS_FREQS, dtype=F32) / ROPE_AXIS_FREQS)
    ang = jnp.stack([rows[:, None] * inv, cols[:, None] * inv], axis=1)
    return jnp.cos(ang).astype(dtype), jnp.sin(ang).astype(dtype)


def apply_axial_rope(x, cos, sin):
    b, l, h, d = x.shape
    x6 = x.reshape(b, l, h, 2, 2, ROPE_AXIS_FREQS)
    x1, x2 = x6[..., 0, :], x6[..., 1, :]
    c = cos[None, :, None]
    s = sin[None, :, None]
    return jnp.stack([x1 * c - x2 * s, x1 * s + x2 * c], axis=-2).reshape(b, l, h, d)


def block_attention(q, k, v):
    b, lq = q.shape[:2]
    nb = lq // Q_BLOCK
    qb = q.reshape(b, nb, Q_BLOCK, ATT_KV_HEADS, ATT_GROUP, HEAD_DIM).swapaxes(0, 1)
    scale = HEAD_DIM ** -0.5

    def one_block(qblk):
        s = jnp.einsum('bqkgd,bskd->bkgqs', qblk, k).astype(F32) * scale
        p = jax.nn.softmax(s, axis=-1).astype(v.dtype)
        return jnp.einsum('bkgqs,bskd->bqkgd', p, v)

    o = lax.map(one_block, qb)
    return o.swapaxes(0, 1).reshape(b, lq, ATT_WIDTH)


def chunk_scan(q, k, v, log_a, s0):
    b, l, h, _ = q.shape
    dv = v.shape[-1]
    n = l // SCAN_CHUNK

    def chunks(t):
        t = t.astype(F32)
        return t.reshape((b, n, SCAN_CHUNK) + t.shape[2:]).swapaxes(0, 1)

    causal = jnp.tril(jnp.ones((SCAN_CHUNK, SCAN_CHUNK), bool))[None, :, :, None]

    def step(state, inp):
        qc, kc, vc, ac = inp
        cum = jnp.cumsum(ac, axis=1)
        diff = cum[:, :, None, :] - cum[:, None, :, :]
        decay = jnp.where(causal, jnp.exp(jnp.where(causal, diff, 0.0)), 0.0)
        scores = jnp.einsum('bihk,bjhk->bijh', qc, kc) * decay
        o = jnp.einsum('bijh,bjhv->bihv', scores, vc)
        o = o + jnp.einsum('bihk,bhkv->bihv', qc * jnp.exp(cum)[..., None], state)
        last = cum[:, -1]
        kw = kc * jnp.exp(last[:, None] - cum)[..., None]
        state = jnp.exp(last)[..., None, None] * state + jnp.einsum('bjhk,bjhv->bhkv', kw, vc)
        return state, o

    state, o = lax.scan(step, s0.astype(F32), (chunks(q), chunks(k), chunks(v), chunks(log_a)))
    return o.swapaxes(0, 1).reshape(b, l, h, dv), state


def bidirectional_scan(q, k_f, k_b, v, la_f, la_b, s0):
    flip = lambda t: jnp.flip(t, axis=1)
    o_f, s_f = chunk_scan(q, k_f, v, la_f, s0[:, 0])
    o_b, s_b = chunk_scan(flip(q), flip(k_b), flip(v), flip(la_b), s0[:, 1])
    return o_f + flip(o_b), jnp.stack([s_f, s_b], axis=1)


def depthwise_conv(x, w, b):
    y = lax.conv_general_dilated(x, w[:, None, :], window_strides=(1,),
                                 padding=[(SSD_CONV_W // 2, SSD_CONV_W // 2)],
                                 dimension_numbers=('NWC', 'WIO', 'NWC'),
                                 feature_group_count=x.shape[-1])
    return y + b


def token_mixers(u, p, ctx):
    b, l, _ = u.shape
    ty = u.dtype
    offsets = [int(o) for o in np.cumsum(IN_SIZES)[:-1]]
    aq, ak, av, rq, rk, rv, rg, sx, sz, sb, sc, sdt = jnp.split(u @ p['w_in'], offsets, axis=-1)

    q = rms_norm(aq.reshape(b, l, ATT_HEADS, HEAD_DIM), p['q_norm_w'])
    k = rms_norm(ak.reshape(b, l, ATT_KV_HEADS, HEAD_DIM), p['k_norm_w'])
    v = av.reshape(b, l, ATT_KV_HEADS, HEAD_DIM)
    if ctx is None:
        keys, vals = k, v
        s0_ret = jnp.zeros((b, 2, RET_HEADS, RET_DK, RET_DV), F32)
        s0_ssd = jnp.zeros((b, 2, SSD_HEADS, SSD_STATE, SSD_HEAD_DIM), F32)
    else:
        ctx_k, ctx_v, s0_ret, s0_ssd = ctx
        cos, sin = grid_rope_tables(l, ty)
        q = apply_axial_rope(q, cos, sin)
        k = apply_axial_rope(k, cos, sin)
        keys = jnp.concatenate([k, ctx_k.astype(ty)], axis=1)
        vals = jnp.concatenate([v, ctx_v.astype(ty)], axis=1)
    att = block_attention(q.reshape(b, l, ATT_KV_HEADS, ATT_GROUP, HEAD_DIM), keys, vals)

    log_gamma = jnp.log1p(-jnp.exp(p['ret_log_decay'].astype(F32)))
    la_r = jnp.broadcast_to(log_gamma[:, None, None, :], (2, b, l, RET_HEADS))
    rqh = rq.reshape(b, l, RET_HEADS, RET_DK)
    rkh = rk.reshape(b, l, RET_HEADS, RET_DK) * (RET_DK ** -0.5)
    rvh = rv.reshape(b, l, RET_HEADS, RET_DV)
    ro, ret_state = bidirectional_scan(rqh, rkh, rkh, rvh, la_r[0], la_r[1], s0_ret)
    ret = (head_group_norm(ro).reshape(b, l, RET_WIDTH) * p['ret_norm_w'].astype(F32)).astype(ty) * jax.nn.silu(rg)

    xbc = jax.nn.silu(depthwise_conv(jnp.concatenate([sx, sb, sc], axis=-1), p['ssd_conv_w'], p['ssd_conv_b']))
    sx, sb, sc = jnp.split(xbc, [SSD_WIDTH, SSD_WIDTH + SSD_GROUPS * SSD_STATE], axis=-1)
    rep = SSD_HEADS // SSD_GROUPS
    xh = sx.reshape(b, l, SSD_HEADS, SSD_HEAD_DIM)
    bh = jnp.repeat(sb.reshape(b, l, SSD_GROUPS, SSD_STATE), rep, axis=2).astype(F32)
    ch = jnp.repeat(sc.reshape(b, l, SSD_GROUPS, SSD_STATE), rep, axis=2)
    dt = jax.nn.softplus(sdt.reshape(b, l, 2, SSD_HEADS).astype(F32) + p['ssd_dt_bias'].astype(F32))
    la_s = dt * -jnp.exp(p['ssd_a_log'].astype(F32))
    so, ssd_state = bidirectional_scan(ch, bh * dt[:, :, 0, :, None], bh * dt[:, :, 1, :, None], xh,
                                       la_s[:, :, 0], la_s[:, :, 1], s0_ssd)
    so = so + p['ssd_d'].astype(F32)[:, None] * xh.astype(F32)
    ssd = rms_norm(so.reshape(b, l, SSD_WIDTH) * jax.nn.silu(sz.astype(F32)), p['ssd_norm_w']).astype(ty)

    out = jnp.concatenate([att, ret, ssd], axis=-1) @ p['w_out']
    new_ctx = (k, v, ret_state.astype(ty), ssd_state.astype(ty)) if ctx is None else None
    return out, new_ctx


def swiglu(x, wg, wu, wd):
    return (jax.nn.silu(x @ wg) * (x @ wu)) @ wd


def routed_experts(xf, idx, wts, w_gate, w_up, w_down):
    t, d = xf.shape
    a = t * TOP_K
    expert_of = idx.reshape(a)
    token_of = jnp.arange(a, dtype=jnp.int32) // TOP_K
    order = jnp.argsort(expert_of)
    sorted_e = expert_of[order]
    counts = jnp.bincount(expert_of, length=N_EXPERTS)
    padded = (counts + MOE_BLOCK - 1) // MOE_BLOCK * MOE_BLOCK
    start = jnp.cumsum(counts) - counts
    pend = jnp.cumsum(padded)
    pstart = pend - padded
    dest = pstart[sorted_e] + jnp.arange(a, dtype=jnp.int32) - start[sorted_e]
    n_blocks = -(-(a + N_EXPERTS * (MOE_BLOCK - 1)) // MOE_BLOCK)
    n_rows = n_blocks * MOE_BLOCK
    row_tok = jnp.full((n_rows,), t, jnp.int32).at[dest].set(token_of[order])
    row_w = jnp.zeros((n_rows,), xf.dtype).at[dest].set(wts.reshape(a)[order])
    block_e = jnp.minimum(jnp.searchsorted(pend, jnp.arange(n_blocks, dtype=jnp.int32) * MOE_BLOCK, side='right'),
                          N_EXPERTS - 1)
    x_pad = jnp.concatenate([xf, jnp.zeros((1, d), xf.dtype)], axis=0)

    def one_block(args):
        tok, e = args
        return swiglu(x_pad[tok], w_gate[e], w_up[e], w_down[e])

    y = lax.map(one_block, (row_tok.reshape(n_blocks, MOE_BLOCK), block_e)).reshape(n_rows, d)
    return jax.ops.segment_sum(y * row_w[:, None], row_tok, num_segments=t + 1)[:t]


def moe_ffn(u, p):
    b, l, d = u.shape
    xf = u.reshape(b * l, d)
    t = b * l
    scores = jax.nn.sigmoid((xf @ p['router_w']).astype(F32))
    sel = scores + p['router_bias'].astype(F32)
    grp = sel.reshape(t, ROUTER_GROUPS, N_EXPERTS // ROUTER_GROUPS)
    grp_score = lax.top_k(grp, 2)[0].sum(-1)
    _, top_grp = lax.top_k(grp_score, ROUTER_TOPK_GROUPS)
    grp_mask = (top_grp[..., None] == jnp.arange(ROUTER_GROUPS)).any(axis=1)
    sel = jnp.where(jnp.repeat(grp_mask, N_EXPERTS // ROUTER_GROUPS, axis=1), sel, -jnp.inf)
    _, idx = lax.top_k(sel, TOP_K)
    w = jnp.take_along_axis(scores, idx, axis=-1)
    w = w / w.sum(-1, keepdims=True) * ROUTED_SCALE
    routed = routed_experts(xf, idx, w.astype(u.dtype), p['exp_w_gate'], p['exp_w_up'], p['exp_w_down'])
    shared = swiglu(xf, p['sh_w_gate'], p['sh_w_up'], p['sh_w_down'])
    return (routed + shared).reshape(b, l, d)


def trunk_layer(x, cond, p, ctx):
    sh1, sc1, g1, sh2, sc2, g2 = modulation(cond, p['w_mod'], p['b_mod'])
    mix_out, new_ctx = token_mixers(x * (1 + sc1) + sh1, p, ctx)
    x = layer_norm(DEEPNORM_ALPHA * x + g1 * mix_out, p['ln_w'][0], p['ln_b'][0])
    ffn_out = moe_ffn(x * (1 + sc2) + sh2, p)
    x = layer_norm(DEEPNORM_ALPHA * x + g2 * ffn_out, p['ln_w'][1], p['ln_b'][1])
    return x, new_ctx


def setup_inputs(seed: int = 0) -> dict:
    key = jax.random.key(seed)
    ks = iter(jax.random.split(key, 40))

    def nrm(shape, scale):
        return scale * jax.random.normal(next(ks), shape, F32)

    def unif(shape, lo, hi):
        return jax.random.uniform(next(ks), shape, F32, lo, hi)

    d = D_MODEL
    x_prompt = nrm((BATCH, SEQ, d), 1.0)
    x_sample = nrm((DEC_BATCH, DEC_SEQ, d), 1.0)
    cache_k = nrm((DEC_BATCH, DEPTH, PAST_LEN, ATT_KV_HEADS, HEAD_DIM), 1.0)
    cache_v = nrm((DEC_BATCH, DEPTH, PAST_LEN, ATT_KV_HEADS, HEAD_DIM), 1.0)
    state_ret = nrm((DEC_BATCH, DEPTH, 2, RET_HEADS, RET_DK, RET_DV), 1.0)
    state_ssd = nrm((DEC_BATCH, DEPTH, 2, SSD_HEADS, SSD_STATE, SSD_HEAD_DIM), 0.2)
    c = nrm((DEC_BATCH, d), 1.0)
    c_ctx = nrm((d,), 1.0)
    w_mod = nrm((DEPTH, d, 6 * d), MOD_SCALE * d ** -0.5)
    b_mod = nrm((DEPTH, 6 * d), 0.02)
    w_in = nrm((DEPTH, d, IN_WIDTH), d ** -0.5)
    q_norm_w = 1.0 + nrm((DEPTH, HEAD_DIM), 0.02)
    k_norm_w = 1.0 + nrm((DEPTH, HEAD_DIM), 0.02)
    base_decay = -(5.0 + jnp.arange(RET_HEADS, dtype=F32)) * math.log(2.0)
    ret_log_decay = jnp.broadcast_to(base_decay, (DEPTH, 2, RET_HEADS)) + nrm((DEPTH, 2, RET_HEADS), 0.05)
    ret_norm_w = 1.0 + nrm((DEPTH, RET_WIDTH), 0.02)
    ssd_conv_w = nrm((DEPTH, SSD_CONV_W, SSD_CONV_CH), SSD_CONV_W ** -0.5)
    ssd_conv_b = nrm((DEPTH, SSD_CONV_CH), 0.02)
    dt0 = jnp.exp(unif((DEPTH, 2, SSD_HEADS), math.log(1e-3), math.log(1e-1)))
    ssd_dt_bias = dt0 + jnp.log(-jnp.expm1(-dt0))
    ssd_a_log = jnp.log(unif((DEPTH, 2, SSD_HEADS), 1.0, 16.0))
    ssd_d = 1.0 + nrm((DEPTH, SSD_HEADS), 0.1)
    ssd_norm_w = 1.0 + nrm((DEPTH, SSD_WIDTH), 0.02)
    w_out = nrm((DEPTH, MIX_WIDTH, d), MIX_WIDTH ** -0.5 * DEEPNORM_BETA)
    ln_w = 1.0 + nrm((DEPTH, 2, d), 0.02)
    ln_b = nrm((DEPTH, 2, d), 0.02)
    router_w = nrm((DEPTH, d, N_EXPERTS), d ** -0.5)
    router_bias = nrm((DEPTH, N_EXPERTS), 0.01)
    exp_w_gate = nrm((DEPTH, N_EXPERTS, d, EXPERT_FF), d ** -0.5)
    exp_w_up = nrm((DEPTH, N_EXPERTS, d, EXPERT_FF), d ** -0.5)
    exp_w_down = nrm((DEPTH, N_EXPERTS, EXPERT_FF, d), EXPERT_FF ** -0.5 * DEEPNORM_BETA)
    sh_w_gate = nrm((DEPTH, d, SHARED_FF), d ** -0.5)
    sh_w_up = nrm((DEPTH, d, SHARED_FF), d ** -0.5)
    sh_w_down = nrm((DEPTH, SHARED_FF, d), SHARED_FF ** -0.5 * DEEPNORM_BETA)
    return {'x_prompt': x_prompt, 'x_sample': x_sample, 'cache_k': cache_k, 'cache_v': cache_v,
            'state_ret': state_ret, 'state_ssd': state_ssd, 'c': c, 'c_ctx': c_ctx,
            'w_mod': w_mod, 'b_mod': b_mod, 'w_in': w_in, 'q_norm_w': q_norm_w, 'k_norm_w': k_norm_w,
            'ret_log_decay': ret_log_decay, 'ret_norm_w': ret_norm_w,
            'ssd_conv_w': ssd_conv_w, 'ssd_conv_b': ssd_conv_b, 'ssd_dt_bias': ssd_dt_bias,
            'ssd_a_log': ssd_a_log, 'ssd_d': ssd_d, 'ssd_norm_w': ssd_norm_w, 'w_out': w_out,
            'ln_w': ln_w, 'ln_b': ln_b, 'router_w': router_w, 'router_bias': router_bias,
            'exp_w_gate': exp_w_gate, 'exp_w_up': exp_w_up, 'exp_w_down': exp_w_down,
            'sh_w_gate': sh_w_gate, 'sh_w_up': sh_w_up, 'sh_w_down': sh_w_down}


def reference(x_prompt, x_sample, cache_k, cache_v, state_ret, state_ssd, c, c_ctx,
              w_mod, b_mod, w_in, q_norm_w, k_norm_w, ret_log_decay, ret_norm_w,
              ssd_conv_w, ssd_conv_b, ssd_dt_bias, ssd_a_log, ssd_d, ssd_norm_w, w_out,
              ln_w, ln_b, router_w, router_bias, exp_w_gate, exp_w_up, exp_w_down,
              sh_w_gate, sh_w_up, sh_w_down):
    def layer_params(l):
        return {'w_mod': w_mod[l], 'b_mod': b_mod[l], 'w_in': w_in[l], 'q_norm_w': q_norm_w[l],
                'k_norm_w': k_norm_w[l], 'ret_log_decay': ret_log_decay[l], 'ret_norm_w': ret_norm_w[l],
                'ssd_conv_w': ssd_conv_w[l], 'ssd_conv_b': ssd_conv_b[l], 'ssd_dt_bias': ssd_dt_bias[l],
                'ssd_a_log': ssd_a_log[l], 'ssd_d': ssd_d[l], 'ssd_norm_w': ssd_norm_w[l], 'w_out': w_out[l],
                'ln_w': ln_w[l], 'ln_b': ln_b[l], 'router_w': router_w[l], 'router_bias': router_bias[l],
                'exp_w_gate': exp_w_gate[l], 'exp_w_up': exp_w_up[l], 'exp_w_down': exp_w_down[l],
                'sh_w_gate': sh_w_gate[l], 'sh_w_up': sh_w_up[l], 'sh_w_down': sh_w_down[l]}

    xc = x_prompt
    ks, vs, srs, sss = [], [], [], []
    for l in range(DEPTH):
        xc, (k_l, v_l, sr_l, ss_l) = trunk_layer(xc, c_ctx[None], layer_params(l), None)
        ks.append(k_l)
        vs.append(v_l)
        srs.append(sr_l)
        sss.append(ss_l)
    new_cache_k = jnp.stack(ks, axis=1)
    new_cache_v = jnp.stack(vs, axis=1)
    new_state_ret = jnp.stack(srs, axis=1)
    new_state_ssd = jnp.stack(sss, axis=1)

    xs = x_sample
    for l in range(DEPTH):
        ctx = (cache_k[:, l], cache_v[:, l], state_ret[:, l], state_ssd[:, l])
        xs, _ = trunk_layer(xs, c, layer_params(l), ctx)

    return (xc, xs, new_cache_k, new_cache_v, new_state_ret, new_state_ssd)
```

```python
import functools

import numpy as np
import jax
import jax.numpy as jnp
from jax import lax
from jax.experimental import pallas as pl
from jax.experimental.pallas import tpu as pltpu

F32 = jnp.float32
BF16 = jnp.bfloat16
I32 = jnp.int32
HIGHEST = lax.Precision.HIGHEST
NT_DIMS = (((1,), (1,)), ((), ()))

D_MODEL = 1024
DEPTH = 4
GRID_W = 64
HEAD_DIM = 64
ATT_HEADS = 8
ATT_KV_HEADS = 2
ATT_WIDTH = ATT_HEADS * HEAD_DIM
ROPE_THETA = 10000.0
ROPE_AXIS_FREQS = HEAD_DIM // 4
RET_HEADS = 4
RET_DK = 64
RET_WIDTH = 256
SSD_HEADS = 4
SSD_WIDTH = 256
SSD_STATE = 64
N_EXPERTS = 256
TOP_K = 8
ROUTER_GROUPS = 8
ROUTER_TOPK_GROUPS = 4
GROUP_SIZE = N_EXPERTS // ROUTER_GROUPS
EXPERT_FF = 256
SHARED_FF = 256
ROUTED_SCALE = 2.5
DEEPNORM_ALPHA = (2 * DEPTH) ** 0.25
IN_SIZES = (512, 128, 128, 256, 256, 256, 256, 256, 256, 128, 128, 8)

LANES = 128
SUBLANES = 8
MIB = 1024 * 1024

QKV_W = 768
RET_OFF, RET_W = 768, 1024
SSD_OFF, SSD_W = 1792, 896
IN_W_PAD = 2688
CONV_CH = 512

MOE_ROWS = 256
ROW_STRIDE = MOE_ROWS + SUBLANES


def _cparams(n_grid, vmem_mib):
    return pltpu.CompilerParams(dimension_semantics=("arbitrary",) * n_grid,
                                vmem_limit_bytes=vmem_mib * MIB)


def _dot(a, b):
    return jnp.dot(a, b, preferred_element_type=F32)


def _dot_nt(a, b):
    return lax.dot_general(a, b, NT_DIMS, preferred_element_type=F32)


def _silu(x):
    return x * jax.nn.sigmoid(x)


def _softplus(x):
    return jnp.maximum(x, 0.0) + jnp.log1p(jnp.exp(-jnp.abs(x)))


def _mod_spec(modp, tiles_per_seq):
    if modp.shape[0] == 1:
        return pl.BlockSpec((1, 6, D_MODEL), lambda i: (0, 0, 0))
    return pl.BlockSpec((1, 6, D_MODEL), lambda i: (i // tiles_per_seq, 0, 0))


def _group_sum(x, g):
    hi = x.astype(BF16)
    lo = (x - hi.astype(F32)).astype(BF16)
    return _dot(hi, g) + _dot(lo, g)


def _swap16(x):
    n = x.shape[1]
    lane = lax.broadcasted_iota(I32, x.shape, 1)
    return jnp.where((lane & 16) == 0, pltpu.roll(x, n - 16, 1), pltpu.roll(x, 16, 1))


def _layer_norm(h, w, b):
    mu = jnp.mean(h, axis=-1, keepdims=True)
    d = h - mu
    var = jnp.mean(d * d, axis=-1, keepdims=True)
    return d * lax.rsqrt(var + 1e-5) * w + b


def _mod_kernel(cond_ref, w_ref, b_ref, o_ref):
    c = cond_ref[...]
    o_ref[0] = jnp.dot(_silu(c), w_ref[0], precision=HIGHEST, preferred_element_type=F32) + b_ref[0]


def _modulation(cond8, w_mod, b_mod):
    tn = 1536
    n_col = w_mod.shape[2] // tn
    return pl.pallas_call(
        _mod_kernel,
        grid=(DEPTH, n_col),
        in_specs=[pl.BlockSpec((8, D_MODEL), lambda l, j: (0, 0)),
                  pl.BlockSpec((1, D_MODEL, tn), lambda l, j: (l, 0, j)),
                  pl.BlockSpec((1, 1, tn), lambda l, j: (l, 0, j))],
        out_specs=pl.BlockSpec((1, 8, tn), lambda l, j: (l, 0, j)),
        out_shape=jax.ShapeDtypeStruct((DEPTH, 8, w_mod.shape[2]), F32),
        compiler_params=_cparams(2, 32),
        name="modulation",
    )(cond8, w_mod, b_mod.reshape(DEPTH, 1, -1))


def _proj_in_kernel(*refs, rope, emit_cache):
    x_ref, mod_ref, w_ref, qnw_ref, knw_ref, g_ref = refs[:6]
    pos = 6
    if rope:
        cos_ref, sin_ref = refs[pos:pos + 2]
        pos += 2
    q_ref, k_ref, v_ref, ret_ref, ssd_ref = refs[pos:pos + 5]
    pos += 5
    u = x_ref[...] * (1.0 + mod_ref[0, 1:2, :]) + mod_ref[0, 0:1, :]
    ub = u.astype(BF16)
    q = _dot(ub, w_ref[:, 0:512])
    k = _dot(ub, w_ref[:, 512:640])
    v = _dot(ub, w_ref[:, 640:768])
    g = g_ref[...]
    inv_hd = 1.0 / HEAD_DIM
    qn = q * lax.rsqrt(_group_sum(q * q, g) * inv_hd + 1e-6) * qnw_ref[...]
    kn = k * lax.rsqrt(_group_sum(k * k, g[0:LANES, 0:LANES]) * inv_hd + 1e-6) * knw_ref[...]
    if emit_cache:
        kf_ref, vf_ref = refs[pos:pos + 2]
        kf_ref[...] = kn
        vf_ref[...] = v
    if rope:
        c = cos_ref[...]
        s = sin_ref[...]
        kn = kn * c + _swap16(kn) * s
        qn = qn * jnp.concatenate([c] * 4, axis=1) + _swap16(qn) * jnp.concatenate([s] * 4, axis=1)
    q_ref[...] = (qn * (HEAD_DIM ** -0.5)).astype(BF16)
    k_ref[...] = kn.astype(BF16)
    v_ref[...] = v.astype(BF16)
    ret_ref[...] = _dot(ub, w_ref[:, RET_OFF:RET_OFF + RET_W])
    ssd_ref[...] = _dot(ub, w_ref[:, SSD_OFF:SSD_OFF + SSD_W])


def _proj_in(x, modp, w_in_p, qnw, knw, g512, rope_tabs, layer, seq_len, tt, emit_cache):
    t = x.shape[0]
    tiles_per_seq = seq_len // tt
    rope = rope_tabs is not None
    row = lambda i: (i, 0)
    const2 = lambda i: (0, 0)
    in_specs = [pl.BlockSpec((tt, D_MODEL), row),
                _mod_spec(modp, tiles_per_seq),
                pl.BlockSpec((None, D_MODEL, IN_W_PAD), lambda i: (layer, 0, 0)),
                pl.BlockSpec((None, 1, 512), lambda i: (layer, 0, 0)),
                pl.BlockSpec((None, 1, LANES), lambda i: (layer, 0, 0)),
                pl.BlockSpec((512, 512), const2)]
    args = [x, modp, w_in_p, qnw, knw, g512]
    if rope:
        tab = pl.BlockSpec((tt, LANES), lambda i: (i % tiles_per_seq, 0))
        in_specs += [tab, tab]
        args += list(rope_tabs)
    out_shape = [jax.ShapeDtypeStruct((t, 512), BF16), jax.ShapeDtypeStruct((t, LANES), BF16),
                 jax.ShapeDtypeStruct((t, LANES), BF16), jax.ShapeDtypeStruct((t, RET_W), F32),
                 jax.ShapeDtypeStruct((t, SSD_W), F32)]
    out_specs = [pl.BlockSpec((tt, 512), row), pl.BlockSpec((tt, LANES), row), pl.BlockSpec((tt, LANES), row),
                 pl.BlockSpec((tt, RET_W), row), pl.BlockSpec((tt, SSD_W), row)]
    if emit_cache:
        out_shape += [jax.ShapeDtypeStruct((t, LANES), F32)] * 2
        out_specs += [pl.BlockSpec((tt, LANES), row)] * 2
    return pl.pallas_call(
        functools.partial(_proj_in_kernel, rope=rope, emit_cache=emit_cache),
        grid=(t // tt,), in_specs=in_specs, out_specs=out_specs, out_shape=out_shape,
        compiler_params=_cparams(1, 48), name="proj_in",
    )(*args)


def _attn_kernel(*refs, with_cache):
    q_ref, k_ref, v_ref = refs[:3]
    o_ref = refs[-1]
    lane = lax.broadcasted_iota(I32, (1, LANES), 1)
    first = lane < HEAD_DIM
    parts = [(k_ref[...], v_ref[...])]
    if with_cache:
        parts.append((refs[3][...].astype(BF16), refs[4][...].astype(BF16)))
    zero = jnp.zeros((), BF16)
    kmask = [[jnp.where(first, kk, zero), jnp.where(first, zero, kk)] for kk, _ in parts]
    for j in range(ATT_HEADS // 2):
        qb = q_ref[:, LANES * j:LANES * (j + 1)]
        outs = []
        for g in range(ATT_KV_HEADS):
            ss = [_dot_nt(qb, km[g]) for km in kmask]
            m = ss[0].max(axis=-1, keepdims=True)
            for s in ss[1:]:
                m = jnp.maximum(m, s.max(axis=-1, keepdims=True))
            acc = None
            den = None
            for s, (_, vv) in zip(ss, parts):
                e = jnp.exp(s - m)
                d = e.sum(axis=-1, keepdims=True)
                o = _dot(e.astype(BF16), vv)
                acc = o if acc is None else acc + o
                den = d if den is None else den + d
            outs.append(acc / den)
        o_ref[:, LANES * j:LANES * (j + 1)] = jnp.where(first, outs[0], outs[1]).astype(BF16)


def _attention(q, k, v, cache, layer, n_seq, seq_len, tq):
    t = q.shape[0]
    nq = seq_len // tq
    in_specs = [pl.BlockSpec((tq, 512), lambda b, i: (b * nq + i, 0)),
                pl.BlockSpec((seq_len, LANES), lambda b, i: (b, 0)),
                pl.BlockSpec((seq_len, LANES), lambda b, i: (b, 0))]
    args = [q, k, v]
    if cache is not None:
        past = cache[0].shape[2]
        cspec = pl.BlockSpec((None, None, past, LANES), lambda b, i: (b, layer, 0, 0))
        in_specs += [cspec, cspec]
        args += list(cache)
    return pl.pallas_call(
        functools.partial(_attn_kernel, with_cache=cache is not None),
        grid=(n_seq, nq), in_specs=in_specs,
        out_specs=pl.BlockSpec((tq, 512), lambda b, i: (b * nq + i, 0)),
        out_shape=jax.ShapeDtypeStruct((t, 512), BF16),
        compiler_params=_cparams(2, 48), name="attention",
    )(*args)


def _scan_kernel(*refs, n, chunk, has_init, emit_state):
    (ret_ref, ssd_ref, hp_ref, hn_ref, cw_ref, cb_ref, dtb_ref, alog_ref, rld_ref,
     rnw_ref, dl_ref, snw_ref, g_ref) = refs[:13]
    pos = 13
    if has_init:
        s0r_ref, s0s_ref = refs[pos:pos + 2]
        pos += 2
    reto_ref, ssdo_ref = refs[pos:pos + 2]
    pos += 2
    if emit_state:
        str_ref, sts_ref = refs[pos:pos + 2]
        pos += 2
    of_ref, st_ref = refs[pos:pos + 2]

    C = chunk
    s = pl.program_id(1)
    bwd = s >= n
    c = jnp.where(bwd, 2 * n - 1 - s, s)

    def init_state(direction):
        if has_init:
            st_ref[0:2] = s0s_ref[0, direction]
            st_ref[2:4] = s0r_ref[0, direction]
        else:
            st_ref[...] = jnp.zeros(st_ref.shape, F32)

    @pl.when(s == 0)
    def _():
        init_state(0)

    @pl.when(s == n)
    def _():
        init_state(1)

    lane = lax.broadcasted_iota(I32, (1, LANES), 1)
    first = lane < HEAD_DIM
    ret = ret_ref[...]
    ssd = ssd_ref[...]

    xin = ssd[:, 0:CONV_CH]
    rowi = lax.broadcasted_iota(I32, (C, 1), 0)
    prev_row = jnp.where(c == 0, 0.0, hp_ref[SUBLANES - 1:SUBLANES, :])
    next_row = jnp.where(c == n - 1, 0.0, hn_ref[0:1, :])
    xp = jnp.where(rowi == 0, prev_row, pltpu.roll(xin, 1, 0))
    xn = jnp.where(rowi == C - 1, next_row, pltpu.roll(xin, C - 1, 0))
    xbc = _silu(xp * cw_ref[0:1, :] + xin * cw_ref[1:2, :] + xn * cw_ref[2:3, :] + cb_ref[...])
    sx = xbc[:, 0:256]
    sb = xbc[:, 256:384]
    sc = xbc[:, 384:512]

    def this_dir(a):
        return jnp.where(bwd, pltpu.roll(a, LANES - SSD_HEADS, 1), a)

    dt_all = _softplus(ssd[:, 768:896] + dtb_ref[...])
    la_ssd = this_dir(dt_all * (-jnp.exp(alog_ref[...])))
    dt_dir = this_dir(dt_all)
    lg = jnp.broadcast_to(jnp.log1p(-jnp.exp(rld_ref[...])), (SUBLANES, LANES))
    lg_dir = pltpu.roll(this_dir(lg), SSD_HEADS, 1)[0:1, :]
    la = jnp.where(lane < 4, la_ssd, jnp.where(lane < 8, lg_dir, 0.0))
    ii = lax.broadcasted_iota(I32, (C, C), 0)
    jj = lax.broadcasted_iota(I32, (C, C), 1)
    msk = jnp.where(bwd, jj - ii, ii - jj) >= 0
    cum = jnp.dot(msk.astype(F32), la, precision=HIGHEST, preferred_element_type=F32)
    last = jnp.where(bwd, cum[0:1, :], cum[C - 1:C, :])
    acol = jnp.where(lane < 8, cum, jnp.where(lane < 12, pltpu.roll(dt_dir, 8, 1), 0.0))
    arow = acol.T

    def decay(h, scale):
        diff = cum[:, h:h + 1] - arow[h:h + 1, :]
        return jnp.where(msk, jnp.exp(jnp.where(msk, diff, 0.0)), 0.0) * scale

    def lanes2(a0, a1):
        return jnp.where(first, a0, a1)

    ii2 = lax.broadcasted_iota(I32, (LANES, LANES), 0)
    jj2 = lax.broadcasted_iota(I32, (LANES, LANES), 1)
    blockdiag = ((ii2 < HEAD_DIM) == (jj2 < HEAD_DIM)).astype(F32)

    o_pairs = []
    for kind in ("ssd", "ret"):
        for p in range(2):
            slot = p if kind == "ssd" else 2 + p
            heads = (2 * p, 2 * p + 1)
            hl = [h if kind == "ssd" else 4 + h for h in heads]
            if kind == "ssd":
                grp = first if p == 0 else jnp.logical_not(first)
                qm = jnp.where(grp, sc, 0.0).astype(BF16)
                km = jnp.where(grp, sb, 0.0)
                v128 = sx[:, LANES * p:LANES * (p + 1)]
                sg = _dot_nt(qm, sb.astype(BF16))
                scores = [sg, sg]
                rs = [arow[8 + h:9 + h, :] for h in heads]
                cs = [acol[:, 8 + h:9 + h] for h in heads]
                q_inter = qm
            else:
                q128 = ret[:, LANES * p:LANES * (p + 1)]
                km = ret[:, 256 + LANES * p:256 + LANES * (p + 1)]
                v128 = ret[:, 512 + LANES * p:512 + LANES * (p + 1)]
                kb = km.astype(BF16)
                scores = [_dot_nt(jnp.where(first, q128, 0.0).astype(BF16), kb),
                          _dot_nt(jnp.where(first, 0.0, q128).astype(BF16), kb)]
                rs = [RET_DK ** -0.5] * 2
                cs = [RET_DK ** -0.5] * 2
                q_inter = q128.astype(BF16)
            vb = v128.astype(BF16)
            intra = [_dot((scores[t] * decay(hl[t], rs[t])).astype(BF16), vb) for t in range(2)]
            state = st_ref[slot]
            ecum = lanes2(jnp.exp(cum[:, hl[0]:hl[0] + 1]), jnp.exp(cum[:, hl[1]:hl[1] + 1]))
            o_pairs.append(lanes2(intra[0], intra[1]) + _dot(q_inter, state.astype(BF16)) * ecum)
            wv = lanes2(jnp.exp(last[:, hl[0]:hl[0] + 1] - cum[:, hl[0]:hl[0] + 1]) * cs[0],
                        jnp.exp(last[:, hl[1]:hl[1] + 1] - cum[:, hl[1]:hl[1] + 1]) * cs[1])
            snew = _dot(km.T.astype(BF16), (v128 * wv).astype(BF16))
            if kind == "ret":
                snew = snew * blockdiag
            elast = lanes2(jnp.exp(last[:, hl[0]:hl[0] + 1]), jnp.exp(last[:, hl[1]:hl[1] + 1]))
            st_ref[slot] = elast * state + snew
    o_all = jnp.concatenate(o_pairs, axis=1)
    rows = pl.ds(pl.multiple_of(c * C, C), C)

    @pl.when(jnp.logical_not(bwd))
    def _():
        of_ref[rows, :] = o_all

    @pl.when(bwd)
    def _():
        tot = of_ref[rows, :] + o_all
        g = g_ref[...]
        inv = 1.0 / HEAD_DIM
        for p in range(2):
            o = tot[:, 256 + LANES * p:256 + LANES * (p + 1)]
            d = o - _group_sum(o, g) * inv
            var = _group_sum(d * d, g) * inv
            rn = d * lax.rsqrt(var + 1e-5) * rnw_ref[:, LANES * p:LANES * (p + 1)]
            gate = ret[:, 768 + LANES * p:768 + LANES * (p + 1)]
            reto_ref[:, LANES * p:LANES * (p + 1)] = (rn * _silu(gate)).astype(BF16)
        y = (tot[:, 0:256] + dl_ref[...] * sx) * _silu(ssd[:, 512:768])
        ms = jnp.mean(y * y, axis=-1, keepdims=True)
        ssdo_ref[...] = (y * lax.rsqrt(ms + 1e-6) * snw_ref[...]).astype(BF16)

    if emit_state:
        @pl.when(s == n - 1)
        def _():
            sts_ref[0, 0] = st_ref[0:2]
            str_ref[0, 0] = st_ref[2:4]

        @pl.when(s == 2 * n - 1)
        def _():
            sts_ref[0, 1] = st_ref[0:2]
            str_ref[0, 1] = st_ref[2:4]


def _scan(ret, ssd, pw, layer, init, n_seq, seq_len, chunk, emit_state):
    t = ret.shape[0]
    n = seq_len // chunk
    rows8 = t // SUBLANES

    def cidx(s):
        return jnp.where(s >= n, 2 * n - 1 - s, s)

    def oidx(s):
        return jnp.where(s >= n, 2 * n - 1 - s, n - 1)

    par = lambda w: pl.BlockSpec((None,) + w, lambda b, s: (layer,) + (0,) * len(w))
    in_specs = [pl.BlockSpec((chunk, RET_W), lambda b, s: (b * n + cidx(s), 0)),
                pl.BlockSpec((chunk, SSD_W), lambda b, s: (b * n + cidx(s), 0)),
                pl.BlockSpec((SUBLANES, CONV_CH),
                             lambda b, s: (jnp.maximum((b * seq_len + cidx(s) * chunk) // SUBLANES - 1, 0), 0)),
                pl.BlockSpec((SUBLANES, CONV_CH),
                             lambda b, s: (jnp.minimum((b * seq_len + (cidx(s) + 1) * chunk) // SUBLANES, rows8 - 1), 0)),
                par((3, CONV_CH)), par((1, CONV_CH)), par((1, LANES)), par((1, LANES)), par((1, LANES)),
                par((1, RET_WIDTH)), par((1, SSD_WIDTH)), par((1, SSD_WIDTH)),
                pl.BlockSpec((LANES, LANES), lambda b, s: (0, 0))]
    args = [ret, ssd, ssd, ssd, pw["conv_w"], pw["conv_b"], pw["dt_bias"], pw["a_log"], pw["ret_ld"],
            pw["ret_nw"], pw["d_lane"], pw["ssd_nw"], pw["g128"]]
    st_spec = pl.BlockSpec((1, 2, 2, LANES, LANES), lambda b, s: (b, 0, 0, 0, 0))
    if init is not None:
        in_specs += [st_spec, st_spec]
        args += list(init)
    out_specs = [pl.BlockSpec((chunk, RET_WIDTH), lambda b, s: (b * n + oidx(s), 0)),
                 pl.BlockSpec((chunk, SSD_WIDTH), lambda b, s: (b * n + oidx(s), 0))]
    out_shape = [jax.ShapeDtypeStruct((t, RET_WIDTH), BF16), jax.ShapeDtypeStruct((t, SSD_WIDTH), BF16)]
    if emit_state:
        out_specs += [st_spec, st_spec]
        out_shape += [jax.ShapeDtypeStruct((n_seq, 2, 2, LANES, LANES), F32)] * 2
    return pl.pallas_call(
        functools.partial(_scan_kernel, n=n, chunk=chunk, has_init=init is not None, emit_state=emit_state),
        grid=(n_seq, 2 * n), in_specs=in_specs, out_specs=out_specs, out_shape=out_shape,
        scratch_shapes=[pltpu.VMEM((seq_len, 512), F32), pltpu.VMEM((4, LANES, LANES), F32)],
        compiler_params=_cparams(2, 48), name="scan",
    )(*args)


def _proj_out_kernel(x_ref, att_ref, ro_ref, so_ref, mod_ref, wo_ref, lnw_ref, lnb_ref, rwh_ref, rwl_ref,
                     x1_ref, u2_ref, u2p_ref, lg_ref):
    mix = (_dot(att_ref[...], wo_ref[0:512, :]) + _dot(ro_ref[...], wo_ref[512:768, :])
           + _dot(so_ref[...], wo_ref[768:1024, :]))
    h = DEEPNORM_ALPHA * x_ref[...] + mod_ref[0, 2:3, :] * mix
    x1 = _layer_norm(h, lnw_ref[...], lnb_ref[...])
    x1_ref[...] = x1
    u2 = x1 * (1.0 + mod_ref[0, 4:5, :]) + mod_ref[0, 3:4, :]
    hi = u2.astype(BF16)
    u2_ref[...] = hi
    hif = hi.astype(F32)
    bits = lax.bitcast_convert_type(hif, jnp.uint32)
    half = D_MODEL // 2
    packed = lax.shift_right_logical(bits[:, 0:half], jnp.uint32(16)) | (bits[:, half:] & jnp.uint32(0xFFFF0000))
    u2p_ref[...] = lax.bitcast_convert_type(packed, I32)
    lo = (u2 - hif).astype(BF16)
    rwh = rwh_ref[...]
    lg_ref[...] = _dot_nt(rwh, hi) + _dot_nt(rwh, lo) + _dot_nt(rwl_ref[...], hi)


def _proj_out(x, att, ro, so, modp, w_out_p, ln_w, ln_b, rwh, rwl, layer, seq_len, tt):
    t = x.shape[0]
    tiles_per_seq = seq_len // tt
    row = lambda i: (i, 0)
    lsel = lambda *w: pl.BlockSpec((None,) + w, lambda i: (layer,) + (0,) * len(w))
    in_specs = [pl.BlockSpec((tt, D_MODEL), row), pl.BlockSpec((tt, 512), row),
                pl.BlockSpec((tt, RET_WIDTH), row), pl.BlockSpec((tt, SSD_WIDTH), row),
                _mod_spec(modp, tiles_per_seq),
                lsel(D_MODEL, D_MODEL),
                pl.BlockSpec((None, None, 1, D_MODEL), lambda i: (layer, 0, 0, 0)),
                pl.BlockSpec((None, None, 1, D_MODEL), lambda i: (layer, 0, 0, 0)),
                lsel(N_EXPERTS, D_MODEL), lsel(N_EXPERTS, D_MODEL)]
    out_shape = [jax.ShapeDtypeStruct((t, D_MODEL), F32), jax.ShapeDtypeStruct((t, D_MODEL), BF16),
                 jax.ShapeDtypeStruct((t, D_MODEL // 2), I32), jax.ShapeDtypeStruct((N_EXPERTS, t), F32)]
    out_specs = [pl.BlockSpec((tt, D_MODEL), row), pl.BlockSpec((tt, D_MODEL), row),
                 pl.BlockSpec((tt, D_MODEL // 2), row), pl.BlockSpec((N_EXPERTS, tt), lambda i: (0, i))]
    return pl.pallas_call(
        _proj_out_kernel, grid=(t // tt,), in_specs=in_specs, out_specs=out_specs, out_shape=out_shape,
        compiler_params=_cparams(1, 48), name="proj_out",
    )(x, att, ro, so, modp, w_out_p, ln_w, ln_b, rwh, rwl)


def _route_kernel(lg_ref, bias_ref, idx_ref, w_ref, rank_ref, cnt_ref, carry_ref):
    i = pl.program_id(0)

    @pl.when(i == 0)
    def _():
        carry_ref[...] = jnp.zeros(carry_ref.shape, F32)

    scores = jax.nn.sigmoid(lg_ref[...])
    sel = scores + bias_ref[...]
    n_e, tr = sel.shape
    neg = -jnp.inf
    sub = lax.broadcasted_iota(I32, (GROUP_SIZE, tr), 0).astype(F32)
    blocks = [sel[GROUP_SIZE * g:GROUP_SIZE * (g + 1), :] for g in range(ROUTER_GROUPS)]
    gscore = []
    for blk in blocks:
        m1 = blk.max(axis=0, keepdims=True)
        pos1 = jnp.min(jnp.where(blk == m1, sub, float(GROUP_SIZE)), axis=0, keepdims=True)
        m2 = jnp.max(jnp.where(sub == pos1, neg, blk), axis=0, keepdims=True)
        gscore.append(m1 + m2)
    masked = []
    for g in range(ROUTER_GROUPS):
        beat = jnp.zeros_like(gscore[g])
        for h in range(ROUTER_GROUPS):
            if h == g:
                continue
            wins = (gscore[h] >= gscore[g]) if h < g else (gscore[h] > gscore[g])
            beat = beat + wins.astype(F32)
        masked.append(jnp.where(beat < float(ROUTER_TOPK_GROUPS), blocks[g], neg))
    cur = jnp.concatenate(masked, axis=0)
    eio = lax.broadcasted_iota(I32, (n_e, tr), 0).astype(F32)
    chosen = jnp.zeros((n_e, tr), F32)
    picks = []
    for _ in range(TOP_K):
        m = cur.max(axis=0, keepdims=True)
        ik = jnp.min(jnp.where(cur == m, eio, float(n_e)), axis=0, keepdims=True)
        hit = eio == ik
        picks.append((ik, hit, jnp.sum(jnp.where(hit, scores, 0.0), axis=0, keepdims=True)))
        cur = jnp.where(hit, neg, cur)
        chosen = jnp.where(hit, 1.0, chosen)
    wsum = picks[0][2]
    for _, _, wk in picks[1:]:
        wsum = wsum + wk
    r_ = lax.broadcasted_iota(I32, (tr, tr), 0)
    c_ = lax.broadcasted_iota(I32, (tr, tr), 1)
    chosen_b = chosen.astype(BF16)
    before = _dot(chosen_b, (r_ < c_).astype(BF16)) + carry_ref[:, 0:1]
    for k, (ik, hit, wk) in enumerate(picks):
        idx_ref[k:k + 1, :] = ik.astype(I32)
        w_ref[k:k + 1, :] = wk / wsum * ROUTED_SCALE
        rank_ref[k:k + 1, :] = jnp.sum(jnp.where(hit, before, 0.0), axis=0, keepdims=True).astype(I32)
    carry_ref[...] = carry_ref[...] + _dot(chosen_b, jnp.ones((tr, LANES), BF16))
    cnt_ref[...] = carry_ref[...]


def _route(lg_t, bias, layer, tr):
    t = lg_t.shape[1]
    col = lambda i: (0, i)
    return pl.pallas_call(
        _route_kernel, grid=(t // tr,),
        in_specs=[pl.BlockSpec((N_EXPERTS, tr), col),
                  pl.BlockSpec((None, N_EXPERTS, 1), lambda i: (layer, 0, 0))],
        out_specs=[pl.BlockSpec((TOP_K, tr), col), pl.BlockSpec((TOP_K, tr), col), pl.BlockSpec((TOP_K, tr), col),
                   pl.BlockSpec((N_EXPERTS, LANES), lambda i: (0, 0))],
        out_shape=[jax.ShapeDtypeStruct((TOP_K, t), I32), jax.ShapeDtypeStruct((TOP_K, t), F32),
                   jax.ShapeDtypeStruct((TOP_K, t), I32), jax.ShapeDtypeStruct((N_EXPERTS, LANES), F32)],
        scratch_shapes=[pltpu.VMEM((N_EXPERTS, LANES), F32)],
        compiler_params=_cparams(1, 48), name="route",
    )(lg_t, bias)


def _moe_up_kernel(be_ref, nu_ref, tok_ref, x_hbm, wg_ref, wu_ref, h_ref, xv_ref, xt_ref, sem, *, n_tok, rows):
    i = pl.program_id(0)
    n_chunk = 4

    @pl.when(i == 0)
    def _():
        cp = pltpu.make_async_copy(x_hbm, xv_ref.at[pl.ds(0, n_tok * n_chunk), :], sem)
        cp.start()
        cp.wait()
        xv_ref[pl.ds(n_tok * n_chunk, SUBLANES), :] = jnp.zeros((SUBLANES, LANES), I32)

    @pl.when(i < nu_ref[0])
    def _():
        stride = rows + SUBLANES
        for r in range(rows):
            t = tok_ref[0, 0, r]
            slab = xv_ref[pl.ds(pl.multiple_of(t * n_chunk, n_chunk), n_chunk), :]
            xt_ref[pl.ds(r, n_chunk, stride=stride), :] = slab
        lo, hi = [], []
        for c in range(n_chunk):
            w = xt_ref[pl.ds(c * stride, rows), :]
            lo.append(lax.bitcast_convert_type(w << 16, F32).astype(BF16))
            hi.append(lax.bitcast_convert_type(w & jnp.int32(-65536), F32).astype(BF16))
        xb = jnp.concatenate(lo + hi, axis=1)
        gate = _dot(xb, wg_ref[...].astype(BF16))
        up = _dot(xb, wu_ref[...].astype(BF16))
        h_ref[...] = (_silu(gate) * up).astype(BF16)

    @pl.when(i >= nu_ref[0])
    def _():
        h_ref[...] = jnp.zeros(h_ref.shape, BF16)


def _moe_up(block_e, n_used, row_tok, x_packed, w_gate, w_up, layer, n_tok, rows):
    nb = block_e.shape[0]
    n_chunk = 4
    grid_spec = pltpu.PrefetchScalarGridSpec(
        num_scalar_prefetch=2, grid=(nb,),
        in_specs=[pl.BlockSpec((1, 1, rows), lambda i, be, nu: (i, 0, 0), memory_space=pltpu.SMEM),
                  pl.BlockSpec(memory_space=pl.ANY),
                  pl.BlockSpec((None, None, D_MODEL, EXPERT_FF), lambda i, be, nu: (layer, be[i], 0, 0)),
                  pl.BlockSpec((None, None, D_MODEL, EXPERT_FF), lambda i, be, nu: (layer, be[i], 0, 0))],
        out_specs=pl.BlockSpec((rows, EXPERT_FF), lambda i, be, nu: (i, 0)),
        scratch_shapes=[pltpu.VMEM((n_tok * n_chunk + SUBLANES, LANES), I32),
                        pltpu.VMEM((n_chunk * (rows + SUBLANES), LANES), I32),
                        pltpu.SemaphoreType.DMA(())])
    return pl.pallas_call(
        functools.partial(_moe_up_kernel, n_tok=n_tok, rows=rows),
        grid_spec=grid_spec, out_shape=jax.ShapeDtypeStruct((nb * rows, EXPERT_FF), BF16),
        compiler_params=_cparams(1, 48), name="moe_up",
    )(block_e, n_used, row_tok.reshape(nb, 1, rows), x_packed.reshape(n_tok * n_chunk, LANES), w_gate, w_up)


def _moe_down_kernel(be_ref, nu_ref, tok_ref, rw_ref, h_ref, wd_ref, out_hbm, acc_ref, y_ref, sem,
                     *, n_tok, rows, batch):
    i = pl.program_id(0)
    n_chunk = D_MODEL // LANES

    @pl.when(i == 0)
    def _():
        acc_ref[...] = jnp.zeros(acc_ref.shape, F32)

    @pl.when(i < nu_ref[0])
    def _():
        stride = rows + SUBLANES
        y = _dot(h_ref[...], wd_ref[...].astype(BF16))
        for c in range(n_chunk):
            y_ref[pl.ds(c * stride, rows), :] = y[:, LANES * c:LANES * (c + 1)]
        for r0 in range(0, rows, batch):
            pending = []
            for r in range(r0, r0 + batch):
                t = tok_ref[0, 0, r]
                dst = pl.ds(pl.multiple_of(t * n_chunk, n_chunk), n_chunk)
                pending.append((dst, acc_ref[dst, :] + rw_ref[0, 0, r] * y_ref[pl.ds(r, n_chunk, stride=stride), :]))
            for dst, val in pending:
                acc_ref[dst, :] = val

    @pl.when(i == pl.num_programs(0) - 1)
    def _():
        cp = pltpu.make_async_copy(acc_ref.at[pl.ds(0, n_tok * n_chunk), :], out_hbm, sem)
        cp.start()
        cp.wait()


def _moe_down(block_e, n_used, row_tok, row_w, h_sorted, w_down, layer, n_tok, rows):
    nb = block_e.shape[0]
    n_chunk = D_MODEL // LANES
    smem_row = pl.BlockSpec((1, 1, rows), lambda i, be, nu: (i, 0, 0), memory_space=pltpu.SMEM)
    grid_spec = pltpu.PrefetchScalarGridSpec(
        num_scalar_prefetch=2, grid=(nb,),
        in_specs=[smem_row, smem_row,
                  pl.BlockSpec((rows, EXPERT_FF), lambda i, be, nu: (i, 0)),
                  pl.BlockSpec((None, None, EXPERT_FF, D_MODEL), lambda i, be, nu: (layer, be[i], 0, 0))],
        out_specs=pl.BlockSpec(memory_space=pl.ANY),
        scratch_shapes=[pltpu.VMEM(((n_tok + 1) * n_chunk, LANES), F32),
                        pltpu.VMEM((n_chunk * (rows + SUBLANES), LANES), F32),
                        pltpu.SemaphoreType.DMA(())])
    out = pl.pallas_call(
        functools.partial(_moe_down_kernel, n_tok=n_tok, rows=rows, batch=8),
        grid_spec=grid_spec, out_shape=jax.ShapeDtypeStruct((n_tok * n_chunk, LANES), F32),
        compiler_params=_cparams(1, 56), name="moe_down",
    )(block_e, n_used, row_tok.reshape(nb, 1, rows), row_w.reshape(nb, 1, rows), h_sorted, w_down)
    return out.reshape(n_tok, D_MODEL)


def _dispatch_plan(idx_t, w_t, rank_t, counts, rows):
    n_tok = idx_t.shape[1]
    n_blocks = -(-(n_tok * TOP_K + N_EXPERTS * (rows - 1)) // rows)
    cnt = counts[:, 0].astype(I32)
    padded = (cnt + rows - 1) // rows * rows
    pend = jnp.cumsum(padded)
    pstart = pend - padded
    dest = (pstart[idx_t] + rank_t).reshape(-1)
    tok = jnp.broadcast_to(jnp.arange(n_tok, dtype=I32)[None, :], idx_t.shape).reshape(-1)
    row_tok = jnp.full((n_blocks * rows,), n_tok, I32).at[dest].set(tok)
    row_w = jnp.zeros((n_blocks * rows,), F32).at[dest].set(w_t.reshape(-1))
    block_e = jnp.minimum(jnp.searchsorted(pend, jnp.arange(n_blocks, dtype=I32) * rows, side="right"),
                          N_EXPERTS - 1).astype(I32)
    n_used = (pend[-1:] // rows).astype(I32)
    return block_e, n_used, row_tok, row_w


def _ffn_out_kernel(x1_ref, u2_ref, routed_ref, mod_ref, wgu_ref, wd_ref, lnw_ref, lnb_ref, o_ref):
    gu = _dot(u2_ref[...], wgu_ref[...])
    shared = _dot((_silu(gu[:, 0:SHARED_FF]) * gu[:, SHARED_FF:]).astype(BF16), wd_ref[...])
    h = DEEPNORM_ALPHA * x1_ref[...] + mod_ref[0, 5:6, :] * (routed_ref[...] + shared)
    o_ref[...] = _layer_norm(h, lnw_ref[...], lnb_ref[...])


def _ffn_out(x1, u2, routed, modp, sh_gu, sh_d, ln_w, ln_b, layer, seq_len, tt):
    t = x1.shape[0]
    tiles_per_seq = seq_len // tt
    row = lambda i: (i, 0)
    lsel = lambda *w: pl.BlockSpec((None,) + w, lambda i: (layer,) + (0,) * len(w))
    ln_spec = pl.BlockSpec((None, None, 1, D_MODEL), lambda i: (layer, 1, 0, 0))
    return pl.pallas_call(
        _ffn_out_kernel, grid=(t // tt,),
        in_specs=[pl.BlockSpec((tt, D_MODEL), row), pl.BlockSpec((tt, D_MODEL), row), pl.BlockSpec((tt, D_MODEL), row),
                  _mod_spec(modp, tiles_per_seq),
                  lsel(D_MODEL, 2 * SHARED_FF), lsel(SHARED_FF, D_MODEL), ln_spec, ln_spec],
        out_specs=pl.BlockSpec((tt, D_MODEL), row),
        out_shape=jax.ShapeDtypeStruct((t, D_MODEL), F32),
        compiler_params=_cparams(1, 48), name="ffn_out",
    )(x1, u2, routed, modp, sh_gu, sh_d, ln_w, ln_b)


def _q_perm():
    cols = []
    for j in range(ATT_HEADS // 2):
        for half in range(2):
            h = j + (ATT_HEADS // 2) * half
            cols.extend(range(h * HEAD_DIM, (h + 1) * HEAD_DIM))
    return np.asarray(cols, np.int32)


def _prepare_params(p):
    off = np.concatenate([[0], np.cumsum(IN_SIZES)])
    seg = lambda k: np.arange(off[k], off[k + 1], dtype=np.int32)
    qp = _q_perm()
    cols = np.concatenate([qp, seg(1), seg(2), seg(3), seg(4), seg(5), seg(6), seg(7), seg(9), seg(10), seg(8), seg(11)])
    w_in = jnp.take(p["w_in"], jnp.asarray(cols), axis=2)
    w_in = jnp.pad(w_in, ((0, 0), (0, 0), (0, IN_W_PAD - w_in.shape[2]))).astype(BF16)
    rows = np.concatenate([qp, np.arange(ATT_WIDTH, D_MODEL, dtype=np.int32)])
    w_out = jnp.take(p["w_out"], jnp.asarray(rows), axis=1).astype(BF16)
    rwt = jnp.swapaxes(p["router_w"], 1, 2)
    rwh = rwt.astype(BF16)
    rwl = (rwt - rwh.astype(F32)).astype(BF16)

    def lane_pad(a, fill=0.0):
        a = a.reshape(DEPTH, 1, -1)
        return jnp.pad(a, ((0, 0), (0, 0), (0, LANES - a.shape[2])), constant_values=fill)

    gi = np.arange(512) // HEAD_DIM
    g512 = jnp.asarray(gi[:, None] == gi[None, :], BF16)
    return {
        "w_in": w_in, "w_out": w_out, "rwh": rwh, "rwl": rwl,
        "qnw": jnp.tile(p["q_norm_w"], (1, ATT_HEADS)).reshape(DEPTH, 1, 512),
        "knw": jnp.tile(p["k_norm_w"], (1, ATT_KV_HEADS)).reshape(DEPTH, 1, LANES),
        "g512": g512, "g128": g512[0:LANES, 0:LANES],
        "conv_w": p["ssd_conv_w"], "conv_b": p["ssd_conv_b"].reshape(DEPTH, 1, CONV_CH),
        "dt_bias": lane_pad(p["ssd_dt_bias"]), "a_log": lane_pad(p["ssd_a_log"]),
        "ret_ld": lane_pad(p["ret_log_decay"], -1.0),
        "ret_nw": p["ret_norm_w"].reshape(DEPTH, 1, RET_WIDTH),
        "d_lane": jnp.repeat(p["ssd_d"], HEAD_DIM, axis=1).reshape(DEPTH, 1, SSD_WIDTH),
        "ssd_nw": p["ssd_norm_w"].reshape(DEPTH, 1, SSD_WIDTH),
        "ln_w": p["ln_w"].reshape(DEPTH, 2, 1, D_MODEL), "ln_b": p["ln_b"].reshape(DEPTH, 2, 1, D_MODEL),
        "router_bias": p["router_bias"].reshape(DEPTH, N_EXPERTS, 1),
        "sh_gu": jnp.concatenate([p["sh_w_gate"], p["sh_w_up"]], axis=2).astype(BF16),
        "sh_d": p["sh_w_down"].astype(BF16),
        "exp_w_gate": p["exp_w_gate"], "exp_w_up": p["exp_w_up"], "exp_w_down": p["exp_w_down"],
    }


def _rope_tables(n_tokens):
    t = jnp.arange(n_tokens, dtype=I32)
    rows = (t // GRID_W).astype(F32)
    cols = (t % GRID_W).astype(F32)
    inv = ROPE_THETA ** (-jnp.arange(ROPE_AXIS_FREQS, dtype=F32) / ROPE_AXIS_FREQS)
    ar = rows[:, None] * inv
    ac = cols[:, None] * inv
    cos = jnp.concatenate([jnp.cos(ar), jnp.cos(ar), jnp.cos(ac), jnp.cos(ac)], axis=1)
    sin = jnp.concatenate([-jnp.sin(ar), jnp.sin(ar), -jnp.sin(ac), jnp.sin(ac)], axis=1)
    return jnp.tile(cos, (1, 2)), jnp.tile(sin, (1, 2))


def _pair_states(st, kind):
    b = st.shape[0]
    out = jnp.zeros((b, 2, 2, LANES, LANES), F32)
    for pr in range(2):
        for tt in range(2):
            r0 = HEAD_DIM * (tt if kind == "ret" else pr)
            out = out.at[:, :, pr, r0:r0 + HEAD_DIM, HEAD_DIM * tt:HEAD_DIM * (tt + 1)].set(st[:, :, 2 * pr + tt])
    return out


def _unpair_states(m, kind):
    heads = []
    for pr in range(2):
        for tt in range(2):
            r0 = HEAD_DIM * (tt if kind == "ret" else pr)
            heads.append(m[:, :, pr, r0:r0 + HEAD_DIM, HEAD_DIM * tt:HEAD_DIM * (tt + 1)])
    return jnp.stack(heads, axis=2)


def _trunk_layer(x, modp, pw, layer, n_seq, seq_len, rope_tabs, cache, init, cfg):
    n_tok = x.shape[0]
    ctx = cache is None
    outs = _proj_in(x, modp, pw["w_in"], pw["qnw"], pw["knw"], pw["g512"], rope_tabs, layer, seq_len,
                    cfg["tt"], emit_cache=ctx)
    q, k, v, ret, ssd = outs[:5]
    att = _attention(q, k, v, cache, layer, n_seq, seq_len, cfg["tq"])
    sc = _scan(ret, ssd, pw, layer, init, n_seq, seq_len, cfg["chunk"], emit_state=ctx)
    x1, u2, u2p, lg_t = _proj_out(x, att, sc[0], sc[1], modp, pw["w_out"], pw["ln_w"], pw["ln_b"],
                                  pw["rwh"], pw["rwl"], layer, seq_len, cfg["tt"])
    idx_t, w_t, rank_t, counts = _route(lg_t, pw["router_bias"], layer, cfg["tr"])
    rows = cfg["rows"]
    block_e, n_used, row_tok, row_w = _dispatch_plan(idx_t, w_t, rank_t, counts, rows)
    h_sorted = _moe_up(block_e, n_used, row_tok, u2p, pw["exp_w_gate"], pw["exp_w_up"], layer, n_tok, rows)
    routed = _moe_down(block_e, n_used, row_tok, row_w, h_sorted, pw["exp_w_down"], layer, n_tok, rows)
    x_new = _ffn_out(x1, u2, routed, modp, pw["sh_gu"], pw["sh_d"], pw["ln_w"], pw["ln_b"], layer, seq_len, cfg["tt"])
    new_ctx = (outs[5], outs[6], sc[2], sc[3]) if ctx else None
    return x_new, new_ctx


def kernel(x_prompt, x_sample, cache_k, cache_v, state_ret, state_ssd, c, c_ctx,
           w_mod, b_mod, w_in, q_norm_w, k_norm_w, ret_log_decay, ret_norm_w,
           ssd_conv_w, ssd_conv_b, ssd_dt_bias, ssd_a_log, ssd_d, ssd_norm_w, w_out,
           ln_w, ln_b, router_w, router_bias, exp_w_gate, exp_w_up, exp_w_down,
           sh_w_gate, sh_w_up, sh_w_down):
    pw = _prepare_params({
        "w_in": w_in, "q_norm_w": q_norm_w, "k_norm_w": k_norm_w, "ret_log_decay": ret_log_decay,
        "ret_norm_w": ret_norm_w, "ssd_conv_w": ssd_conv_w, "ssd_conv_b": ssd_conv_b, "ssd_dt_bias": ssd_dt_bias,
        "ssd_a_log": ssd_a_log, "ssd_d": ssd_d, "ssd_norm_w": ssd_norm_w, "w_out": w_out, "ln_w": ln_w, "ln_b": ln_b,
        "router_w": router_w, "router_bias": router_bias, "exp_w_gate": exp_w_gate, "exp_w_up": exp_w_up,
        "exp_w_down": exp_w_down, "sh_w_gate": sh_w_gate, "sh_w_up": sh_w_up, "sh_w_down": sh_w_down})
    nb, seq, d = x_prompt.shape
    nd, dseq, _ = x_sample.shape
    cond = jnp.concatenate([c_ctx[None], c, jnp.zeros((8 - 1 - nd, d), F32)], axis=0)
    mod = _modulation(cond, w_mod, b_mod)
    rope_tabs = _rope_tables(dseq)
    past = cache_k.shape[2]
    ck = cache_k.reshape(nd, DEPTH, past, LANES)
    cv = cache_v.reshape(nd, DEPTH, past, LANES)
    cfg_ctx = {"tt": 512, "tq": seq, "chunk": seq, "tr": 512, "rows": MOE_ROWS}
    cfg_lat = {"tt": 512, "tq": 256, "chunk": 256, "tr": 512, "rows": MOE_ROWS}

    xc = x_prompt.reshape(nb * seq, d)
    xs = x_sample.reshape(nd * dseq, d)
    ks, vs, srs, sss = [], [], [], []
    for l in range(DEPTH):
        mod_ctx = mod[l, 0:1].reshape(1, 6, d)
        mod_lat = mod[l, 1:1 + nd].reshape(nd, 6, d)
        xc, (k_l, v_l, sr_l, ss_l) = _trunk_layer(xc, mod_ctx, pw, l, nb, seq, None, None, None, cfg_ctx)
        ks.append(k_l.reshape(nb, seq, ATT_KV_HEADS, HEAD_DIM))
        vs.append(v_l.reshape(nb, seq, ATT_KV_HEADS, HEAD_DIM))
        srs.append(_unpair_states(sr_l, "ret"))
        sss.append(_unpair_states(ss_l, "ssd"))
        init = (_pair_states(state_ret[:, l], "ret"), _pair_states(state_ssd[:, l], "ssd"))
        xs, _ = _trunk_layer(xs, mod_lat, pw, l, nd, dseq, rope_tabs, (ck, cv), init, cfg_lat)
    return (xc.reshape(nb, seq, d), xs.reshape(nd, dseq, d), jnp.stack(ks, axis=1), jnp.stack(vs, axis=1),
            jnp.stack(srs, axis=1), jnp.stack(sss, axis=1))
```

```python
import functools

import numpy as np
import jax
import jax.numpy as jnp
from jax import lax
from jax.experimental import pallas as pl
from jax.experimental.pallas import tpu as pltpu

F32 = jnp.float32
BF16 = jnp.bfloat16
I32 = jnp.int32
HIGHEST = lax.Precision.HIGHEST
NT_DIMS = (((1,), (1,)), ((), ()))

D_MODEL = 1024
DEPTH = 4
GRID_W = 64
HEAD_DIM = 64
ATT_HEADS = 8
ATT_KV_HEADS = 2
ATT_WIDTH = ATT_HEADS * HEAD_DIM
ROPE_THETA = 10000.0
ROPE_AXIS_FREQS = HEAD_DIM // 4
RET_HEADS = 4
RET_DK = 64
RET_WIDTH = 256
SSD_HEADS = 4
SSD_WIDTH = 256
SSD_STATE = 64
N_EXPERTS = 256
TOP_K = 8
ROUTER_GROUPS = 8
ROUTER_TOPK_GROUPS = 4
GROUP_SIZE = N_EXPERTS // ROUTER_GROUPS
EXPERT_FF = 256
SHARED_FF = 256
ROUTED_SCALE = 2.5
DEEPNORM_ALPHA = (2 * DEPTH) ** 0.25
IN_SIZES = (512, 128, 128, 256, 256, 256, 256, 256, 256, 128, 128, 8)

LANES = 128
SUBLANES = 8
MIB = 1024 * 1024

QKV_W = 768
RET_OFF, RET_W = 768, 1024
SSD_OFF, SSD_W = 1792, 896
IN_W_PAD = 2688
CONV_CH = 512

MOE_ROWS = 256
MOE_VMEM_MIB = 62


def _cparams(n_grid, vmem_mib):
    return pltpu.CompilerParams(dimension_semantics=("arbitrary",) * n_grid,
                                vmem_limit_bytes=vmem_mib * MIB)


def _dot(a, b):
    return jnp.dot(a, b, preferred_element_type=F32)


def _dot_nt(a, b):
    return lax.dot_general(a, b, NT_DIMS, preferred_element_type=F32)


def _silu(x):
    return x * jax.nn.sigmoid(x)


def _softplus(x):
    return jnp.maximum(x, 0.0) + jnp.log1p(jnp.exp(-jnp.abs(x)))


def _mod_spec(modp, tiles_per_seq):
    if modp.shape[0] == 1:
        return pl.BlockSpec((1, 6, D_MODEL), lambda i: (0, 0, 0))
    return pl.BlockSpec((1, 6, D_MODEL), lambda i: (i // tiles_per_seq, 0, 0))


def _group_sum(x, g):
    hi = x.astype(BF16)
    lo = (x - hi.astype(F32)).astype(BF16)
    return _dot(hi, g) + _dot(lo, g)


def _swap16(x):
    n = x.shape[1]
    lane = lax.broadcasted_iota(I32, x.shape, 1)
    return jnp.where((lane & 16) == 0, pltpu.roll(x, n - 16, 1), pltpu.roll(x, 16, 1))


def _layer_norm(h, w, b):
    mu = jnp.mean(h, axis=-1, keepdims=True)
    d = h - mu
    var = jnp.mean(d * d, axis=-1, keepdims=True)
    return d * lax.rsqrt(var + 1e-5) * w + b


def _mod_kernel(cond_ref, w_ref, b_ref, o_ref):
    c = cond_ref[...]
    o_ref[0] = jnp.dot(_silu(c), w_ref[0], precision=HIGHEST, preferred_element_type=F32) + b_ref[0]


def _modulation(cond8, w_mod, b_mod):
    tn = 1536
    n_col = w_mod.shape[2] // tn
    return pl.pallas_call(
        _mod_kernel,
        grid=(DEPTH, n_col),
        in_specs=[pl.BlockSpec((8, D_MODEL), lambda l, j: (0, 0)),
                  pl.BlockSpec((1, D_MODEL, tn), lambda l, j: (l, 0, j)),
                  pl.BlockSpec((1, 1, tn), lambda l, j: (l, 0, j))],
        out_specs=pl.BlockSpec((1, 8, tn), lambda l, j: (l, 0, j)),
        out_shape=jax.ShapeDtypeStruct((DEPTH, 8, w_mod.shape[2]), F32),
        compiler_params=_cparams(2, 32),
        name="modulation",
    )(cond8, w_mod, b_mod.reshape(DEPTH, 1, -1))


def _proj_in_kernel(*refs, rope, emit_cache):
    x_ref, mod_ref, w_ref, qnw_ref, knw_ref, g_ref = refs[:6]
    pos = 6
    if rope:
        cos_ref, sin_ref = refs[pos:pos + 2]
        pos += 2
    q_ref, k_ref, v_ref, ret_ref, ssd_ref = refs[pos:pos + 5]
    pos += 5
    u = x_ref[...] * (1.0 + mod_ref[0, 1:2, :]) + mod_ref[0, 0:1, :]
    ub = u.astype(BF16)
    q = _dot(ub, w_ref[:, 0:512])
    k = _dot(ub, w_ref[:, 512:640])
    v = _dot(ub, w_ref[:, 640:768])
    g = g_ref[...]
    inv_hd = 1.0 / HEAD_DIM
    qn = q * lax.rsqrt(_group_sum(q * q, g) * inv_hd + 1e-6) * qnw_ref[...]
    kn = k * lax.rsqrt(_group_sum(k * k, g[0:LANES, 0:LANES]) * inv_hd + 1e-6) * knw_ref[...]
    if emit_cache:
        kf_ref, vf_ref = refs[pos:pos + 2]
        kf_ref[...] = kn
        vf_ref[...] = v
    if rope:
        c = cos_ref[...]
        s = sin_ref[...]
        kn = kn * c + _swap16(kn) * s
        qn = qn * jnp.concatenate([c] * 4, axis=1) + _swap16(qn) * jnp.concatenate([s] * 4, axis=1)
    q_ref[...] = (qn * (HEAD_DIM ** -0.5)).astype(BF16)
    k_ref[...] = kn.astype(BF16)
    v_ref[...] = v.astype(BF16)
    ret_ref[...] = _dot(ub, w_ref[:, RET_OFF:RET_OFF + RET_W])
    ssd_ref[...] = _dot(ub, w_ref[:, SSD_OFF:SSD_OFF + SSD_W])


def _proj_in(x, modp, w_in_p, qnw, knw, g512, rope_tabs, layer, seq_len, tt, emit_cache):
    t = x.shape[0]
    tiles_per_seq = seq_len // tt
    rope = rope_tabs is not None
    row = lambda i: (i, 0)
    const2 = lambda i: (0, 0)
    in_specs = [pl.BlockSpec((tt, D_MODEL), row),
                _mod_spec(modp, tiles_per_seq),
                pl.BlockSpec((None, D_MODEL, IN_W_PAD), lambda i: (layer, 0, 0)),
                pl.BlockSpec((None, 1, 512), lambda i: (layer, 0, 0)),
                pl.BlockSpec((None, 1, LANES), lambda i: (layer, 0, 0)),
                pl.BlockSpec((512, 512), const2)]
    args = [x, modp, w_in_p, qnw, knw, g512]
    if rope:
        tab = pl.BlockSpec((tt, LANES), lambda i: (i % tiles_per_seq, 0))
        in_specs += [tab, tab]
        args += list(rope_tabs)
    out_shape = [jax.ShapeDtypeStruct((t, 512), BF16), jax.ShapeDtypeStruct((t, LANES), BF16),
                 jax.ShapeDtypeStruct((t, LANES), BF16), jax.ShapeDtypeStruct((t, RET_W), F32),
                 jax.ShapeDtypeStruct((t, SSD_W), F32)]
    out_specs = [pl.BlockSpec((tt, 512), row), pl.BlockSpec((tt, LANES), row), pl.BlockSpec((tt, LANES), row),
                 pl.BlockSpec((tt, RET_W), row), pl.BlockSpec((tt, SSD_W), row)]
    if emit_cache:
        out_shape += [jax.ShapeDtypeStruct((t, LANES), F32)] * 2
        out_specs += [pl.BlockSpec((tt, LANES), row)] * 2
    return pl.pallas_call(
        functools.partial(_proj_in_kernel, rope=rope, emit_cache=emit_cache),
        grid=(t // tt,), in_specs=in_specs, out_specs=out_specs, out_shape=out_shape,
        compiler_params=_cparams(1, 48), name="proj_in",
    )(*args)


def _attn_kernel(*refs, with_cache):
    q_ref, k_ref, v_ref = refs[:3]
    o_ref = refs[-1]
    lane = lax.broadcasted_iota(I32, (1, LANES), 1)
    first = lane < HEAD_DIM
    parts = [(k_ref[...], v_ref[...])]
    if with_cache:
        parts.append((refs[3][...].astype(BF16), refs[4][...].astype(BF16)))
    zero = jnp.zeros((), BF16)
    kmask = [[jnp.where(first, kk, zero), jnp.where(first, zero, kk)] for kk, _ in parts]
    for j in range(ATT_HEADS // 2):
        qb = q_ref[:, LANES * j:LANES * (j + 1)]
        outs = []
        for g in range(ATT_KV_HEADS):
            ss = [_dot_nt(qb, km[g]) for km in kmask]
            m = ss[0].max(axis=-1, keepdims=True)
            for s in ss[1:]:
                m = jnp.maximum(m, s.max(axis=-1, keepdims=True))
            acc = None
            den = None
            for s, (_, vv) in zip(ss, parts):
                e = jnp.exp(s - m)
                d = e.sum(axis=-1, keepdims=True)
                o = _dot(e.astype(BF16), vv)
                acc = o if acc is None else acc + o
                den = d if den is None else den + d
            outs.append(acc / den)
        o_ref[:, LANES * j:LANES * (j + 1)] = jnp.where(first, outs[0], outs[1]).astype(BF16)


def _attention(q, k, v, cache, layer, n_seq, seq_len, tq):
    t = q.shape[0]
    nq = seq_len // tq
    in_specs = [pl.BlockSpec((tq, 512), lambda b, i: (b * nq + i, 0)),
                pl.BlockSpec((seq_len, LANES), lambda b, i: (b, 0)),
                pl.BlockSpec((seq_len, LANES), lambda b, i: (b, 0))]
    args = [q, k, v]
    if cache is not None:
        past = cache[0].shape[2]
        cspec = pl.BlockSpec((None, None, past, LANES), lambda b, i: (b, layer, 0, 0))
        in_specs += [cspec, cspec]
        args += list(cache)
    return pl.pallas_call(
        functools.partial(_attn_kernel, with_cache=cache is not None),
        grid=(n_seq, nq), in_specs=in_specs,
        out_specs=pl.BlockSpec((tq, 512), lambda b, i: (b * nq + i, 0)),
        out_shape=jax.ShapeDtypeStruct((t, 512), BF16),
        compiler_params=_cparams(2, 48), name="attention",
    )(*args)


def _scan_kernel(*refs, n, chunk, has_init, emit_state):
    (ret_ref, ssd_ref, hp_ref, hn_ref, cw_ref, cb_ref, dtb_ref, alog_ref, rld_ref,
     rnw_ref, dl_ref, snw_ref, g_ref) = refs[:13]
    pos = 13
    if has_init:
        s0r_ref, s0s_ref = refs[pos:pos + 2]
        pos += 2
    reto_ref, ssdo_ref = refs[pos:pos + 2]
    pos += 2
    if emit_state:
        str_ref, sts_ref = refs[pos:pos + 2]
        pos += 2
    of_ref, st_ref = refs[pos:pos + 2]

    C = chunk
    s = pl.program_id(1)
    bwd = s >= n
    c = jnp.where(bwd, 2 * n - 1 - s, s)

    def init_state(direction):
        if has_init:
            st_ref[0:2] = s0s_ref[0, direction]
            st_ref[2:4] = s0r_ref[0, direction]
        else:
            st_ref[...] = jnp.zeros(st_ref.shape, F32)

    @pl.when(s == 0)
    def _():
        init_state(0)

    @pl.when(s == n)
    def _():
        init_state(1)

    lane = lax.broadcasted_iota(I32, (1, LANES), 1)
    first = lane < HEAD_DIM
    ret = ret_ref[...]
    ssd = ssd_ref[...]

    xin = ssd[:, 0:CONV_CH]
    rowi = lax.broadcasted_iota(I32, (C, 1), 0)
    prev_row = jnp.where(c == 0, 0.0, hp_ref[SUBLANES - 1:SUBLANES, :])
    next_row = jnp.where(c == n - 1, 0.0, hn_ref[0:1, :])
    xp = jnp.where(rowi == 0, prev_row, pltpu.roll(xin, 1, 0))
    xn = jnp.where(rowi == C - 1, next_row, pltpu.roll(xin, C - 1, 0))
    xbc = _silu(xp * cw_ref[0:1, :] + xin * cw_ref[1:2, :] + xn * cw_ref[2:3, :] + cb_ref[...])
    sx = xbc[:, 0:256]
    sb = xbc[:, 256:384]
    sc = xbc[:, 384:512]

    def this_dir(a):
        return jnp.where(bwd, pltpu.roll(a, LANES - SSD_HEADS, 1), a)

    dt_all = _softplus(ssd[:, 768:896] + dtb_ref[...])
    la_ssd = this_dir(dt_all * (-jnp.exp(alog_ref[...])))
    dt_dir = this_dir(dt_all)
    lg = jnp.broadcast_to(jnp.log1p(-jnp.exp(rld_ref[...])), (SUBLANES, LANES))
    lg_dir = pltpu.roll(this_dir(lg), SSD_HEADS, 1)[0:1, :]
    la = jnp.where(lane < 4, la_ssd, jnp.where(lane < 8, lg_dir, 0.0))
    ii = lax.broadcasted_iota(I32, (C, C), 0)
    jj = lax.broadcasted_iota(I32, (C, C), 1)
    msk = jnp.where(bwd, jj - ii, ii - jj) >= 0
    cum = jnp.dot(msk.astype(F32), la, precision=HIGHEST, preferred_element_type=F32)
    last = jnp.where(bwd, cum[0:1, :], cum[C - 1:C, :])
    acol = jnp.where(lane < 8, cum, jnp.where(lane < 12, pltpu.roll(dt_dir, 8, 1), 0.0))
    arow = acol.T

    def decay(h, scale):
        diff = cum[:, h:h + 1] - arow[h:h + 1, :]
        return jnp.where(msk, jnp.exp(jnp.where(msk, diff, 0.0)), 0.0) * scale

    def lanes2(a0, a1):
        return jnp.where(first, a0, a1)

    ii2 = lax.broadcasted_iota(I32, (LANES, LANES), 0)
    jj2 = lax.broadcasted_iota(I32, (LANES, LANES), 1)
    blockdiag = ((ii2 < HEAD_DIM) == (jj2 < HEAD_DIM)).astype(F32)

    o_pairs = []
    for kind in ("ssd", "ret"):
        for p in range(2):
            slot = p if kind == "ssd" else 2 + p
            heads = (2 * p, 2 * p + 1)
            hl = [h if kind == "ssd" else 4 + h for h in heads]
            if kind == "ssd":
                grp = first if p == 0 else jnp.logical_not(first)
                qm = jnp.where(grp, sc, 0.0).astype(BF16)
                km = jnp.where(grp, sb, 0.0)
                v128 = sx[:, LANES * p:LANES * (p + 1)]
                sg = _dot_nt(qm, sb.astype(BF16))
                scores = [sg, sg]
                rs = [arow[8 + h:9 + h, :] for h in heads]
                cs = [acol[:, 8 + h:9 + h] for h in heads]
                q_inter = qm
            else:
                q128 = ret[:, LANES * p:LANES * (p + 1)]
                km = ret[:, 256 + LANES * p:256 + LANES * (p + 1)]
                v128 = ret[:, 512 + LANES * p:512 + LANES * (p + 1)]
                kb = km.astype(BF16)
                scores = [_dot_nt(jnp.where(first, q128, 0.0).astype(BF16), kb),
                          _dot_nt(jnp.where(first, 0.0, q128).astype(BF16), kb)]
                rs = [RET_DK ** -0.5] * 2
                cs = [RET_DK ** -0.5] * 2
                q_inter = q128.astype(BF16)
            vb = v128.astype(BF16)
            intra = [_dot((scores[t] * decay(hl[t], rs[t])).astype(BF16), vb) for t in range(2)]
            state = st_ref[slot]
            ecum = lanes2(jnp.exp(cum[:, hl[0]:hl[0] + 1]), jnp.exp(cum[:, hl[1]:hl[1] + 1]))
            o_pairs.append(lanes2(intra[0], intra[1]) + _dot(q_inter, state.astype(BF16)) * ecum)
            wv = lanes2(jnp.exp(last[:, hl[0]:hl[0] + 1] - cum[:, hl[0]:hl[0] + 1]) * cs[0],
                        jnp.exp(last[:, hl[1]:hl[1] + 1] - cum[:, hl[1]:hl[1] + 1]) * cs[1])
            snew = _dot(km.T.astype(BF16), (v128 * wv).astype(BF16))
            if kind == "ret":
                snew = snew * blockdiag
            elast = lanes2(jnp.exp(last[:, hl[0]:hl[0] + 1]), jnp.exp(last[:, hl[1]:hl[1] + 1]))
            st_ref[slot] = elast * state + snew
    o_all = jnp.concatenate(o_pairs, axis=1)
    rows = pl.ds(pl.multiple_of(c * C, C), C)

    @pl.when(jnp.logical_not(bwd))
    def _():
        of_ref[rows, :] = o_all

    @pl.when(bwd)
    def _():
        tot = of_ref[rows, :] + o_all
        g = g_ref[...]
        inv = 1.0 / HEAD_DIM
        for p in range(2):
            o = tot[:, 256 + LANES * p:256 + LANES * (p + 1)]
            d = o - _group_sum(o, g) * inv
            var = _group_sum(d * d, g) * inv
            rn = d * lax.rsqrt(var + 1e-5) * rnw_ref[:, LANES * p:LANES * (p + 1)]
            gate = ret[:, 768 + LANES * p:768 + LANES * (p + 1)]
            reto_ref[:, LANES * p:LANES * (p + 1)] = (rn * _silu(gate)).astype(BF16)
        y = (tot[:, 0:256] + dl_ref[...] * sx) * _silu(ssd[:, 512:768])
        ms = jnp.mean(y * y, axis=-1, keepdims=True)
        ssdo_ref[...] = (y * lax.rsqrt(ms + 1e-6) * snw_ref[...]).astype(BF16)

    if emit_state:
        @pl.when(s == n - 1)
        def _():
            sts_ref[0, 0] = st_ref[0:2]
            str_ref[0, 0] = st_ref[2:4]

        @pl.when(s == 2 * n - 1)
        def _():
            sts_ref[0, 1] = st_ref[0:2]
            str_ref[0, 1] = st_ref[2:4]


def _scan(ret, ssd, pw, layer, init, n_seq, seq_len, chunk, emit_state):
    t = ret.shape[0]
    n = seq_len // chunk
    rows8 = t // SUBLANES

    def cidx(s):
        return jnp.where(s >= n, 2 * n - 1 - s, s)

    def oidx(s):
        return jnp.where(s >= n, 2 * n - 1 - s, n - 1)

    par = lambda w: pl.BlockSpec((None,) + w, lambda b, s: (layer,) + (0,) * len(w))
    in_specs = [pl.BlockSpec((chunk, RET_W), lambda b, s: (b * n + cidx(s), 0)),
                pl.BlockSpec((chunk, SSD_W), lambda b, s: (b * n + cidx(s), 0)),
                pl.BlockSpec((SUBLANES, CONV_CH),
                             lambda b, s: (jnp.maximum((b * seq_len + cidx(s) * chunk) // SUBLANES - 1, 0), 0)),
                pl.BlockSpec((SUBLANES, CONV_CH),
                             lambda b, s: (jnp.minimum((b * seq_len + (cidx(s) + 1) * chunk) // SUBLANES, rows8 - 1), 0)),
                par((3, CONV_CH)), par((1, CONV_CH)), par((1, LANES)), par((1, LANES)), par((1, LANES)),
                par((1, RET_WIDTH)), par((1, SSD_WIDTH)), par((1, SSD_WIDTH)),
                pl.BlockSpec((LANES, LANES), lambda b, s: (0, 0))]
    args = [ret, ssd, ssd, ssd, pw["conv_w"], pw["conv_b"], pw["dt_bias"], pw["a_log"], pw["ret_ld"],
            pw["ret_nw"], pw["d_lane"], pw["ssd_nw"], pw["g128"]]
    st_spec = pl.BlockSpec((1, 2, 2, LANES, LANES), lambda b, s: (b, 0, 0, 0, 0))
    if init is not None:
        in_specs += [st_spec, st_spec]
        args += list(init)
    out_specs = [pl.BlockSpec((chunk, RET_WIDTH), lambda b, s: (b * n + oidx(s), 0)),
                 pl.BlockSpec((chunk, SSD_WIDTH), lambda b, s: (b * n + oidx(s), 0))]
    out_shape = [jax.ShapeDtypeStruct((t, RET_WIDTH), BF16), jax.ShapeDtypeStruct((t, SSD_WIDTH), BF16)]
    if emit_state:
        out_specs += [st_spec, st_spec]
        out_shape += [jax.ShapeDtypeStruct((n_seq, 2, 2, LANES, LANES), F32)] * 2
    return pl.pallas_call(
        functools.partial(_scan_kernel, n=n, chunk=chunk, has_init=init is not None, emit_state=emit_state),
        grid=(n_seq, 2 * n), in_specs=in_specs, out_specs=out_specs, out_shape=out_shape,
        scratch_shapes=[pltpu.VMEM((seq_len, 512), F32), pltpu.VMEM((4, LANES, LANES), F32)],
        compiler_params=_cparams(2, 48), name="scan",
    )(*args)


def _proj_out_kernel(x_ref, att_ref, ro_ref, so_ref, mod_ref, wo_ref, lnw_ref, lnb_ref, rwh_ref, rwl_ref,
                     x1_ref, u2_ref, u2p_ref, lg_ref):
    mix = (_dot(att_ref[...], wo_ref[0:512, :]) + _dot(ro_ref[...], wo_ref[512:768, :])
           + _dot(so_ref[...], wo_ref[768:1024, :]))
    h = DEEPNORM_ALPHA * x_ref[...] + mod_ref[0, 2:3, :] * mix
    x1 = _layer_norm(h, lnw_ref[...], lnb_ref[...])
    x1_ref[...] = x1
    u2 = x1 * (1.0 + mod_ref[0, 4:5, :]) + mod_ref[0, 3:4, :]
    hi = u2.astype(BF16)
    u2_ref[...] = hi
    hif = hi.astype(F32)
    bits = lax.bitcast_convert_type(hif, jnp.uint32)
    half = D_MODEL // 2
    packed = lax.shift_right_logical(bits[:, 0:half], jnp.uint32(16)) | (bits[:, half:] & jnp.uint32(0xFFFF0000))
    u2p_ref[...] = lax.bitcast_convert_type(packed, I32)
    lo = (u2 - hif).astype(BF16)
    rwh = rwh_ref[...]
    lg_ref[...] = _dot_nt(rwh, hi) + _dot_nt(rwh, lo) + _dot_nt(rwl_ref[...], hi)


def _proj_out(x, att, ro, so, modp, w_out_p, ln_w, ln_b, rwh, rwl, layer, seq_len, tt):
    t = x.shape[0]
    tiles_per_seq = seq_len // tt
    row = lambda i: (i, 0)
    lsel = lambda *w: pl.BlockSpec((None,) + w, lambda i: (layer,) + (0,) * len(w))
    in_specs = [pl.BlockSpec((tt, D_MODEL), row), pl.BlockSpec((tt, 512), row),
                pl.BlockSpec((tt, RET_WIDTH), row), pl.BlockSpec((tt, SSD_WIDTH), row),
                _mod_spec(modp, tiles_per_seq),
                lsel(D_MODEL, D_MODEL),
                pl.BlockSpec((None, None, 1, D_MODEL), lambda i: (layer, 0, 0, 0)),
                pl.BlockSpec((None, None, 1, D_MODEL), lambda i: (layer, 0, 0, 0)),
                lsel(N_EXPERTS, D_MODEL), lsel(N_EXPERTS, D_MODEL)]
    out_shape = [jax.ShapeDtypeStruct((t, D_MODEL), F32), jax.ShapeDtypeStruct((t, D_MODEL), BF16),
                 jax.ShapeDtypeStruct((t, D_MODEL // 2), I32), jax.ShapeDtypeStruct((N_EXPERTS, t), F32)]
    out_specs = [pl.BlockSpec((tt, D_MODEL), row), pl.BlockSpec((tt, D_MODEL), row),
                 pl.BlockSpec((tt, D_MODEL // 2), row), pl.BlockSpec((N_EXPERTS, tt), lambda i: (0, i))]
    return pl.pallas_call(
        _proj_out_kernel, grid=(t // tt,), in_specs=in_specs, out_specs=out_specs, out_shape=out_shape,
        compiler_params=_cparams(1, 48), name="proj_out",
    )(x, att, ro, so, modp, w_out_p, ln_w, ln_b, rwh, rwl)


def _route_kernel(lg_ref, bias_ref, idx_ref, rank_ref, wc_ref, cnt_ref, carry_ref):
    i = pl.program_id(0)

    @pl.when(i == 0)
    def _():
        carry_ref[...] = jnp.zeros(carry_ref.shape, F32)

    scores = jax.nn.sigmoid(lg_ref[...])
    sel = scores + bias_ref[...]
    n_e, tr = sel.shape
    neg = -jnp.inf
    sub = lax.broadcasted_iota(I32, (GROUP_SIZE, tr), 0).astype(F32)
    blocks = [sel[GROUP_SIZE * g:GROUP_SIZE * (g + 1), :] for g in range(ROUTER_GROUPS)]
    gscore = []
    for blk in blocks:
        m1 = blk.max(axis=0, keepdims=True)
        pos1 = jnp.min(jnp.where(blk == m1, sub, float(GROUP_SIZE)), axis=0, keepdims=True)
        m2 = jnp.max(jnp.where(sub == pos1, neg, blk), axis=0, keepdims=True)
        gscore.append(m1 + m2)
    masked = []
    for g in range(ROUTER_GROUPS):
        beat = jnp.zeros_like(gscore[g])
        for h in range(ROUTER_GROUPS):
            if h == g:
                continue
            wins = (gscore[h] >= gscore[g]) if h < g else (gscore[h] > gscore[g])
            beat = beat + wins.astype(F32)
        masked.append(jnp.where(beat < float(ROUTER_TOPK_GROUPS), blocks[g], neg))
    cur = jnp.concatenate(masked, axis=0)
    eio = lax.broadcasted_iota(I32, (n_e, tr), 0).astype(F32)
    chosen = jnp.zeros((n_e, tr), F32)
    picks = []
    for _ in range(TOP_K):
        m = cur.max(axis=0, keepdims=True)
        ik = jnp.min(jnp.where(cur == m, eio, float(n_e)), axis=0, keepdims=True)
        hit = eio == ik
        picks.append((ik, hit))
        cur = jnp.where(hit, neg, cur)
        chosen = jnp.where(hit, 1.0, chosen)
    picked = chosen * scores
    wc_ref[...] = picked / jnp.sum(picked, axis=0, keepdims=True) * ROUTED_SCALE
    r_ = lax.broadcasted_iota(I32, (tr, tr), 0)
    c_ = lax.broadcasted_iota(I32, (tr, tr), 1)
    chosen_b = chosen.astype(BF16)
    before = _dot(chosen_b, (r_ < c_).astype(BF16)) + carry_ref[:, 0:1]
    for k, (ik, hit) in enumerate(picks):
        idx_ref[k:k + 1, :] = ik.astype(I32)
        rank_ref[k:k + 1, :] = jnp.sum(jnp.where(hit, before, 0.0), axis=0, keepdims=True).astype(I32)
    carry_ref[...] = carry_ref[...] + _dot(chosen_b, jnp.ones((tr, LANES), BF16))
    cnt_ref[...] = carry_ref[...]


def _route(lg_t, bias, layer, tr):
    t = lg_t.shape[1]
    col = lambda i: (0, i)
    return pl.pallas_call(
        _route_kernel, grid=(t // tr,),
        in_specs=[pl.BlockSpec((N_EXPERTS, tr), col),
                  pl.BlockSpec((None, N_EXPERTS, 1), lambda i: (layer, 0, 0))],
        out_specs=[pl.BlockSpec((TOP_K, tr), col), pl.BlockSpec((TOP_K, tr), col),
                   pl.BlockSpec((N_EXPERTS, tr), col), pl.BlockSpec((N_EXPERTS, LANES), lambda i: (0, 0))],
        out_shape=[jax.ShapeDtypeStruct((TOP_K, t), I32), jax.ShapeDtypeStruct((TOP_K, t), I32),
                   jax.ShapeDtypeStruct((N_EXPERTS, t), F32), jax.ShapeDtypeStruct((N_EXPERTS, LANES), F32)],
        scratch_shapes=[pltpu.VMEM((N_EXPERTS, LANES), F32)],
        compiler_params=_cparams(1, 48), name="route",
    )(lg_t, bias)


def _plan_kernel(start_ref, idx_ref, rank_ref, tok_ref, *, tp):
    base = pl.program_id(0) * tp

    def body(j, carry):
        for k in range(TOP_K):
            tok_ref[start_ref[idx_ref[k, j]] + rank_ref[k, j]] = base + j
        return carry

    lax.fori_loop(0, tp, body, 0)


def _plan(start, idx_t, rank_t, tp):
    t = idx_t.shape[1]
    blk = pl.BlockSpec((TOP_K, tp), lambda i, st: (0, i), memory_space=pltpu.SMEM)
    grid_spec = pltpu.PrefetchScalarGridSpec(
        num_scalar_prefetch=1, grid=(t // tp,), in_specs=[blk, blk],
        out_specs=pl.BlockSpec(memory_space=pltpu.SMEM))
    return pl.pallas_call(
        functools.partial(_plan_kernel, tp=tp), grid_spec=grid_spec,
        out_shape=jax.ShapeDtypeStruct((t * TOP_K,), I32),
        compiler_params=_cparams(1, 32), name="plan",
    )(start, idx_t, rank_t)


def _moe_kernel(start_ref, cnt_ref, tok_ref, wc_ref, x_hbm, wg_ref, wu_ref, wd_ref, out_hbm,
                xv_ref, acc_ref, xt_ref, y_ref, trow_ref, sem, *, n_tok, rows, batch):
    e = pl.program_id(0)
    x_chunks = 4
    o_chunks = D_MODEL // LANES
    stride = rows + SUBLANES
    n_assign = n_tok * TOP_K

    @pl.when(e == 0)
    def _():
        cp = pltpu.make_async_copy(x_hbm, xv_ref.at[pl.ds(0, n_tok * x_chunks), :], sem)
        cp.start()
        acc_ref[...] = jnp.zeros(acc_ref.shape, F32)
        xv_ref[pl.ds(n_tok * x_chunks, SUBLANES), :] = jnp.zeros((SUBLANES, LANES), I32)
        cp.wait()

    n = cnt_ref[e]
    first = start_ref[e]
    wg = wg_ref[...].astype(BF16)
    wu = wu_ref[...].astype(BF16)
    wd = wd_ref[...].astype(BF16)

    def block(b, carry):
        base = first + b * rows
        left = n - b * rows
        for r in range(rows):
            t = tok_ref[jnp.minimum(base + r, n_assign - 1)]
            t = jnp.where(r < left, t, n_tok)
            trow_ref[r] = t
            xt_ref[pl.ds(r, x_chunks, stride=stride), :] = xv_ref[pl.ds(pl.multiple_of(t * x_chunks, x_chunks), x_chunks), :]
        lo, hi = [], []
        for c in range(x_chunks):
            w = xt_ref[pl.ds(c * stride, rows), :]
            lo.append(lax.bitcast_convert_type(w << 16, F32).astype(BF16))
            hi.append(lax.bitcast_convert_type(w & jnp.int32(-65536), F32).astype(BF16))
        xb = jnp.concatenate(lo + hi, axis=1)
        h = (_silu(_dot(xb, wg)) * _dot(xb, wu)).astype(BF16)
        y = _dot(h, wd)
        for c in range(o_chunks):
            y_ref[pl.ds(c * stride, rows), :] = y[:, LANES * c:LANES * (c + 1)]
        for r0 in range(0, rows, batch):
            pending = []
            for r in range(r0, r0 + batch):
                t = trow_ref[r]
                wt = wc_ref[0, 0, jnp.minimum(t, n_tok - 1)]
                dst = pl.ds(pl.multiple_of(t * o_chunks, o_chunks), o_chunks)
                pending.append((dst, acc_ref[dst, :] + wt * y_ref[pl.ds(r, o_chunks, stride=stride), :]))
            for dst, val in pending:
                acc_ref[dst, :] = val
        return carry

    lax.fori_loop(0, (n + rows - 1) // rows, block, 0)

    @pl.when(e == pl.num_programs(0) - 1)
    def _():
        cp = pltpu.make_async_copy(acc_ref.at[pl.ds(0, n_tok * o_chunks), :], out_hbm, sem)
        cp.start()
        cp.wait()


def _moe(start, cnt, tok_sorted, wc, x_packed, w_gate, w_up, w_down, layer, n_tok, rows, vmem_mib):
    x_chunks = 4
    o_chunks = D_MODEL // LANES
    wspec = lambda a, b: pl.BlockSpec((None, None, a, b), lambda e, st, ct: (layer, e, 0, 0))
    grid_spec = pltpu.PrefetchScalarGridSpec(
        num_scalar_prefetch=2, grid=(N_EXPERTS,),
        in_specs=[pl.BlockSpec(memory_space=pltpu.SMEM),
                  pl.BlockSpec((1, 1, n_tok), lambda e, st, ct: (e, 0, 0), memory_space=pltpu.SMEM),
                  pl.BlockSpec(memory_space=pl.ANY),
                  wspec(D_MODEL, EXPERT_FF), wspec(D_MODEL, EXPERT_FF), wspec(EXPERT_FF, D_MODEL)],
        out_specs=pl.BlockSpec(memory_space=pl.ANY),
        scratch_shapes=[pltpu.VMEM((n_tok * x_chunks + SUBLANES, LANES), I32),
                        pltpu.VMEM(((n_tok + 1) * o_chunks, LANES), F32),
                        pltpu.VMEM((x_chunks * (rows + SUBLANES), LANES), I32),
                        pltpu.VMEM((o_chunks * (rows + SUBLANES), LANES), F32),
                        pltpu.SMEM((rows,), I32),
                        pltpu.SemaphoreType.DMA(())])
    out = pl.pallas_call(
        functools.partial(_moe_kernel, n_tok=n_tok, rows=rows, batch=8),
        grid_spec=grid_spec, out_shape=jax.ShapeDtypeStruct((n_tok * o_chunks, LANES), F32),
        compiler_params=_cparams(1, vmem_mib), name="moe",
    )(start, cnt, tok_sorted, wc.reshape(N_EXPERTS, 1, n_tok), x_packed.reshape(n_tok * x_chunks, LANES),
      w_gate, w_up, w_down)
    return out.reshape(n_tok, D_MODEL)


def _ffn_out_kernel(x1_ref, u2_ref, routed_ref, mod_ref, wgu_ref, wd_ref, lnw_ref, lnb_ref, o_ref):
    gu = _dot(u2_ref[...], wgu_ref[...])
    shared = _dot((_silu(gu[:, 0:SHARED_FF]) * gu[:, SHARED_FF:]).astype(BF16), wd_ref[...])
    h = DEEPNORM_ALPHA * x1_ref[...] + mod_ref[0, 5:6, :] * (routed_ref[...] + shared)
    o_ref[...] = _layer_norm(h, lnw_ref[...], lnb_ref[...])


def _ffn_out(x1, u2, routed, modp, sh_gu, sh_d, ln_w, ln_b, layer, seq_len, tt):
    t = x1.shape[0]
    tiles_per_seq = seq_len // tt
    row = lambda i: (i, 0)
    lsel = lambda *w: pl.BlockSpec((None,) + w, lambda i: (layer,) + (0,) * len(w))
    ln_spec = pl.BlockSpec((None, None, 1, D_MODEL), lambda i: (layer, 1, 0, 0))
    return pl.pallas_call(
        _ffn_out_kernel, grid=(t // tt,),
        in_specs=[pl.BlockSpec((tt, D_MODEL), row), pl.BlockSpec((tt, D_MODEL), row), pl.BlockSpec((tt, D_MODEL), row),
                  _mod_spec(modp, tiles_per_seq),
                  lsel(D_MODEL, 2 * SHARED_FF), lsel(SHARED_FF, D_MODEL), ln_spec, ln_spec],
        out_specs=pl.BlockSpec((tt, D_MODEL), row),
        out_shape=jax.ShapeDtypeStruct((t, D_MODEL), F32),
        compiler_params=_cparams(1, 48), name="ffn_out",
    )(x1, u2, routed, modp, sh_gu, sh_d, ln_w, ln_b)


def _q_perm():
    cols = []
    for j in range(ATT_HEADS // 2):
        for half in range(2):
            h = j + (ATT_HEADS // 2) * half
            cols.extend(range(h * HEAD_DIM, (h + 1) * HEAD_DIM))
    return np.asarray(cols, np.int32)


def _prepare_params(p):
    off = np.concatenate([[0], np.cumsum(IN_SIZES)])
    seg = lambda k: np.arange(off[k], off[k + 1], dtype=np.int32)
    qp = _q_perm()
    cols = np.concatenate([qp, seg(1), seg(2), seg(3), seg(4), seg(5), seg(6), seg(7), seg(9), seg(10), seg(8), seg(11)])
    w_in = jnp.take(p["w_in"], jnp.asarray(cols), axis=2)
    w_in = jnp.pad(w_in, ((0, 0), (0, 0), (0, IN_W_PAD - w_in.shape[2]))).astype(BF16)
    rows = np.concatenate([qp, np.arange(ATT_WIDTH, D_MODEL, dtype=np.int32)])
    w_out = jnp.take(p["w_out"], jnp.asarray(rows), axis=1).astype(BF16)
    rwt = jnp.swapaxes(p["router_w"], 1, 2)
    rwh = rwt.astype(BF16)
    rwl = (rwt - rwh.astype(F32)).astype(BF16)

    def lane_pad(a, fill=0.0):
        a = a.reshape(DEPTH, 1, -1)
        return jnp.pad(a, ((0, 0), (0, 0), (0, LANES - a.shape[2])), constant_values=fill)

    gi = np.arange(512) // HEAD_DIM
    g512 = jnp.asarray(gi[:, None] == gi[None, :], BF16)
    return {
        "w_in": w_in, "w_out": w_out, "rwh": rwh, "rwl": rwl,
        "qnw": jnp.tile(p["q_norm_w"], (1, ATT_HEADS)).reshape(DEPTH, 1, 512),
        "knw": jnp.tile(p["k_norm_w"], (1, ATT_KV_HEADS)).reshape(DEPTH, 1, LANES),
        "g512": g512, "g128": g512[0:LANES, 0:LANES],
        "conv_w": p["ssd_conv_w"], "conv_b": p["ssd_conv_b"].reshape(DEPTH, 1, CONV_CH),
        "dt_bias": lane_pad(p["ssd_dt_bias"]), "a_log": lane_pad(p["ssd_a_log"]),
        "ret_ld": lane_pad(p["ret_log_decay"], -1.0),
        "ret_nw": p["ret_norm_w"].reshape(DEPTH, 1, RET_WIDTH),
        "d_lane": jnp.repeat(p["ssd_d"], HEAD_DIM, axis=1).reshape(DEPTH, 1, SSD_WIDTH),
        "ssd_nw": p["ssd_norm_w"].reshape(DEPTH, 1, SSD_WIDTH),
        "ln_w": p["ln_w"].reshape(DEPTH, 2, 1, D_MODEL), "ln_b": p["ln_b"].reshape(DEPTH, 2, 1, D_MODEL),
        "router_bias": p["router_bias"].reshape(DEPTH, N_EXPERTS, 1),
        "sh_gu": jnp.concatenate([p["sh_w_gate"], p["sh_w_up"]], axis=2).astype(BF16),
        "sh_d": p["sh_w_down"].astype(BF16),
        "exp_w_gate": p["exp_w_gate"], "exp_w_up": p["exp_w_up"], "exp_w_down": p["exp_w_down"],
    }


def _rope_tables(n_tokens):
    t = jnp.arange(n_tokens, dtype=I32)
    rows = (t // GRID_W).astype(F32)
    cols = (t % GRID_W).astype(F32)
    inv = ROPE_THETA ** (-jnp.arange(ROPE_AXIS_FREQS, dtype=F32) / ROPE_AXIS_FREQS)
    ar = rows[:, None] * inv
    ac = cols[:, None] * inv
    cos = jnp.concatenate([jnp.cos(ar), jnp.cos(ar), jnp.cos(ac), jnp.cos(ac)], axis=1)
    sin = jnp.concatenate([-jnp.sin(ar), jnp.sin(ar), -jnp.sin(ac), jnp.sin(ac)], axis=1)
    return jnp.tile(cos, (1, 2)), jnp.tile(sin, (1, 2))


def _pair_states(st, kind):
    b = st.shape[0]
    out = jnp.zeros((b, 2, 2, LANES, LANES), F32)
    for pr in range(2):
        for tt in range(2):
            r0 = HEAD_DIM * (tt if kind == "ret" else pr)
            out = out.at[:, :, pr, r0:r0 + HEAD_DIM, HEAD_DIM * tt:HEAD_DIM * (tt + 1)].set(st[:, :, 2 * pr + tt])
    return out


def _unpair_states(m, kind):
    heads = []
    for pr in range(2):
        for tt in range(2):
            r0 = HEAD_DIM * (tt if kind == "ret" else pr)
            heads.append(m[:, :, pr, r0:r0 + HEAD_DIM, HEAD_DIM * tt:HEAD_DIM * (tt + 1)])
    return jnp.stack(heads, axis=2)


def _trunk_layer(x, modp, pw, layer, n_seq, seq_len, rope_tabs, cache, init, cfg):
    n_tok = x.shape[0]
    ctx = cache is None
    outs = _proj_in(x, modp, pw["w_in"], pw["qnw"], pw["knw"], pw["g512"], rope_tabs, layer, seq_len,
                    cfg["tt"], emit_cache=ctx)
    q, k, v, ret, ssd = outs[:5]
    att = _attention(q, k, v, cache, layer, n_seq, seq_len, cfg["tq"])
    sc = _scan(ret, ssd, pw, layer, init, n_seq, seq_len, cfg["chunk"], emit_state=ctx)
    x1, u2, u2p, lg_t = _proj_out(x, att, sc[0], sc[1], modp, pw["w_out"], pw["ln_w"], pw["ln_b"],
                                  pw["rwh"], pw["rwl"], layer, seq_len, cfg["tt"])
    idx_t, rank_t, wc, counts = _route(lg_t, pw["router_bias"], layer, cfg["tr"])
    cnt = counts[:, 0].astype(I32)
    start = jnp.cumsum(cnt) - cnt
    tok_sorted = _plan(start, idx_t, rank_t, cfg["tr"])
    routed = _moe(start, cnt, tok_sorted, wc, u2p, pw["exp_w_gate"], pw["exp_w_up"], pw["exp_w_down"],
                  layer, n_tok, cfg["rows"], cfg["moe_vmem"])
    x_new = _ffn_out(x1, u2, routed, modp, pw["sh_gu"], pw["sh_d"], pw["ln_w"], pw["ln_b"], layer, seq_len, cfg["tt"])
    new_ctx = (outs[5], outs[6], sc[2], sc[3]) if ctx else None
    return x_new, new_ctx


def kernel(x_prompt, x_sample, cache_k, cache_v, state_ret, state_ssd, c, c_ctx,
           w_mod, b_mod, w_in, q_norm_w, k_norm_w, ret_log_decay, ret_norm_w,
           ssd_conv_w, ssd_conv_b, ssd_dt_bias, ssd_a_log, ssd_d, ssd_norm_w, w_out,
           ln_w, ln_b, router_w, router_bias, exp_w_gate, exp_w_up, exp_w_down,
           sh_w_gate, sh_w_up, sh_w_down):
    pw = _prepare_params({
        "w_in": w_in, "q_norm_w": q_norm_w, "k_norm_w": k_norm_w, "ret_log_decay": ret_log_decay,
        "ret_norm_w": ret_norm_w, "ssd_conv_w": ssd_conv_w, "ssd_conv_b": ssd_conv_b, "ssd_dt_bias": ssd_dt_bias,
        "ssd_a_log": ssd_a_log, "ssd_d": ssd_d, "ssd_norm_w": ssd_norm_w, "w_out": w_out, "ln_w": ln_w, "ln_b": ln_b,
        "router_w": router_w, "router_bias": router_bias, "exp_w_gate": exp_w_gate, "exp_w_up": exp_w_up,
        "exp_w_down": exp_w_down, "sh_w_gate": sh_w_gate, "sh_w_up": sh_w_up, "sh_w_down": sh_w_down})
    nb, seq, d = x_prompt.shape
    nd, dseq, _ = x_sample.shape
    cond = jnp.concatenate([c_ctx[None], c, jnp.zeros((8 - 1 - nd, d), F32)], axis=0)
    mod = _modulation(cond, w_mod, b_mod)
    rope_tabs = _rope_tables(dseq)
    past = cache_k.shape[2]
    ck = cache_k.reshape(nd, DEPTH, past, LANES)
    cv = cache_v.reshape(nd, DEPTH, past, LANES)
    cfg_ctx = {"tt": 512, "tq": seq, "chunk": seq, "tr": 512, "rows": MOE_ROWS, "moe_vmem": MOE_VMEM_MIB}
    cfg_lat = {"tt": 512, "tq": 256, "chunk": 256, "tr": 512, "rows": MOE_ROWS, "moe_vmem": MOE_VMEM_MIB}

    xc = x_prompt.reshape(nb * seq, d)
    xs = x_sample.reshape(nd * dseq, d)
    ks, vs, srs, sss = [], [], [], []
    for l in range(DEPTH):
        mod_ctx = mod[l, 0:1].reshape(1, 6, d)
        mod_lat = mod[l, 1:1 + nd].reshape(nd, 6, d)
        xc, (k_l, v_l, sr_l, ss_l) = _trunk_layer(xc, mod_ctx, pw, l, nb, seq, None, None, None, cfg_ctx)
        ks.append(k_l.reshape(nb, seq, ATT_KV_HEADS, HEAD_DIM))
        vs.append(v_l.reshape(nb, seq, ATT_KV_HEADS, HEAD_DIM))
        srs.append(_unpair_states(sr_l, "ret"))
        sss.append(_unpair_states(ss_l, "ssd"))
        init = (_pair_states(state_ret[:, l], "ret"), _pair_states(state_ssd[:, l], "ssd"))
        xs, _ = _trunk_layer(xs, mod_lat, pw, l, nd, dseq, rope_tabs, (ck, cv), init, cfg_lat)
    return (xc.reshape(nb, seq, d), xs.reshape(nd, dseq, d), jnp.stack(ks, axis=1), jnp.stack(vs, axis=1),
            jnp.stack(srs, axis=1), jnp.stack(sss, axis=1))
```

```python
import functools

import numpy as np
import jax
import jax.numpy as jnp
from jax import lax
from jax.experimental import pallas as pl
from jax.experimental.pallas import tpu as pltpu

F32 = jnp.float32
BF16 = jnp.bfloat16
I32 = jnp.int32
HIGHEST = lax.Precision.HIGHEST
NT_DIMS = (((1,), (1,)), ((), ()))

D_MODEL = 1024
DEPTH = 4
GRID_W = 64
HEAD_DIM = 64
ATT_HEADS = 8
ATT_KV_HEADS = 2
ATT_WIDTH = ATT_HEADS * HEAD_DIM
ROPE_THETA = 10000.0
ROPE_AXIS_FREQS = HEAD_DIM // 4
RET_HEADS = 4
RET_DK = 64
RET_WIDTH = 256
SSD_HEADS = 4
SSD_WIDTH = 256
SSD_STATE = 64
N_EXPERTS = 256
TOP_K = 8
ROUTER_GROUPS = 8
ROUTER_TOPK_GROUPS = 4
GROUP_SIZE = N_EXPERTS // ROUTER_GROUPS
EXPERT_FF = 256
SHARED_FF = 256
ROUTED_SCALE = 2.5
DEEPNORM_ALPHA = (2 * DEPTH) ** 0.25
IN_SIZES = (512, 128, 128, 256, 256, 256, 256, 256, 256, 128, 128, 8)

LANES = 128
SUBLANES = 8
MIB = 1024 * 1024

QKV_W = 768
RET_OFF, RET_W = 768, 1024
SSD_OFF, SSD_W = 1792, 896
IN_W_PAD = 2688
CONV_CH = 512

OUT_CHUNKS = D_MODEL // LANES
MOE_ROWS = 256
MOE_VMEM_MIB = 62


def _cparams(n_grid, vmem_mib):
    return pltpu.CompilerParams(dimension_semantics=("arbitrary",) * n_grid,
                                vmem_limit_bytes=vmem_mib * MIB)


def _dot(a, b):
    return jnp.dot(a, b, preferred_element_type=F32)


def _dot_nt(a, b):
    return lax.dot_general(a, b, NT_DIMS, preferred_element_type=F32)


def _silu(x):
    return x * jax.nn.sigmoid(x)


def _softplus(x):
    return jnp.maximum(x, 0.0) + jnp.log1p(jnp.exp(-jnp.abs(x)))


def _mod_spec(modp, tiles_per_seq):
    if modp.shape[0] == 1:
        return pl.BlockSpec((1, 6, D_MODEL), lambda i: (0, 0, 0))
    return pl.BlockSpec((1, 6, D_MODEL), lambda i: (i // tiles_per_seq, 0, 0))


def _group_sum(x, g):
    hi = x.astype(BF16)
    lo = (x - hi.astype(F32)).astype(BF16)
    return _dot(hi, g) + _dot(lo, g)


def _swap16(x):
    n = x.shape[1]
    lane = lax.broadcasted_iota(I32, x.shape, 1)
    return jnp.where((lane & 16) == 0, pltpu.roll(x, n - 16, 1), pltpu.roll(x, 16, 1))


def _layer_norm(h, w, b):
    mu = jnp.mean(h, axis=-1, keepdims=True)
    d = h - mu
    var = jnp.mean(d * d, axis=-1, keepdims=True)
    return d * lax.rsqrt(var + 1e-5) * w + b


def _mod_kernel(cond_ref, w_ref, b_ref, o_ref):
    c = cond_ref[...]
    o_ref[0] = jnp.dot(_silu(c), w_ref[0], precision=HIGHEST, preferred_element_type=F32) + b_ref[0]


def _modulation(cond8, w_mod, b_mod):
    tn = 1536
    n_col = w_mod.shape[2] // tn
    return pl.pallas_call(
        _mod_kernel,
        grid=(DEPTH, n_col),
        in_specs=[pl.BlockSpec((8, D_MODEL), lambda l, j: (0, 0)),
                  pl.BlockSpec((1, D_MODEL, tn), lambda l, j: (l, 0, j)),
                  pl.BlockSpec((1, 1, tn), lambda l, j: (l, 0, j))],
        out_specs=pl.BlockSpec((1, 8, tn), lambda l, j: (l, 0, j)),
        out_shape=jax.ShapeDtypeStruct((DEPTH, 8, w_mod.shape[2]), F32),
        compiler_params=_cparams(2, 32),
        name="modulation",
    )(cond8, w_mod, b_mod.reshape(DEPTH, 1, -1))


def _proj_in_kernel(*refs, rope, emit_cache):
    x_ref, mod_ref, w_ref, qnw_ref, knw_ref, g_ref = refs[:6]
    pos = 6
    if rope:
        cos_ref, sin_ref = refs[pos:pos + 2]
        pos += 2
    q_ref, k_ref, v_ref, ret_ref, ssd_ref = refs[pos:pos + 5]
    pos += 5
    u = x_ref[...] * (1.0 + mod_ref[0, 1:2, :]) + mod_ref[0, 0:1, :]
    ub = u.astype(BF16)
    q = _dot(ub, w_ref[:, 0:512])
    k = _dot(ub, w_ref[:, 512:640])
    v = _dot(ub, w_ref[:, 640:768])
    g = g_ref[...]
    inv_hd = 1.0 / HEAD_DIM
    qn = q * lax.rsqrt(_group_sum(q * q, g) * inv_hd + 1e-6) * qnw_ref[...]
    kn = k * lax.rsqrt(_group_sum(k * k, g[0:LANES, 0:LANES]) * inv_hd + 1e-6) * knw_ref[...]
    if emit_cache:
        kf_ref, vf_ref = refs[pos:pos + 2]
        kf_ref[...] = kn
        vf_ref[...] = v
    if rope:
        c = cos_ref[...]
        s = sin_ref[...]
        kn = kn * c + _swap16(kn) * s
        qn = qn * jnp.concatenate([c] * 4, axis=1) + _swap16(qn) * jnp.concatenate([s] * 4, axis=1)
    q_ref[...] = (qn * (HEAD_DIM ** -0.5)).astype(BF16)
    k_ref[...] = kn.astype(BF16)
    v_ref[...] = v.astype(BF16)
    ret_ref[...] = _dot(ub, w_ref[:, RET_OFF:RET_OFF + RET_W])
    ssd_ref[...] = _dot(ub, w_ref[:, SSD_OFF:SSD_OFF + SSD_W])


def _proj_in(x, modp, w_in_p, qnw, knw, g512, rope_tabs, layer, seq_len, tt, emit_cache):
    t = x.shape[0]
    tiles_per_seq = seq_len // tt
    rope = rope_tabs is not None
    row = lambda i: (i, 0)
    const2 = lambda i: (0, 0)
    in_specs = [pl.BlockSpec((tt, D_MODEL), row),
                _mod_spec(modp, tiles_per_seq),
                pl.BlockSpec((None, D_MODEL, IN_W_PAD), lambda i: (layer, 0, 0)),
                pl.BlockSpec((None, 1, 512), lambda i: (layer, 0, 0)),
                pl.BlockSpec((None, 1, LANES), lambda i: (layer, 0, 0)),
                pl.BlockSpec((512, 512), const2)]
    args = [x, modp, w_in_p, qnw, knw, g512]
    if rope:
        tab = pl.BlockSpec((tt, LANES), lambda i: (i % tiles_per_seq, 0))
        in_specs += [tab, tab]
        args += list(rope_tabs)
    out_shape = [jax.ShapeDtypeStruct((t, 512), BF16), jax.ShapeDtypeStruct((t, LANES), BF16),
                 jax.ShapeDtypeStruct((t, LANES), BF16), jax.ShapeDtypeStruct((t, RET_W), F32),
                 jax.ShapeDtypeStruct((t, SSD_W), F32)]
    out_specs = [pl.BlockSpec((tt, 512), row), pl.BlockSpec((tt, LANES), row), pl.BlockSpec((tt, LANES), row),
                 pl.BlockSpec((tt, RET_W), row), pl.BlockSpec((tt, SSD_W), row)]
    if emit_cache:
        out_shape += [jax.ShapeDtypeStruct((t, LANES), F32)] * 2
        out_specs += [pl.BlockSpec((tt, LANES), row)] * 2
    return pl.pallas_call(
        functools.partial(_proj_in_kernel, rope=rope, emit_cache=emit_cache),
        grid=(t // tt,), in_specs=in_specs, out_specs=out_specs, out_shape=out_shape,
        compiler_params=_cparams(1, 48), name="proj_in",
    )(*args)


def _attn_kernel(*refs, with_cache):
    q_ref, k_ref, v_ref = refs[:3]
    o_ref = refs[-1]
    lane = lax.broadcasted_iota(I32, (1, LANES), 1)
    first = lane < HEAD_DIM
    parts = [(k_ref[...], v_ref[...])]
    if with_cache:
        parts.append((refs[3][...].astype(BF16), refs[4][...].astype(BF16)))
    zero = jnp.zeros((), BF16)
    kmask = [[jnp.where(first, kk, zero), jnp.where(first, zero, kk)] for kk, _ in parts]
    for j in range(ATT_HEADS // 2):
        qb = q_ref[:, LANES * j:LANES * (j + 1)]
        outs = []
        for g in range(ATT_KV_HEADS):
            ss = [_dot_nt(qb, km[g]) for km in kmask]
            m = ss[0].max(axis=-1, keepdims=True)
            for s in ss[1:]:
                m = jnp.maximum(m, s.max(axis=-1, keepdims=True))
            acc = None
            den = None
            for s, (_, vv) in zip(ss, parts):
                e = jnp.exp(s - m)
                d = e.sum(axis=-1, keepdims=True)
                o = _dot(e.astype(BF16), vv)
                acc = o if acc is None else acc + o
                den = d if den is None else den + d
            outs.append(acc / den)
        o_ref[:, LANES * j:LANES * (j + 1)] = jnp.where(first, outs[0], outs[1]).astype(BF16)


def _attention(q, k, v, cache, layer, n_seq, seq_len, tq):
    t = q.shape[0]
    nq = seq_len // tq
    in_specs = [pl.BlockSpec((tq, 512), lambda b, i: (b * nq + i, 0)),
                pl.BlockSpec((seq_len, LANES), lambda b, i: (b, 0)),
                pl.BlockSpec((seq_len, LANES), lambda b, i: (b, 0))]
    args = [q, k, v]
    if cache is not None:
        past = cache[0].shape[2]
        cspec = pl.BlockSpec((None, None, past, LANES), lambda b, i: (b, layer, 0, 0))
        in_specs += [cspec, cspec]
        args += list(cache)
    return pl.pallas_call(
        functools.partial(_attn_kernel, with_cache=cache is not None),
        grid=(n_seq, nq), in_specs=in_specs,
        out_specs=pl.BlockSpec((tq, 512), lambda b, i: (b * nq + i, 0)),
        out_shape=jax.ShapeDtypeStruct((t, 512), BF16),
        compiler_params=_cparams(2, 48), name="attention",
    )(*args)


def _scan_kernel(*refs, n, chunk, has_init, emit_state):
    (ret_ref, ssd_ref, hp_ref, hn_ref, cw_ref, cb_ref, dtb_ref, alog_ref, rld_ref,
     rnw_ref, dl_ref, snw_ref, g_ref) = refs[:13]
    pos = 13
    if has_init:
        s0r_ref, s0s_ref = refs[pos:pos + 2]
        pos += 2
    reto_ref, ssdo_ref = refs[pos:pos + 2]
    pos += 2
    if emit_state:
        str_ref, sts_ref = refs[pos:pos + 2]
        pos += 2
    of_ref, st_ref = refs[pos:pos + 2]

    C = chunk
    s = pl.program_id(1)
    bwd = s >= n
    c = jnp.where(bwd, 2 * n - 1 - s, s)

    def init_state(direction):
        if has_init:
            st_ref[0:2] = s0s_ref[0, direction]
            st_ref[2:4] = s0r_ref[0, direction]
        else:
            st_ref[...] = jnp.zeros(st_ref.shape, F32)

    @pl.when(s == 0)
    def _():
        init_state(0)

    @pl.when(s == n)
    def _():
        init_state(1)

    lane = lax.broadcasted_iota(I32, (1, LANES), 1)
    first = lane < HEAD_DIM
    ret = ret_ref[...]
    ssd = ssd_ref[...]

    xin = ssd[:, 0:CONV_CH]
    rowi = lax.broadcasted_iota(I32, (C, 1), 0)
    prev_row = jnp.where(c == 0, 0.0, hp_ref[SUBLANES - 1:SUBLANES, :])
    next_row = jnp.where(c == n - 1, 0.0, hn_ref[0:1, :])
    xp = jnp.where(rowi == 0, prev_row, pltpu.roll(xin, 1, 0))
    xn = jnp.where(rowi == C - 1, next_row, pltpu.roll(xin, C - 1, 0))
    xbc = _silu(xp * cw_ref[0:1, :] + xin * cw_ref[1:2, :] + xn * cw_ref[2:3, :] + cb_ref[...])
    sx = xbc[:, 0:256]
    sb = xbc[:, 256:384]
    sc = xbc[:, 384:512]

    def this_dir(a):
        return jnp.where(bwd, pltpu.roll(a, LANES - SSD_HEADS, 1), a)

    dt_all = _softplus(ssd[:, 768:896] + dtb_ref[...])
    la_ssd = this_dir(dt_all * (-jnp.exp(alog_ref[...])))
    dt_dir = this_dir(dt_all)
    lg = jnp.broadcast_to(jnp.log1p(-jnp.exp(rld_ref[...])), (SUBLANES, LANES))
    lg_dir = pltpu.roll(this_dir(lg), SSD_HEADS, 1)[0:1, :]
    la = jnp.where(lane < 4, la_ssd, jnp.where(lane < 8, lg_dir, 0.0))
    ii = lax.broadcasted_iota(I32, (C, C), 0)
    jj = lax.broadcasted_iota(I32, (C, C), 1)
    msk = jnp.where(bwd, jj - ii, ii - jj) >= 0
    cum = jnp.dot(msk.astype(F32), la, precision=HIGHEST, preferred_element_type=F32)
    last = jnp.where(bwd, cum[0:1, :], cum[C - 1:C, :])
    acol = jnp.where(lane < 8, cum, jnp.where(lane < 12, pltpu.roll(dt_dir, 8, 1), 0.0))
    arow = acol.T

    def decay(h, scale):
        diff = cum[:, h:h + 1] - arow[h:h + 1, :]
        return jnp.where(msk, jnp.exp(jnp.where(msk, diff, 0.0)), 0.0) * scale

    def lanes2(a0, a1):
        return jnp.where(first, a0, a1)

    ii2 = lax.broadcasted_iota(I32, (LANES, LANES), 0)
    jj2 = lax.broadcasted_iota(I32, (LANES, LANES), 1)
    blockdiag = ((ii2 < HEAD_DIM) == (jj2 < HEAD_DIM)).astype(F32)

    o_pairs = []
    for kind in ("ssd", "ret"):
        for p in range(2):
            slot = p if kind == "ssd" else 2 + p
            heads = (2 * p, 2 * p + 1)
            hl = [h if kind == "ssd" else 4 + h for h in heads]
            if kind == "ssd":
                grp = first if p == 0 else jnp.logical_not(first)
                qm = jnp.where(grp, sc, 0.0).astype(BF16)
                km = jnp.where(grp, sb, 0.0)
                v128 = sx[:, LANES * p:LANES * (p + 1)]
                sg = _dot_nt(qm, sb.astype(BF16))
                scores = [sg, sg]
                rs = [arow[8 + h:9 + h, :] for h in heads]
                cs = [acol[:, 8 + h:9 + h] for h in heads]
                q_inter = qm
            else:
                q128 = ret[:, LANES * p:LANES * (p + 1)]
                km = ret[:, 256 + LANES * p:256 + LANES * (p + 1)]
                v128 = ret[:, 512 + LANES * p:512 + LANES * (p + 1)]
                kb = km.astype(BF16)
                scores = [_dot_nt(jnp.where(first, q128, 0.0).astype(BF16), kb),
                          _dot_nt(jnp.where(first, 0.0, q128).astype(BF16), kb)]
                rs = [RET_DK ** -0.5] * 2
                cs = [RET_DK ** -0.5] * 2
                q_inter = q128.astype(BF16)
            vb = v128.astype(BF16)
            intra = [_dot((scores[t] * decay(hl[t], rs[t])).astype(BF16), vb) for t in range(2)]
            state = st_ref[slot]
            ecum = lanes2(jnp.exp(cum[:, hl[0]:hl[0] + 1]), jnp.exp(cum[:, hl[1]:hl[1] + 1]))
            o_pairs.append(lanes2(intra[0], intra[1]) + _dot(q_inter, state.astype(BF16)) * ecum)
            wv = lanes2(jnp.exp(last[:, hl[0]:hl[0] + 1] - cum[:, hl[0]:hl[0] + 1]) * cs[0],
                        jnp.exp(last[:, hl[1]:hl[1] + 1] - cum[:, hl[1]:hl[1] + 1]) * cs[1])
            snew = _dot(km.T.astype(BF16), (v128 * wv).astype(BF16))
            if kind == "ret":
                snew = snew * blockdiag
            elast = lanes2(jnp.exp(last[:, hl[0]:hl[0] + 1]), jnp.exp(last[:, hl[1]:hl[1] + 1]))
            st_ref[slot] = elast * state + snew
    o_all = jnp.concatenate(o_pairs, axis=1)
    rows = pl.ds(pl.multiple_of(c * C, C), C)

    @pl.when(jnp.logical_not(bwd))
    def _():
        of_ref[rows, :] = o_all

    @pl.when(bwd)
    def _():
        tot = of_ref[rows, :] + o_all
        g = g_ref[...]
        inv = 1.0 / HEAD_DIM
        for p in range(2):
            o = tot[:, 256 + LANES * p:256 + LANES * (p + 1)]
            d = o - _group_sum(o, g) * inv
            var = _group_sum(d * d, g) * inv
            rn = d * lax.rsqrt(var + 1e-5) * rnw_ref[:, LANES * p:LANES * (p + 1)]
            gate = ret[:, 768 + LANES * p:768 + LANES * (p + 1)]
            reto_ref[:, LANES * p:LANES * (p + 1)] = (rn * _silu(gate)).astype(BF16)
        y = (tot[:, 0:256] + dl_ref[...] * sx) * _silu(ssd[:, 512:768])
        ms = jnp.mean(y * y, axis=-1, keepdims=True)
        ssdo_ref[...] = (y * lax.rsqrt(ms + 1e-6) * snw_ref[...]).astype(BF16)

    if emit_state:
        @pl.when(s == n - 1)
        def _():
            sts_ref[0, 0] = st_ref[0:2]
            str_ref[0, 0] = st_ref[2:4]

        @pl.when(s == 2 * n - 1)
        def _():
            sts_ref[0, 1] = st_ref[0:2]
            str_ref[0, 1] = st_ref[2:4]


def _scan(ret, ssd, pw, layer, init, n_seq, seq_len, chunk, emit_state):
    t = ret.shape[0]
    n = seq_len // chunk
    rows8 = t // SUBLANES

    def cidx(s):
        return jnp.where(s >= n, 2 * n - 1 - s, s)

    def oidx(s):
        return jnp.where(s >= n, 2 * n - 1 - s, n - 1)

    par = lambda w: pl.BlockSpec((None,) + w, lambda b, s: (layer,) + (0,) * len(w))
    in_specs = [pl.BlockSpec((chunk, RET_W), lambda b, s: (b * n + cidx(s), 0)),
                pl.BlockSpec((chunk, SSD_W), lambda b, s: (b * n + cidx(s), 0)),
                pl.BlockSpec((SUBLANES, CONV_CH),
                             lambda b, s: (jnp.maximum((b * seq_len + cidx(s) * chunk) // SUBLANES - 1, 0), 0)),
                pl.BlockSpec((SUBLANES, CONV_CH),
                             lambda b, s: (jnp.minimum((b * seq_len + (cidx(s) + 1) * chunk) // SUBLANES, rows8 - 1), 0)),
                par((3, CONV_CH)), par((1, CONV_CH)), par((1, LANES)), par((1, LANES)), par((1, LANES)),
                par((1, RET_WIDTH)), par((1, SSD_WIDTH)), par((1, SSD_WIDTH)),
                pl.BlockSpec((LANES, LANES), lambda b, s: (0, 0))]
    args = [ret, ssd, ssd, ssd, pw["conv_w"], pw["conv_b"], pw["dt_bias"], pw["a_log"], pw["ret_ld"],
            pw["ret_nw"], pw["d_lane"], pw["ssd_nw"], pw["g128"]]
    st_spec = pl.BlockSpec((1, 2, 2, LANES, LANES), lambda b, s: (b, 0, 0, 0, 0))
    if init is not None:
        in_specs += [st_spec, st_spec]
        args += list(init)
    out_specs = [pl.BlockSpec((chunk, RET_WIDTH), lambda b, s: (b * n + oidx(s), 0)),
                 pl.BlockSpec((chunk, SSD_WIDTH), lambda b, s: (b * n + oidx(s), 0))]
    out_shape = [jax.ShapeDtypeStruct((t, RET_WIDTH), BF16), jax.ShapeDtypeStruct((t, SSD_WIDTH), BF16)]
    if emit_state:
        out_specs += [st_spec, st_spec]
        out_shape += [jax.ShapeDtypeStruct((n_seq, 2, 2, LANES, LANES), F32)] * 2
    return pl.pallas_call(
        functools.partial(_scan_kernel, n=n, chunk=chunk, has_init=init is not None, emit_state=emit_state),
        grid=(n_seq, 2 * n), in_specs=in_specs, out_specs=out_specs, out_shape=out_shape,
        scratch_shapes=[pltpu.VMEM((seq_len, 512), F32), pltpu.VMEM((4, LANES, LANES), F32)],
        compiler_params=_cparams(2, 48), name="scan",
    )(*args)


def _proj_out_kernel(x_ref, att_ref, ro_ref, so_ref, mod_ref, wo_ref, lnw_ref, lnb_ref, rwh_ref, rwl_ref,
                     x1_ref, u2_ref, lg_ref):
    mix = (_dot(att_ref[...], wo_ref[0:512, :]) + _dot(ro_ref[...], wo_ref[512:768, :])
           + _dot(so_ref[...], wo_ref[768:1024, :]))
    h = DEEPNORM_ALPHA * x_ref[...] + mod_ref[0, 2:3, :] * mix
    x1 = _layer_norm(h, lnw_ref[...], lnb_ref[...])
    x1_ref[...] = x1
    u2 = x1 * (1.0 + mod_ref[0, 4:5, :]) + mod_ref[0, 3:4, :]
    hi = u2.astype(BF16)
    u2_ref[...] = hi
    lo = (u2 - hi.astype(F32)).astype(BF16)
    rwh = rwh_ref[...]
    lg_ref[...] = _dot_nt(rwh, hi) + _dot_nt(rwh, lo) + _dot_nt(rwl_ref[...], hi)


def _proj_out(x, att, ro, so, modp, w_out_p, ln_w, ln_b, rwh, rwl, layer, seq_len, tt):
    t = x.shape[0]
    tiles_per_seq = seq_len // tt
    row = lambda i: (i, 0)
    lsel = lambda *w: pl.BlockSpec((None,) + w, lambda i: (layer,) + (0,) * len(w))
    in_specs = [pl.BlockSpec((tt, D_MODEL), row), pl.BlockSpec((tt, 512), row),
                pl.BlockSpec((tt, RET_WIDTH), row), pl.BlockSpec((tt, SSD_WIDTH), row),
                _mod_spec(modp, tiles_per_seq),
                lsel(D_MODEL, D_MODEL),
                pl.BlockSpec((None, None, 1, D_MODEL), lambda i: (layer, 0, 0, 0)),
                pl.BlockSpec((None, None, 1, D_MODEL), lambda i: (layer, 0, 0, 0)),
                lsel(N_EXPERTS, D_MODEL), lsel(N_EXPERTS, D_MODEL)]
    out_shape = [jax.ShapeDtypeStruct((t, D_MODEL), F32), jax.ShapeDtypeStruct((t, D_MODEL), BF16),
                 jax.ShapeDtypeStruct((N_EXPERTS, t), F32)]
    out_specs = [pl.BlockSpec((tt, D_MODEL), row), pl.BlockSpec((tt, D_MODEL), row),
                 pl.BlockSpec((N_EXPERTS, tt), lambda i: (0, i))]
    return pl.pallas_call(
        _proj_out_kernel, grid=(t // tt,), in_specs=in_specs, out_specs=out_specs, out_shape=out_shape,
        compiler_params=_cparams(1, 48), name="proj_out",
    )(x, att, ro, so, modp, w_out_p, ln_w, ln_b, rwh, rwl)


def _route_kernel(lg_ref, bias_ref, idx_ref, rank_ref, wc_ref, cnt_ref, carry_ref):
    i = pl.program_id(0)

    @pl.when(i == 0)
    def _():
        carry_ref[...] = jnp.zeros(carry_ref.shape, F32)

    scores = jax.nn.sigmoid(lg_ref[...])
    sel = scores + bias_ref[...]
    n_e, tr = sel.shape
    neg = -jnp.inf
    sub = lax.broadcasted_iota(I32, (GROUP_SIZE, tr), 0).astype(F32)
    blocks = [sel[GROUP_SIZE * g:GROUP_SIZE * (g + 1), :] for g in range(ROUTER_GROUPS)]
    gscore = []
    for blk in blocks:
        m1 = blk.max(axis=0, keepdims=True)
        pos1 = jnp.min(jnp.where(blk == m1, sub, float(GROUP_SIZE)), axis=0, keepdims=True)
        m2 = jnp.max(jnp.where(sub == pos1, neg, blk), axis=0, keepdims=True)
        gscore.append(m1 + m2)
    masked = []
    for g in range(ROUTER_GROUPS):
        beat = jnp.zeros_like(gscore[g])
        for h in range(ROUTER_GROUPS):
            if h == g:
                continue
            wins = (gscore[h] >= gscore[g]) if h < g else (gscore[h] > gscore[g])
            beat = beat + wins.astype(F32)
        masked.append(jnp.where(beat < float(ROUTER_TOPK_GROUPS), blocks[g], neg))
    cur = jnp.concatenate(masked, axis=0)
    eio = lax.broadcasted_iota(I32, (n_e, tr), 0).astype(F32)
    chosen = jnp.zeros((n_e, tr), F32)
    picks = []
    for _ in range(TOP_K):
        m = cur.max(axis=0, keepdims=True)
        ik = jnp.min(jnp.where(cur == m, eio, float(n_e)), axis=0, keepdims=True)
        hit = eio == ik
        picks.append((ik, hit))
        cur = jnp.where(hit, neg, cur)
        chosen = jnp.where(hit, 1.0, chosen)
    picked = chosen * scores
    wc_ref[:, 0, :] = picked / jnp.sum(picked, axis=0, keepdims=True) * ROUTED_SCALE
    r_ = lax.broadcasted_iota(I32, (tr, tr), 0)
    c_ = lax.broadcasted_iota(I32, (tr, tr), 1)
    chosen_b = chosen.astype(BF16)
    before = _dot(chosen_b, (r_ < c_).astype(BF16)) + carry_ref[:, 0:1]
    for k, (ik, hit) in enumerate(picks):
        idx_ref[k:k + 1, :] = ik.astype(I32)
        rank_ref[k:k + 1, :] = jnp.sum(jnp.where(hit, before, 0.0), axis=0, keepdims=True).astype(I32)
    carry_ref[...] = carry_ref[...] + _dot(chosen_b, jnp.ones((tr, LANES), BF16))
    cnt_ref[...] = carry_ref[...]


def _route(lg_t, bias, layer, tr):
    t = lg_t.shape[1]
    col = lambda i: (0, i)
    return pl.pallas_call(
        _route_kernel, grid=(t // tr,),
        in_specs=[pl.BlockSpec((N_EXPERTS, tr), col),
                  pl.BlockSpec((None, N_EXPERTS, 1), lambda i: (layer, 0, 0))],
        out_specs=[pl.BlockSpec((TOP_K, tr), col), pl.BlockSpec((TOP_K, tr), col),
                   pl.BlockSpec((N_EXPERTS, 1, tr), lambda i: (0, 0, i)),
                   pl.BlockSpec((N_EXPERTS, LANES), lambda i: (0, 0))],
        out_shape=[jax.ShapeDtypeStruct((TOP_K, t), I32), jax.ShapeDtypeStruct((TOP_K, t), I32),
                   jax.ShapeDtypeStruct((N_EXPERTS, 1, t), F32), jax.ShapeDtypeStruct((N_EXPERTS, LANES), F32)],
        scratch_shapes=[pltpu.VMEM((N_EXPERTS, LANES), F32)],
        compiler_params=_cparams(1, 48), name="route",
    )(lg_t, bias)


def _plan_kernel(start_ref, idx_ref, rank_ref, tok_ref, *, tp):
    base = pl.program_id(0) * tp

    def body(j, carry):
        for k in range(TOP_K):
            tok_ref[start_ref[idx_ref[k, j]] + rank_ref[k, j]] = base + j
        return carry

    lax.fori_loop(0, tp, body, 0)


def _plan(start, idx_t, rank_t, tp):
    t = idx_t.shape[1]
    blk = pl.BlockSpec((TOP_K, tp), lambda i, st: (0, i), memory_space=pltpu.SMEM)
    grid_spec = pltpu.PrefetchScalarGridSpec(
        num_scalar_prefetch=1, grid=(t // tp,), in_specs=[blk, blk],
        out_specs=pl.BlockSpec(memory_space=pltpu.SMEM))
    return pl.pallas_call(
        functools.partial(_plan_kernel, tp=tp), grid_spec=grid_spec,
        out_shape=jax.ShapeDtypeStruct((t * TOP_K,), I32),
        compiler_params=_cparams(1, 32), name="plan",
    )(start, idx_t, rank_t)


def _moe_kernel(start_ref, cnt_ref, tok_ref, wc_ref, x_hbm, wg_ref, wu_ref, wd_ref, out_hbm,
                xv_ref, acc_ref, xt_ref, y_ref, trow_ref, sem, *, n_tok, rows, sub, batch):
    e = pl.program_id(0)
    n_chunks = OUT_CHUNKS
    pair_rows = 2 * n_chunks
    stride = rows + SUBLANES

    @pl.when(e == 0)
    def _():
        cp = pltpu.make_async_copy(x_hbm, xv_ref.at[pl.ds(0, n_tok * n_chunks), :], sem)
        cp.start()
        acc_ref[...] = jnp.zeros(acc_ref.shape, F32)
        xt_ref[...] = jnp.zeros(xt_ref.shape, F32)
        xv_ref[pl.ds(n_tok * n_chunks, pair_rows), :] = jnp.zeros((pair_rows, LANES), BF16)
        cp.wait()

    n = cnt_ref[e]
    first = start_ref[e]
    wg = wg_ref[...].astype(BF16)
    wu = wu_ref[...].astype(BF16)
    wd = wd_ref[...].astype(BF16)

    def block(b, carry):
        base = first + b * rows
        left = jnp.minimum(n - b * rows, rows)

        for s0 in range(0, rows, sub):
            @pl.when(s0 < left)
            def _():
                for r in range(s0, s0 + sub):
                    valid = r < left
                    t = jnp.where(valid, tok_ref[jnp.where(valid, base + r, 0)], n_tok)
                    trow_ref[r] = t
                    pair = xv_ref[pl.ds(pl.multiple_of((t >> 1) * pair_rows, pair_rows), pair_rows), :].astype(F32)
                    xt_ref[pl.ds(r, n_chunks, stride=stride), :] = jnp.where(
                        (t & 1) == 1, pair[n_chunks:pair_rows], pair[0:n_chunks])

        xb = jnp.concatenate([xt_ref[pl.ds(c * stride, rows), :] for c in range(n_chunks)], axis=1).astype(BF16)
        h = (_silu(_dot(xb, wg)) * _dot(xb, wu)).astype(BF16)
        y = _dot(h, wd)
        for c in range(n_chunks):
            y_ref[pl.ds(c * stride, rows), :] = y[:, LANES * c:LANES * (c + 1)]
        for s0 in range(0, rows, sub):
            @pl.when(s0 < left)
            def _():
                for r0 in range(s0, s0 + sub, batch):
                    pending = []
                    for r in range(r0, r0 + batch):
                        t = trow_ref[r]
                        wt = wc_ref[0, 0, jnp.minimum(t, n_tok - 1)]
                        dst = pl.ds(pl.multiple_of(t * n_chunks, n_chunks), n_chunks)
                        pending.append((dst, acc_ref[dst, :] + wt * y_ref[pl.ds(r, n_chunks, stride=stride), :]))
                    for dst, val in pending:
                        acc_ref[dst, :] = val
        return carry

    lax.fori_loop(0, (n + rows - 1) // rows, block, 0)

    @pl.when(e == pl.num_programs(0) - 1)
    def _():
        cp = pltpu.make_async_copy(acc_ref.at[pl.ds(0, n_tok * n_chunks), :], out_hbm, sem)
        cp.start()
        cp.wait()


def _moe(start, cnt, tok_sorted, wc, x_rows, w_gate, w_up, w_down, layer, n_tok, rows, vmem_mib):
    n_chunks = OUT_CHUNKS
    wspec = lambda a, b: pl.BlockSpec((None, None, a, b), lambda e, st, ct: (layer, e, 0, 0))
    grid_spec = pltpu.PrefetchScalarGridSpec(
        num_scalar_prefetch=2, grid=(N_EXPERTS,),
        in_specs=[pl.BlockSpec(memory_space=pltpu.SMEM),
                  pl.BlockSpec((1, 1, n_tok), lambda e, st, ct: (e, 0, 0), memory_space=pltpu.SMEM),
                  pl.BlockSpec(memory_space=pl.ANY),
                  wspec(D_MODEL, EXPERT_FF), wspec(D_MODEL, EXPERT_FF), wspec(EXPERT_FF, D_MODEL)],
        out_specs=pl.BlockSpec(memory_space=pl.ANY),
        scratch_shapes=[pltpu.VMEM(((n_tok + 2) * n_chunks, LANES), BF16),
                        pltpu.VMEM(((n_tok + 1) * n_chunks, LANES), F32),
                        pltpu.VMEM((n_chunks * (rows + SUBLANES), LANES), F32),
                        pltpu.VMEM((n_chunks * (rows + SUBLANES), LANES), F32),
                        pltpu.SMEM((rows,), I32),
                        pltpu.SemaphoreType.DMA(())])
    return pl.pallas_call(
        functools.partial(_moe_kernel, n_tok=n_tok, rows=rows, sub=64, batch=8),
        grid_spec=grid_spec, out_shape=jax.ShapeDtypeStruct((n_tok * n_chunks, LANES), F32),
        compiler_params=_cparams(1, vmem_mib), name="moe",
    )(start, cnt, tok_sorted, wc, x_rows, w_gate, w_up, w_down)


def _ffn_out_kernel(x1_ref, u2_ref, routed_ref, mod_ref, wgu_ref, wd_ref, lnw_ref, lnb_ref, o_ref):
    gu = _dot(u2_ref[...], wgu_ref[...])
    shared = _dot((_silu(gu[:, 0:SHARED_FF]) * gu[:, SHARED_FF:]).astype(BF16), wd_ref[...])
    tt = x1_ref.shape[0]
    routed = jnp.concatenate([routed_ref[pl.ds(c, tt, stride=OUT_CHUNKS), :] for c in range(OUT_CHUNKS)], axis=1)
    h = DEEPNORM_ALPHA * x1_ref[...] + mod_ref[0, 5:6, :] * (routed + shared)
    o_ref[...] = _layer_norm(h, lnw_ref[...], lnb_ref[...])


def _ffn_out(x1, u2, routed, modp, sh_gu, sh_d, ln_w, ln_b, layer, seq_len, tt):
    t = x1.shape[0]
    tiles_per_seq = seq_len // tt
    row = lambda i: (i, 0)
    lsel = lambda *w: pl.BlockSpec((None,) + w, lambda i: (layer,) + (0,) * len(w))
    ln_spec = pl.BlockSpec((None, None, 1, D_MODEL), lambda i: (layer, 1, 0, 0))
    return pl.pallas_call(
        _ffn_out_kernel, grid=(t // tt,),
        in_specs=[pl.BlockSpec((tt, D_MODEL), row), pl.BlockSpec((tt, D_MODEL), row),
                  pl.BlockSpec((tt * OUT_CHUNKS, LANES), row),
                  _mod_spec(modp, tiles_per_seq),
                  lsel(D_MODEL, 2 * SHARED_FF), lsel(SHARED_FF, D_MODEL), ln_spec, ln_spec],
        out_specs=pl.BlockSpec((tt, D_MODEL), row),
        out_shape=jax.ShapeDtypeStruct((t, D_MODEL), F32),
        compiler_params=_cparams(1, 48), name="ffn_out",
    )(x1, u2, routed, modp, sh_gu, sh_d, ln_w, ln_b)


def _q_perm():
    cols = []
    for j in range(ATT_HEADS // 2):
        for half in range(2):
            h = j + (ATT_HEADS // 2) * half
            cols.extend(range(h * HEAD_DIM, (h + 1) * HEAD_DIM))
    return np.asarray(cols, np.int32)


def _prepare_params(p):
    off = np.concatenate([[0], np.cumsum(IN_SIZES)])
    seg = lambda k: np.arange(off[k], off[k + 1], dtype=np.int32)
    qp = _q_perm()
    cols = np.concatenate([qp, seg(1), seg(2), seg(3), seg(4), seg(5), seg(6), seg(7), seg(9), seg(10), seg(8), seg(11)])
    w_in = jnp.take(p["w_in"], jnp.asarray(cols), axis=2)
    w_in = jnp.pad(w_in, ((0, 0), (0, 0), (0, IN_W_PAD - w_in.shape[2]))).astype(BF16)
    rows = np.concatenate([qp, np.arange(ATT_WIDTH, D_MODEL, dtype=np.int32)])
    w_out = jnp.take(p["w_out"], jnp.asarray(rows), axis=1).astype(BF16)
    rwt = jnp.swapaxes(p["router_w"], 1, 2)
    rwh = rwt.astype(BF16)
    rwl = (rwt - rwh.astype(F32)).astype(BF16)

    def lane_pad(a, fill=0.0):
        a = a.reshape(DEPTH, 1, -1)
        return jnp.pad(a, ((0, 0), (0, 0), (0, LANES - a.shape[2])), constant_values=fill)

    gi = np.arange(512) // HEAD_DIM
    g512 = jnp.asarray(gi[:, None] == gi[None, :], BF16)
    return {
        "w_in": w_in, "w_out": w_out, "rwh": rwh, "rwl": rwl,
        "qnw": jnp.tile(p["q_norm_w"], (1, ATT_HEADS)).reshape(DEPTH, 1, 512),
        "knw": jnp.tile(p["k_norm_w"], (1, ATT_KV_HEADS)).reshape(DEPTH, 1, LANES),
        "g512": g512, "g128": g512[0:LANES, 0:LANES],
        "conv_w": p["ssd_conv_w"], "conv_b": p["ssd_conv_b"].reshape(DEPTH, 1, CONV_CH),
        "dt_bias": lane_pad(p["ssd_dt_bias"]), "a_log": lane_pad(p["ssd_a_log"]),
        "ret_ld": lane_pad(p["ret_log_decay"], -1.0),
        "ret_nw": p["ret_norm_w"].reshape(DEPTH, 1, RET_WIDTH),
        "d_lane": jnp.repeat(p["ssd_d"], HEAD_DIM, axis=1).reshape(DEPTH, 1, SSD_WIDTH),
        "ssd_nw": p["ssd_norm_w"].reshape(DEPTH, 1, SSD_WIDTH),
        "ln_w": p["ln_w"].reshape(DEPTH, 2, 1, D_MODEL), "ln_b": p["ln_b"].reshape(DEPTH, 2, 1, D_MODEL),
        "router_bias": p["router_bias"].reshape(DEPTH, N_EXPERTS, 1),
        "sh_gu": jnp.concatenate([p["sh_w_gate"], p["sh_w_up"]], axis=2).astype(BF16),
        "sh_d": p["sh_w_down"].astype(BF16),
        "exp_w_gate": p["exp_w_gate"], "exp_w_up": p["exp_w_up"], "exp_w_down": p["exp_w_down"],
    }


def _rope_tables(n_tokens):
    t = jnp.arange(n_tokens, dtype=I32)
    rows = (t // GRID_W).astype(F32)
    cols = (t % GRID_W).astype(F32)
    inv = ROPE_THETA ** (-jnp.arange(ROPE_AXIS_FREQS, dtype=F32) / ROPE_AXIS_FREQS)
    ar = rows[:, None] * inv
    ac = cols[:, None] * inv
    cos = jnp.concatenate([jnp.cos(ar), jnp.cos(ar), jnp.cos(ac), jnp.cos(ac)], axis=1)
    sin = jnp.concatenate([-jnp.sin(ar), jnp.sin(ar), -jnp.sin(ac), jnp.sin(ac)], axis=1)
    return jnp.tile(cos, (1, 2)), jnp.tile(sin, (1, 2))


def _pair_states(st, kind):
    b = st.shape[0]
    out = jnp.zeros((b, 2, 2, LANES, LANES), F32)
    for pr in range(2):
        for tt in range(2):
            r0 = HEAD_DIM * (tt if kind == "ret" else pr)
            out = out.at[:, :, pr, r0:r0 + HEAD_DIM, HEAD_DIM * tt:HEAD_DIM * (tt + 1)].set(st[:, :, 2 * pr + tt])
    return out


def _unpair_states(m, kind):
    heads = []
    for pr in range(2):
        for tt in range(2):
            r0 = HEAD_DIM * (tt if kind == "ret" else pr)
            heads.append(m[:, :, pr, r0:r0 + HEAD_DIM, HEAD_DIM * tt:HEAD_DIM * (tt + 1)])
    return jnp.stack(heads, axis=2)


def _trunk_layer(x, modp, pw, layer, n_seq, seq_len, rope_tabs, cache, init, cfg):
    n_tok = x.shape[0]
    ctx = cache is None
    outs = _proj_in(x, modp, pw["w_in"], pw["qnw"], pw["knw"], pw["g512"], rope_tabs, layer, seq_len,
                    cfg["tt"], emit_cache=ctx)
    q, k, v, ret, ssd = outs[:5]
    att = _attention(q, k, v, cache, layer, n_seq, seq_len, cfg["tq"])
    sc = _scan(ret, ssd, pw, layer, init, n_seq, seq_len, cfg["chunk"], emit_state=ctx)
    x1, u2, lg_t = _proj_out(x, att, sc[0], sc[1], modp, pw["w_out"], pw["ln_w"], pw["ln_b"],
                                  pw["rwh"], pw["rwl"], layer, seq_len, cfg["tt"])
    idx_t, rank_t, wc, counts = _route(lg_t, pw["router_bias"], layer, cfg["tr"])
    cnt = counts[:, 0].astype(I32)
    start = jnp.cumsum(cnt) - cnt
    tok_sorted = _plan(start, idx_t, rank_t, cfg["tr"])
    u2_rows = u2.reshape(n_tok * OUT_CHUNKS, LANES)
    routed = _moe(start, cnt, tok_sorted, wc, u2_rows, pw["exp_w_gate"], pw["exp_w_up"], pw["exp_w_down"],
                  layer, n_tok, cfg["rows"], cfg["moe_vmem"])
    x_new = _ffn_out(x1, u2, routed, modp, pw["sh_gu"], pw["sh_d"], pw["ln_w"], pw["ln_b"], layer, seq_len, cfg["tt"])
    new_ctx = (outs[5], outs[6], sc[2], sc[3]) if ctx else None
    return x_new, new_ctx


def kernel(x_prompt, x_sample, cache_k, cache_v, state_ret, state_ssd, c, c_ctx,
           w_mod, b_mod, w_in, q_norm_w, k_norm_w, ret_log_decay, ret_norm_w,
           ssd_conv_w, ssd_conv_b, ssd_dt_bias, ssd_a_log, ssd_d, ssd_norm_w, w_out,
           ln_w, ln_b, router_w, router_bias, exp_w_gate, exp_w_up, exp_w_down,
           sh_w_gate, sh_w_up, sh_w_down):
    pw = _prepare_params({
        "w_in": w_in, "q_norm_w": q_norm_w, "k_norm_w": k_norm_w, "ret_log_decay": ret_log_decay,
        "ret_norm_w": ret_norm_w, "ssd_conv_w": ssd_conv_w, "ssd_conv_b": ssd_conv_b, "ssd_dt_bias": ssd_dt_bias,
        "ssd_a_log": ssd_a_log, "ssd_d": ssd_d, "ssd_norm_w": ssd_norm_w, "w_out": w_out, "ln_w": ln_w, "ln_b": ln_b,
        "router_w": router_w, "router_bias": router_bias, "exp_w_gate": exp_w_gate, "exp_w_up": exp_w_up,
        "exp_w_down": exp_w_down, "sh_w_gate": sh_w_gate, "sh_w_up": sh_w_up, "sh_w_down": sh_w_down})
    nb, seq, d = x_prompt.shape
    nd, dseq, _ = x_sample.shape
    cond = jnp.concatenate([c_ctx[None], c, jnp.zeros((8 - 1 - nd, d), F32)], axis=0)
    mod = _modulation(cond, w_mod, b_mod)
    rope_tabs = _rope_tables(dseq)
    past = cache_k.shape[2]
    ck = cache_k.reshape(nd, DEPTH, past, LANES)
    cv = cache_v.reshape(nd, DEPTH, past, LANES)
    cfg_ctx = {"tt": 512, "tq": seq, "chunk": seq, "tr": 512, "rows": MOE_ROWS, "moe_vmem": MOE_VMEM_MIB}
    cfg_lat = {"tt": 512, "tq": 512, "chunk": 256, "tr": 512, "rows": MOE_ROWS, "moe_vmem": MOE_VMEM_MIB}

    xc = x_prompt.reshape(nb * seq, d)
    xs = x_sample.reshape(nd * dseq, d)
    ks, vs, srs, sss = [], [], [], []
    for l in range(DEPTH):
        mod_ctx = mod[l, 0:1].reshape(1, 6, d)
        mod_lat = mod[l, 1:1 + nd].reshape(nd, 6, d)
        xc, (k_l, v_l, sr_l, ss_l) = _trunk_layer(xc, mod_ctx, pw, l, nb, seq, None, None, None, cfg_ctx)
        ks.append(k_l.reshape(nb, seq, ATT_KV_HEADS, HEAD_DIM))
        vs.append(v_l.reshape(nb, seq, ATT_KV_HEADS, HEAD_DIM))
        srs.append(_unpair_states(sr_l, "ret"))
        sss.append(_unpair_states(ss_l, "ssd"))
        init = (_pair_states(state_ret[:, l], "ret"), _pair_states(state_ssd[:, l], "ssd"))
        xs, _ = _trunk_layer(xs, mod_lat, pw, l, nd, dseq, rope_tabs, (ck, cv), init, cfg_lat)
    return (xc.reshape(nb, seq, d), xs.reshape(nd, dseq, d), jnp.stack(ks, axis=1), jnp.stack(vs, axis=1),
            jnp.stack(srs, axis=1), jnp.stack(sss, axis=1))
```

```python
import functools

import numpy as np
import jax
import jax.numpy as jnp
from jax import lax
from jax.experimental import pallas as pl
from jax.experimental.pallas import tpu as pltpu

F32 = jnp.float32
BF16 = jnp.bfloat16
I32 = jnp.int32
HIGHEST = lax.Precision.HIGHEST
NT_DIMS = (((1,), (1,)), ((), ()))

D_MODEL = 1024
DEPTH = 4
GRID_W = 64
HEAD_DIM = 64
ATT_HEADS = 8
ATT_KV_HEADS = 2
ATT_WIDTH = ATT_HEADS * HEAD_DIM
ROPE_THETA = 10000.0
ROPE_AXIS_FREQS = HEAD_DIM // 4
RET_HEADS = 4
RET_DK = 64
RET_WIDTH = 256
SSD_HEADS = 4
SSD_WIDTH = 256
SSD_STATE = 64
N_EXPERTS = 256
TOP_K = 8
ROUTER_GROUPS = 8
ROUTER_TOPK_GROUPS = 4
GROUP_SIZE = N_EXPERTS // ROUTER_GROUPS
EXPERT_FF = 256
SHARED_FF = 256
ROUTED_SCALE = 2.5
DEEPNORM_ALPHA = (2 * DEPTH) ** 0.25
IN_SIZES = (512, 128, 128, 256, 256, 256, 256, 256, 256, 128, 128, 8)

LANES = 128
SUBLANES = 8
MIB = 1024 * 1024

QKV_W = 768
RET_OFF, RET_W = 768, 1024
SSD_OFF, SSD_W = 1792, 896
IN_W_PAD = 2688
CONV_CH = 512

OUT_CHUNKS = D_MODEL // LANES
MOE_ROWS = 256
MOE_VMEM_MIB = 62


def _cparams(n_grid, vmem_mib):
    return pltpu.CompilerParams(dimension_semantics=("arbitrary",) * n_grid,
                                vmem_limit_bytes=vmem_mib * MIB)


def _dot(a, b):
    return jnp.dot(a, b, preferred_element_type=F32)


def _dot_nt(a, b):
    return lax.dot_general(a, b, NT_DIMS, preferred_element_type=F32)


def _silu(x):
    return x * jax.nn.sigmoid(x)


def _softplus(x):
    return jnp.maximum(x, 0.0) + jnp.log1p(jnp.exp(-jnp.abs(x)))


def _mod_spec(modp, tiles_per_seq):
    if modp.shape[0] == 1:
        return pl.BlockSpec((1, 6, D_MODEL), lambda i: (0, 0, 0))
    return pl.BlockSpec((1, 6, D_MODEL), lambda i: (i // tiles_per_seq, 0, 0))


def _group_sum(x, g):
    hi = x.astype(BF16)
    lo = (x - hi.astype(F32)).astype(BF16)
    return _dot(hi, g) + _dot(lo, g)


def _swap16(x):
    n = x.shape[1]
    lane = lax.broadcasted_iota(I32, x.shape, 1)
    return jnp.where((lane & 16) == 0, pltpu.roll(x, n - 16, 1), pltpu.roll(x, 16, 1))


def _layer_norm(h, w, b):
    mu = jnp.mean(h, axis=-1, keepdims=True)
    d = h - mu
    var = jnp.mean(d * d, axis=-1, keepdims=True)
    return d * lax.rsqrt(var + 1e-5) * w + b


def _mod_kernel(cond_ref, w_ref, b_ref, o_ref):
    c = cond_ref[...]
    o_ref[0] = jnp.dot(_silu(c), w_ref[0], precision=HIGHEST, preferred_element_type=F32) + b_ref[0]


def _modulation(cond8, w_mod, b_mod):
    tn = 1536
    n_col = w_mod.shape[2] // tn
    return pl.pallas_call(
        _mod_kernel,
        grid=(DEPTH, n_col),
        in_specs=[pl.BlockSpec((8, D_MODEL), lambda l, j: (0, 0)),
                  pl.BlockSpec((1, D_MODEL, tn), lambda l, j: (l, 0, j)),
                  pl.BlockSpec((1, 1, tn), lambda l, j: (l, 0, j))],
        out_specs=pl.BlockSpec((1, 8, tn), lambda l, j: (l, 0, j)),
        out_shape=jax.ShapeDtypeStruct((DEPTH, 8, w_mod.shape[2]), F32),
        compiler_params=_cparams(2, 32),
        name="modulation",
    )(cond8, w_mod, b_mod.reshape(DEPTH, 1, -1))


def _proj_in_kernel(*refs, rope, emit_cache):
    x_ref, mod_ref, w_ref, qnw_ref, knw_ref, g_ref = refs[:6]
    pos = 6
    if rope:
        cos_ref, sin_ref = refs[pos:pos + 2]
        pos += 2
    q_ref, k_ref, v_ref, ret_ref, ssd_ref = refs[pos:pos + 5]
    pos += 5
    u = x_ref[...] * (1.0 + mod_ref[0, 1:2, :]) + mod_ref[0, 0:1, :]
    ub = u.astype(BF16)
    q = _dot(ub, w_ref[:, 0:512])
    k = _dot(ub, w_ref[:, 512:640])
    v = _dot(ub, w_ref[:, 640:768])
    g = g_ref[...]
    inv_hd = 1.0 / HEAD_DIM
    qn = q * lax.rsqrt(_group_sum(q * q, g) * inv_hd + 1e-6) * qnw_ref[...]
    kn = k * lax.rsqrt(_group_sum(k * k, g[0:LANES, 0:LANES]) * inv_hd + 1e-6) * knw_ref[...]
    if emit_cache:
        kf_ref, vf_ref = refs[pos:pos + 2]
        kf_ref[...] = kn
        vf_ref[...] = v
    if rope:
        c = cos_ref[...]
        s = sin_ref[...]
        kn = kn * c + _swap16(kn) * s
        qn = qn * jnp.concatenate([c] * 4, axis=1) + _swap16(qn) * jnp.concatenate([s] * 4, axis=1)
    q_ref[...] = (qn * (HEAD_DIM ** -0.5)).astype(BF16)
    k_ref[...] = kn.astype(BF16)
    v_ref[...] = v.astype(BF16)
    ret_ref[...] = _dot(ub, w_ref[:, RET_OFF:RET_OFF + RET_W])
    ssd_ref[...] = _dot(ub, w_ref[:, SSD_OFF:SSD_OFF + SSD_W])


def _proj_in(x, modp, w_in_p, qnw, knw, g512, rope_tabs, layer, seq_len, tt, emit_cache):
    t = x.shape[0]
    tiles_per_seq = seq_len // tt
    rope = rope_tabs is not None
    row = lambda i: (i, 0)
    const2 = lambda i: (0, 0)
    in_specs = [pl.BlockSpec((tt, D_MODEL), row),
                _mod_spec(modp, tiles_per_seq),
                pl.BlockSpec((None, D_MODEL, IN_W_PAD), lambda i: (layer, 0, 0)),
                pl.BlockSpec((None, 1, 512), lambda i: (layer, 0, 0)),
                pl.BlockSpec((None, 1, LANES), lambda i: (layer, 0, 0)),
                pl.BlockSpec((512, 512), const2)]
    args = [x, modp, w_in_p, qnw, knw, g512]
    if rope:
        tab = pl.BlockSpec((tt, LANES), lambda i: (i % tiles_per_seq, 0))
        in_specs += [tab, tab]
        args += list(rope_tabs)
    out_shape = [jax.ShapeDtypeStruct((t, 512), BF16), jax.ShapeDtypeStruct((t, LANES), BF16),
                 jax.ShapeDtypeStruct((t, LANES), BF16), jax.ShapeDtypeStruct((t, RET_W), F32),
                 jax.ShapeDtypeStruct((t, SSD_W), F32)]
    out_specs = [pl.BlockSpec((tt, 512), row), pl.BlockSpec((tt, LANES), row), pl.BlockSpec((tt, LANES), row),
                 pl.BlockSpec((tt, RET_W), row), pl.BlockSpec((tt, SSD_W), row)]
    if emit_cache:
        out_shape += [jax.ShapeDtypeStruct((t, LANES), F32)] * 2
        out_specs += [pl.BlockSpec((tt, LANES), row)] * 2
    return pl.pallas_call(
        functools.partial(_proj_in_kernel, rope=rope, emit_cache=emit_cache),
        grid=(t // tt,), in_specs=in_specs, out_specs=out_specs, out_shape=out_shape,
        compiler_params=_cparams(1, 48), name="proj_in",
    )(*args)


def _attn_kernel(*refs, with_cache):
    q_ref, k_ref, v_ref = refs[:3]
    o_ref = refs[-1]
    lane = lax.broadcasted_iota(I32, (1, LANES), 1)
    first = lane < HEAD_DIM
    parts = [(k_ref[...], v_ref[...])]
    if with_cache:
        parts.append((refs[3][...].astype(BF16), refs[4][...].astype(BF16)))
    zero = jnp.zeros((), BF16)
    kmask = [[jnp.where(first, kk, zero), jnp.where(first, zero, kk)] for kk, _ in parts]
    one = jnp.ones((), BF16)
    vaug = [[jnp.where(first, vv, one), jnp.where(first, one, vv)] for _, vv in parts]
    for j in range(ATT_HEADS // 2):
        qb = q_ref[:, LANES * j:LANES * (j + 1)]
        outs = []
        for g in range(ATT_KV_HEADS):
            ss = [_dot_nt(qb, km[g]) for km in kmask]
            m = ss[0].max(axis=-1, keepdims=True)
            for s in ss[1:]:
                m = jnp.maximum(m, s.max(axis=-1, keepdims=True))
            acc = None
            for s, va in zip(ss, vaug):
                o = _dot(jnp.exp(s - m).astype(BF16), va[g])
                acc = o if acc is None else acc + o
            outs.append(acc / pltpu.roll(acc, HEAD_DIM, 1))
        o_ref[:, LANES * j:LANES * (j + 1)] = jnp.where(first, outs[0], outs[1]).astype(BF16)


def _attention(q, k, v, cache, layer, n_seq, seq_len, tq):
    t = q.shape[0]
    nq = seq_len // tq
    in_specs = [pl.BlockSpec((tq, 512), lambda b, i: (b * nq + i, 0)),
                pl.BlockSpec((seq_len, LANES), lambda b, i: (b, 0)),
                pl.BlockSpec((seq_len, LANES), lambda b, i: (b, 0))]
    args = [q, k, v]
    if cache is not None:
        past = cache[0].shape[2]
        cspec = pl.BlockSpec((None, None, past, LANES), lambda b, i: (b, layer, 0, 0))
        in_specs += [cspec, cspec]
        args += list(cache)
    return pl.pallas_call(
        functools.partial(_attn_kernel, with_cache=cache is not None),
        grid=(n_seq, nq), in_specs=in_specs,
        out_specs=pl.BlockSpec((tq, 512), lambda b, i: (b * nq + i, 0)),
        out_shape=jax.ShapeDtypeStruct((t, 512), BF16),
        compiler_params=_cparams(2, 48), name="attention",
    )(*args)


def _scan_kernel(*refs, n, chunk, has_init, emit_state):
    (ret_ref, ssd_ref, hp_ref, hn_ref, cw_ref, cb_ref, dtb_ref, alog_ref, rld_ref,
     rnw_ref, dl_ref, snw_ref, g_ref) = refs[:13]
    pos = 13
    if has_init:
        s0r_ref, s0s_ref = refs[pos:pos + 2]
        pos += 2
    reto_ref, ssdo_ref = refs[pos:pos + 2]
    pos += 2
    if emit_state:
        str_ref, sts_ref = refs[pos:pos + 2]
        pos += 2
    of_ref, st_ref = refs[pos:pos + 2]

    C = chunk
    s = pl.program_id(1)
    bwd = s >= n
    c = jnp.where(bwd, 2 * n - 1 - s, s)

    def init_state(direction):
        if has_init:
            st_ref[0:2] = s0s_ref[0, direction]
            st_ref[2:4] = s0r_ref[0, direction]
        else:
            st_ref[...] = jnp.zeros(st_ref.shape, F32)

    @pl.when(s == 0)
    def _():
        init_state(0)

    @pl.when(s == n)
    def _():
        init_state(1)

    lane = lax.broadcasted_iota(I32, (1, LANES), 1)
    first = lane < HEAD_DIM
    ret = ret_ref[...]
    ssd = ssd_ref[...]

    xin = ssd[:, 0:CONV_CH]
    rowi = lax.broadcasted_iota(I32, (C, 1), 0)
    prev_row = jnp.where(c == 0, 0.0, hp_ref[SUBLANES - 1:SUBLANES, :])
    next_row = jnp.where(c == n - 1, 0.0, hn_ref[0:1, :])
    xp = jnp.where(rowi == 0, prev_row, pltpu.roll(xin, 1, 0))
    xn = jnp.where(rowi == C - 1, next_row, pltpu.roll(xin, C - 1, 0))
    xbc = _silu(xp * cw_ref[0:1, :] + xin * cw_ref[1:2, :] + xn * cw_ref[2:3, :] + cb_ref[...])
    sx = xbc[:, 0:256]
    sb = xbc[:, 256:384]
    sc = xbc[:, 384:512]

    def this_dir(a):
        return jnp.where(bwd, pltpu.roll(a, LANES - SSD_HEADS, 1), a)

    dt_all = _softplus(ssd[:, 768:896] + dtb_ref[...])
    la_ssd = this_dir(dt_all * (-jnp.exp(alog_ref[...])))
    dt_dir = this_dir(dt_all)
    lg = jnp.broadcast_to(jnp.log1p(-jnp.exp(rld_ref[...])), (SUBLANES, LANES))
    lg_dir = pltpu.roll(this_dir(lg), SSD_HEADS, 1)[0:1, :]
    la = jnp.where(lane < 4, la_ssd, jnp.where(lane < 8, lg_dir, 0.0))
    ii = lax.broadcasted_iota(I32, (C, C), 0)
    jj = lax.broadcasted_iota(I32, (C, C), 1)
    msk = jnp.where(bwd, jj - ii, ii - jj) >= 0
    cum = jnp.dot(msk.astype(F32), la, precision=HIGHEST, preferred_element_type=F32)
    last = jnp.where(bwd, cum[0:1, :], cum[C - 1:C, :])
    acol = jnp.where(lane < 8, cum, jnp.where(lane < 12, pltpu.roll(dt_dir, 8, 1), 0.0))
    arow = acol.T

    def decay(h, scale):
        diff = cum[:, h:h + 1] - arow[h:h + 1, :]
        return jnp.where(msk, jnp.exp(jnp.where(msk, diff, 0.0)), 0.0) * scale

    def lanes2(a0, a1):
        return jnp.where(first, a0, a1)

    ii2 = lax.broadcasted_iota(I32, (LANES, LANES), 0)
    jj2 = lax.broadcasted_iota(I32, (LANES, LANES), 1)
    blockdiag = ((ii2 < HEAD_DIM) == (jj2 < HEAD_DIM)).astype(F32)

    o_pairs = []
    for kind in ("ssd", "ret"):
        for p in range(2):
            slot = p if kind == "ssd" else 2 + p
            heads = (2 * p, 2 * p + 1)
            hl = [h if kind == "ssd" else 4 + h for h in heads]
            if kind == "ssd":
                grp = first if p == 0 else jnp.logical_not(first)
                qm = jnp.where(grp, sc, 0.0).astype(BF16)
                km = jnp.where(grp, sb, 0.0)
                v128 = sx[:, LANES * p:LANES * (p + 1)]
                sg = _dot_nt(qm, sb.astype(BF16))
                scores = [sg, sg]
                rs = [arow[8 + h:9 + h, :] for h in heads]
                cs = [acol[:, 8 + h:9 + h] for h in heads]
                q_inter = qm
            else:
                q128 = ret[:, LANES * p:LANES * (p + 1)]
                km = ret[:, 256 + LANES * p:256 + LANES * (p + 1)]
                v128 = ret[:, 512 + LANES * p:512 + LANES * (p + 1)]
                kb = km.astype(BF16)
                scores = [_dot_nt(jnp.where(first, q128, 0.0).astype(BF16), kb),
                          _dot_nt(jnp.where(first, 0.0, q128).astype(BF16), kb)]
                rs = [RET_DK ** -0.5] * 2
                cs = [RET_DK ** -0.5] * 2
                q_inter = q128.astype(BF16)
            vb = v128.astype(BF16)
            intra = [_dot((scores[t] * decay(hl[t], rs[t])).astype(BF16), vb) for t in range(2)]
            state = st_ref[slot]
            ecum = lanes2(jnp.exp(cum[:, hl[0]:hl[0] + 1]), jnp.exp(cum[:, hl[1]:hl[1] + 1]))
            o_pairs.append(lanes2(intra[0], intra[1]) + _dot(q_inter, state.astype(BF16)) * ecum)
            wv = lanes2(jnp.exp(last[:, hl[0]:hl[0] + 1] - cum[:, hl[0]:hl[0] + 1]) * cs[0],
                        jnp.exp(last[:, hl[1]:hl[1] + 1] - cum[:, hl[1]:hl[1] + 1]) * cs[1])
            snew = _dot(km.T.astype(BF16), (v128 * wv).astype(BF16))
            if kind == "ret":
                snew = snew * blockdiag
            elast = lanes2(jnp.exp(last[:, hl[0]:hl[0] + 1]), jnp.exp(last[:, hl[1]:hl[1] + 1]))
            st_ref[slot] = elast * state + snew
    o_all = jnp.concatenate(o_pairs, axis=1)
    rows = pl.ds(pl.multiple_of(c * C, C), C)

    @pl.when(jnp.logical_not(bwd))
    def _():
        of_ref[rows, :] = o_all

    @pl.when(bwd)
    def _():
        tot = of_ref[rows, :] + o_all
        g = g_ref[...]
        inv = 1.0 / HEAD_DIM
        for p in range(2):
            o = tot[:, 256 + LANES * p:256 + LANES * (p + 1)]
            d = o - _group_sum(o, g) * inv
            var = _group_sum(d * d, g) * inv
            rn = d * lax.rsqrt(var + 1e-5) * rnw_ref[:, LANES * p:LANES * (p + 1)]
            gate = ret[:, 768 + LANES * p:768 + LANES * (p + 1)]
            reto_ref[:, LANES * p:LANES * (p + 1)] = (rn * _silu(gate)).astype(BF16)
        y = (tot[:, 0:256] + dl_ref[...] * sx) * _silu(ssd[:, 512:768])
        ms = jnp.mean(y * y, axis=-1, keepdims=True)
        ssdo_ref[...] = (y * lax.rsqrt(ms + 1e-6) * snw_ref[...]).astype(BF16)

    if emit_state:
        @pl.when(s == n - 1)
        def _():
            sts_ref[0, 0] = st_ref[0:2]
            str_ref[0, 0] = st_ref[2:4]

        @pl.when(s == 2 * n - 1)
        def _():
            sts_ref[0, 1] = st_ref[0:2]
            str_ref[0, 1] = st_ref[2:4]


def _scan(ret, ssd, pw, layer, init, n_seq, seq_len, chunk, emit_state):
    t = ret.shape[0]
    n = seq_len // chunk
    rows8 = t // SUBLANES

    def cidx(s):
        return jnp.where(s >= n, 2 * n - 1 - s, s)

    def oidx(s):
        return jnp.where(s >= n, 2 * n - 1 - s, n - 1)

    par = lambda w: pl.BlockSpec((None,) + w, lambda b, s: (layer,) + (0,) * len(w))
    in_specs = [pl.BlockSpec((chunk, RET_W), lambda b, s: (b * n + cidx(s), 0)),
                pl.BlockSpec((chunk, SSD_W), lambda b, s: (b * n + cidx(s), 0)),
                pl.BlockSpec((SUBLANES, CONV_CH),
                             lambda b, s: (jnp.maximum((b * seq_len + cidx(s) * chunk) // SUBLANES - 1, 0), 0)),
                pl.BlockSpec((SUBLANES, CONV_CH),
                             lambda b, s: (jnp.minimum((b * seq_len + (cidx(s) + 1) * chunk) // SUBLANES, rows8 - 1), 0)),
                par((3, CONV_CH)), par((1, CONV_CH)), par((1, LANES)), par((1, LANES)), par((1, LANES)),
                par((1, RET_WIDTH)), par((1, SSD_WIDTH)), par((1, SSD_WIDTH)),
                pl.BlockSpec((LANES, LANES), lambda b, s: (0, 0))]
    args = [ret, ssd, ssd, ssd, pw["conv_w"], pw["conv_b"], pw["dt_bias"], pw["a_log"], pw["ret_ld"],
            pw["ret_nw"], pw["d_lane"], pw["ssd_nw"], pw["g128"]]
    st_spec = pl.BlockSpec((1, 2, 2, LANES, LANES), lambda b, s: (b, 0, 0, 0, 0))
    if init is not None:
        in_specs += [st_spec, st_spec]
        args += list(init)
    out_specs = [pl.BlockSpec((chunk, RET_WIDTH), lambda b, s: (b * n + oidx(s), 0)),
                 pl.BlockSpec((chunk, SSD_WIDTH), lambda b, s: (b * n + oidx(s), 0))]
    out_shape = [jax.ShapeDtypeStruct((t, RET_WIDTH), BF16), jax.ShapeDtypeStruct((t, SSD_WIDTH), BF16)]
    if emit_state:
        out_specs += [st_spec, st_spec]
        out_shape += [jax.ShapeDtypeStruct((n_seq, 2, 2, LANES, LANES), F32)] * 2
    return pl.pallas_call(
        functools.partial(_scan_kernel, n=n, chunk=chunk, has_init=init is not None, emit_state=emit_state),
        grid=(n_seq, 2 * n), in_specs=in_specs, out_specs=out_specs, out_shape=out_shape,
        scratch_shapes=[pltpu.VMEM((seq_len, 512), F32), pltpu.VMEM((4, LANES, LANES), F32)],
        compiler_params=_cparams(2, 48), name="scan",
    )(*args)


def _proj_out_kernel(x_ref, att_ref, ro_ref, so_ref, mod_ref, wo_ref, lnw_ref, lnb_ref, rwh_ref, rwl_ref,
                     x1_ref, u2_ref, lg_ref):
    mix = (_dot(att_ref[...], wo_ref[0:512, :]) + _dot(ro_ref[...], wo_ref[512:768, :])
           + _dot(so_ref[...], wo_ref[768:1024, :]))
    h = DEEPNORM_ALPHA * x_ref[...] + mod_ref[0, 2:3, :] * mix
    x1 = _layer_norm(h, lnw_ref[...], lnb_ref[...])
    x1_ref[...] = x1
    u2 = x1 * (1.0 + mod_ref[0, 4:5, :]) + mod_ref[0, 3:4, :]
    hi = u2.astype(BF16)
    u2_ref[...] = hi
    lo = (u2 - hi.astype(F32)).astype(BF16)
    rwh = rwh_ref[...]
    lg_ref[...] = _dot_nt(rwh, hi) + _dot_nt(rwh, lo) + _dot_nt(rwl_ref[...], hi)


def _proj_out(x, att, ro, so, modp, w_out_p, ln_w, ln_b, rwh, rwl, layer, seq_len, tt):
    t = x.shape[0]
    tiles_per_seq = seq_len // tt
    row = lambda i: (i, 0)
    lsel = lambda *w: pl.BlockSpec((None,) + w, lambda i: (layer,) + (0,) * len(w))
    in_specs = [pl.BlockSpec((tt, D_MODEL), row), pl.BlockSpec((tt, 512), row),
                pl.BlockSpec((tt, RET_WIDTH), row), pl.BlockSpec((tt, SSD_WIDTH), row),
                _mod_spec(modp, tiles_per_seq),
                lsel(D_MODEL, D_MODEL),
                pl.BlockSpec((None, None, 1, D_MODEL), lambda i: (layer, 0, 0, 0)),
                pl.BlockSpec((None, None, 1, D_MODEL), lambda i: (layer, 0, 0, 0)),
                lsel(N_EXPERTS, D_MODEL), lsel(N_EXPERTS, D_MODEL)]
    out_shape = [jax.ShapeDtypeStruct((t, D_MODEL), F32), jax.ShapeDtypeStruct((t, D_MODEL), BF16),
                 jax.ShapeDtypeStruct((N_EXPERTS, t), F32)]
    out_specs = [pl.BlockSpec((tt, D_MODEL), row), pl.BlockSpec((tt, D_MODEL), row),
                 pl.BlockSpec((N_EXPERTS, tt), lambda i: (0, i))]
    return pl.pallas_call(
        _proj_out_kernel, grid=(t // tt,), in_specs=in_specs, out_specs=out_specs, out_shape=out_shape,
        compiler_params=_cparams(1, 48), name="proj_out",
    )(x, att, ro, so, modp, w_out_p, ln_w, ln_b, rwh, rwl)


def _route_kernel(lg_ref, bias_ref, idx_ref, rank_ref, wc_ref, cnt_ref, carry_ref):
    i = pl.program_id(0)

    @pl.when(i == 0)
    def _():
        carry_ref[...] = jnp.zeros(carry_ref.shape, F32)

    scores = jax.nn.sigmoid(lg_ref[...])
    sel = scores + bias_ref[...]
    n_e, tr = sel.shape
    neg = -jnp.inf
    sub = lax.broadcasted_iota(I32, (GROUP_SIZE, tr), 0).astype(F32)
    blocks = [sel[GROUP_SIZE * g:GROUP_SIZE * (g + 1), :] for g in range(ROUTER_GROUPS)]
    gscore = []
    for blk in blocks:
        m1 = blk.max(axis=0, keepdims=True)
        pos1 = jnp.min(jnp.where(blk == m1, sub, float(GROUP_SIZE)), axis=0, keepdims=True)
        m2 = jnp.max(jnp.where(sub == pos1, neg, blk), axis=0, keepdims=True)
        gscore.append(m1 + m2)
    masked = []
    for g in range(ROUTER_GROUPS):
        beat = jnp.zeros_like(gscore[g])
        for h in range(ROUTER_GROUPS):
            if h == g:
                continue
            wins = (gscore[h] >= gscore[g]) if h < g else (gscore[h] > gscore[g])
            beat = beat + wins.astype(F32)
        masked.append(jnp.where(beat < float(ROUTER_TOPK_GROUPS), blocks[g], neg))
    cur = jnp.concatenate(masked, axis=0)
    eio = lax.broadcasted_iota(I32, (n_e, tr), 0).astype(F32)
    chosen = jnp.zeros((n_e, tr), F32)
    picks = []
    for _ in range(TOP_K):
        m = cur.max(axis=0, keepdims=True)
        ik = jnp.min(jnp.where(cur == m, eio, float(n_e)), axis=0, keepdims=True)
        hit = eio == ik
        picks.append((ik, hit))
        cur = jnp.where(hit, neg, cur)
        chosen = jnp.where(hit, 1.0, chosen)
    picked = chosen * scores
    wc_ref[:, 0, :] = picked / jnp.sum(picked, axis=0, keepdims=True) * ROUTED_SCALE
    r_ = lax.broadcasted_iota(I32, (tr, tr), 0)
    c_ = lax.broadcasted_iota(I32, (tr, tr), 1)
    chosen_b = chosen.astype(BF16)
    before = _dot(chosen_b, (r_ < c_).astype(BF16)) + carry_ref[:, 0:1]
    for k, (ik, hit) in enumerate(picks):
        idx_ref[k:k + 1, :] = ik.astype(I32)
        rank_ref[k:k + 1, :] = jnp.sum(jnp.where(hit, before, 0.0), axis=0, keepdims=True).astype(I32)
    carry_ref[...] = carry_ref[...] + _dot(chosen_b, jnp.ones((tr, LANES), BF16))
    cnt_ref[...] = carry_ref[...]


def _route(lg_t, bias, layer, tr):
    t = lg_t.shape[1]
    col = lambda i: (0, i)
    return pl.pallas_call(
        _route_kernel, grid=(t // tr,),
        in_specs=[pl.BlockSpec((N_EXPERTS, tr), col),
                  pl.BlockSpec((None, N_EXPERTS, 1), lambda i: (layer, 0, 0))],
        out_specs=[pl.BlockSpec((TOP_K, tr), col), pl.BlockSpec((TOP_K, tr), col),
                   pl.BlockSpec((N_EXPERTS, 1, tr), lambda i: (0, 0, i)),
                   pl.BlockSpec((N_EXPERTS, LANES), lambda i: (0, 0))],
        out_shape=[jax.ShapeDtypeStruct((TOP_K, t), I32), jax.ShapeDtypeStruct((TOP_K, t), I32),
                   jax.ShapeDtypeStruct((N_EXPERTS, 1, t), F32), jax.ShapeDtypeStruct((N_EXPERTS, LANES), F32)],
        scratch_shapes=[pltpu.VMEM((N_EXPERTS, LANES), F32)],
        compiler_params=_cparams(1, 48), name="route",
    )(lg_t, bias)


def _slots_kernel(start_ref, idx_ref, rank_ref, slot_ref):
    n_e = start_ref.shape[0]
    tp = idx_ref.shape[1]
    eio = lax.broadcasted_iota(I32, (n_e, tp), 0)
    start = start_ref[...]
    for k in range(TOP_K):
        first = jnp.sum(jnp.where(eio == idx_ref[k:k + 1, :], start, 0.0), axis=0, keepdims=True)
        slot_ref[k:k + 1, :] = first.astype(I32) + rank_ref[k:k + 1, :]


def _plan_kernel(slot_ref, tok_ref, *, tp):
    base = pl.program_id(0) * tp

    def body(j, carry):
        for k in range(TOP_K):
            tok_ref[slot_ref[k, j]] = base + j
        return carry

    lax.fori_loop(0, tp, body, 0)


def _plan(start, idx_t, rank_t, tp):
    t = idx_t.shape[1]
    col = lambda i: (0, i)
    slots = pl.pallas_call(
        _slots_kernel, grid=(t // tp,),
        in_specs=[pl.BlockSpec((N_EXPERTS, 1), lambda i: (0, 0)),
                  pl.BlockSpec((TOP_K, tp), col), pl.BlockSpec((TOP_K, tp), col)],
        out_specs=pl.BlockSpec((TOP_K, tp), col),
        out_shape=jax.ShapeDtypeStruct((TOP_K, t), I32),
        compiler_params=_cparams(1, 32), name="slots",
    )(start.astype(F32).reshape(N_EXPERTS, 1), idx_t, rank_t)
    return pl.pallas_call(
        functools.partial(_plan_kernel, tp=tp), grid=(t // tp,),
        in_specs=[pl.BlockSpec((TOP_K, tp), col, memory_space=pltpu.SMEM)],
        out_specs=pl.BlockSpec(memory_space=pltpu.SMEM),
        out_shape=jax.ShapeDtypeStruct((t * TOP_K,), I32),
        compiler_params=_cparams(1, 32), name="plan",
    )(slots)


def _moe_kernel(start_ref, cnt_ref, tok_ref, wc_ref, x_hbm, wg_ref, wu_ref, wd_ref, out_hbm,
                xv_ref, acc_ref, xt_ref, y_ref, trow_ref, sem, *, n_tok, rows, sub, batch):
    e = pl.program_id(0)
    n_chunks = OUT_CHUNKS
    pair_rows = 2 * n_chunks
    stride = rows + SUBLANES

    @pl.when(e == 0)
    def _():
        cp = pltpu.make_async_copy(x_hbm, xv_ref.at[pl.ds(0, n_tok * n_chunks), :], sem)
        cp.start()
        acc_ref[...] = jnp.zeros(acc_ref.shape, F32)
        xt_ref[...] = jnp.zeros(xt_ref.shape, F32)
        xv_ref[pl.ds(n_tok * n_chunks, pair_rows), :] = jnp.zeros((pair_rows, LANES), BF16)
        cp.wait()

    n = cnt_ref[e]
    first = start_ref[e]
    wg = wg_ref[...].astype(BF16)
    wu = wu_ref[...].astype(BF16)
    wd = wd_ref[...].astype(BF16)

    def block(b, carry):
        base = first + b * rows
        left = jnp.minimum(n - b * rows, rows)

        def gather_row(r, t):
            trow_ref[r] = t
            pair = xv_ref[pl.ds(pl.multiple_of((t >> 1) * pair_rows, pair_rows), pair_rows), :].astype(F32)
            xt_ref[pl.ds(r, n_chunks, stride=stride), :] = jnp.where(
                (t & 1) == 1, pair[n_chunks:pair_rows], pair[0:n_chunks])

        for s0 in range(0, rows, sub):
            @pl.when(s0 + sub <= left)
            def _():
                for r in range(s0, s0 + sub):
                    gather_row(r, tok_ref[base + r])

            @pl.when(jnp.logical_and(s0 < left, left < s0 + sub))
            def _():
                for r in range(s0, s0 + sub):
                    valid = r < left
                    gather_row(r, jnp.where(valid, tok_ref[jnp.where(valid, base + r, 0)], n_tok))

        xb = jnp.concatenate([xt_ref[pl.ds(c * stride, rows), :] for c in range(n_chunks)], axis=1).astype(BF16)
        h = (_silu(_dot(xb, wg)) * _dot(xb, wu)).astype(BF16)
        y = _dot(h, wd)
        for c in range(n_chunks):
            y_ref[pl.ds(c * stride, rows), :] = y[:, LANES * c:LANES * (c + 1)]
        for s0 in range(0, rows, sub):
            @pl.when(s0 < left)
            def _():
                for r0 in range(s0, s0 + sub, batch):
                    pending = []
                    for r in range(r0, r0 + batch):
                        t = trow_ref[r]
                        wt = wc_ref[0, 0, jnp.minimum(t, n_tok - 1)]
                        dst = pl.ds(pl.multiple_of(t * n_chunks, n_chunks), n_chunks)
                        pending.append((dst, acc_ref[dst, :] + wt * y_ref[pl.ds(r, n_chunks, stride=stride), :]))
                    for dst, val in pending:
                        acc_ref[dst, :] = val
        return carry

    lax.fori_loop(0, (n + rows - 1) // rows, block, 0)

    @pl.when(e == pl.num_programs(0) - 1)
    def _():
        cp = pltpu.make_async_copy(acc_ref.at[pl.ds(0, n_tok * n_chunks), :], out_hbm, sem)
        cp.start()
        cp.wait()


def _moe(start, cnt, tok_sorted, wc, x_rows, w_gate, w_up, w_down, layer, n_tok, rows, vmem_mib):
    n_chunks = OUT_CHUNKS
    wspec = lambda a, b: pl.BlockSpec((None, None, a, b), lambda e, st, ct: (layer, e, 0, 0))
    grid_spec = pltpu.PrefetchScalarGridSpec(
        num_scalar_prefetch=2, grid=(N_EXPERTS,),
        in_specs=[pl.BlockSpec(memory_space=pltpu.SMEM),
                  pl.BlockSpec((1, 1, n_tok), lambda e, st, ct: (e, 0, 0), memory_space=pltpu.SMEM),
                  pl.BlockSpec(memory_space=pl.ANY),
                  wspec(D_MODEL, EXPERT_FF), wspec(D_MODEL, EXPERT_FF), wspec(EXPERT_FF, D_MODEL)],
        out_specs=pl.BlockSpec(memory_space=pl.ANY),
        scratch_shapes=[pltpu.VMEM(((n_tok + 2) * n_chunks, LANES), BF16),
                        pltpu.VMEM(((n_tok + 1) * n_chunks, LANES), F32),
                        pltpu.VMEM((n_chunks * (rows + SUBLANES), LANES), F32),
                        pltpu.VMEM((n_chunks * (rows + SUBLANES), LANES), F32),
                        pltpu.SMEM((rows,), I32),
                        pltpu.SemaphoreType.DMA(())])
    return pl.pallas_call(
        functools.partial(_moe_kernel, n_tok=n_tok, rows=rows, sub=64, batch=8),
        grid_spec=grid_spec, out_shape=jax.ShapeDtypeStruct((n_tok * n_chunks, LANES), F32),
        compiler_params=_cparams(1, vmem_mib), name="moe",
    )(start, cnt, tok_sorted, wc, x_rows, w_gate, w_up, w_down)


def _ffn_out_kernel(x1_ref, u2_ref, routed_ref, mod_ref, wgu_ref, wd_ref, lnw_ref, lnb_ref, o_ref):
    gu = _dot(u2_ref[...], wgu_ref[...])
    shared = _dot((_silu(gu[:, 0:SHARED_FF]) * gu[:, SHARED_FF:]).astype(BF16), wd_ref[...])
    tt = x1_ref.shape[0]
    routed = jnp.concatenate([routed_ref[pl.ds(c, tt, stride=OUT_CHUNKS), :] for c in range(OUT_CHUNKS)], axis=1)
    h = DEEPNORM_ALPHA * x1_ref[...] + mod_ref[0, 5:6, :] * (routed + shared)
    o_ref[...] = _layer_norm(h, lnw_ref[...], lnb_ref[...])


def _ffn_out(x1, u2, routed, modp, sh_gu, sh_d, ln_w, ln_b, layer, seq_len, tt):
    t = x1.shape[0]
    tiles_per_seq = seq_len // tt
    row = lambda i: (i, 0)
    lsel = lambda *w: pl.BlockSpec((None,) + w, lambda i: (layer,) + (0,) * len(w))
    ln_spec = pl.BlockSpec((None, None, 1, D_MODEL), lambda i: (layer, 1, 0, 0))
    return pl.pallas_call(
        _ffn_out_kernel, grid=(t // tt,),
        in_specs=[pl.BlockSpec((tt, D_MODEL), row), pl.BlockSpec((tt, D_MODEL), row),
                  pl.BlockSpec((tt * OUT_CHUNKS, LANES), row),
                  _mod_spec(modp, tiles_per_seq),
                  lsel(D_MODEL, 2 * SHARED_FF), lsel(SHARED_FF, D_MODEL), ln_spec, ln_spec],
        out_specs=pl.BlockSpec((tt, D_MODEL), row),
        out_shape=jax.ShapeDtypeStruct((t, D_MODEL), F32),
        compiler_params=_cparams(1, 48), name="ffn_out",
    )(x1, u2, routed, modp, sh_gu, sh_d, ln_w, ln_b)


def _q_perm():
    cols = []
    for j in range(ATT_HEADS // 2):
        for half in range(2):
            h = j + (ATT_HEADS // 2) * half
            cols.extend(range(h * HEAD_DIM, (h + 1) * HEAD_DIM))
    return np.asarray(cols, np.int32)


def _prepare_params(p):
    off = np.concatenate([[0], np.cumsum(IN_SIZES)])
    seg = lambda k: np.arange(off[k], off[k + 1], dtype=np.int32)
    qp = _q_perm()
    cols = np.concatenate([qp, seg(1), seg(2), seg(3), seg(4), seg(5), seg(6), seg(7), seg(9), seg(10), seg(8), seg(11)])
    w_in = jnp.take(p["w_in"], jnp.asarray(cols), axis=2)
    w_in = jnp.pad(w_in, ((0, 0), (0, 0), (0, IN_W_PAD - w_in.shape[2]))).astype(BF16)
    rows = np.concatenate([qp, np.arange(ATT_WIDTH, D_MODEL, dtype=np.int32)])
    w_out = jnp.take(p["w_out"], jnp.asarray(rows), axis=1).astype(BF16)
    rwt = jnp.swapaxes(p["router_w"], 1, 2)
    rwh = rwt.astype(BF16)
    rwl = (rwt - rwh.astype(F32)).astype(BF16)

    def lane_pad(a, fill=0.0):
        a = a.reshape(DEPTH, 1, -1)
        return jnp.pad(a, ((0, 0), (0, 0), (0, LANES - a.shape[2])), constant_values=fill)

    gi = np.arange(512) // HEAD_DIM
    g512 = jnp.asarray(gi[:, None] == gi[None, :], BF16)
    return {
        "w_in": w_in, "w_out": w_out, "rwh": rwh, "rwl": rwl,
        "qnw": jnp.tile(p["q_norm_w"], (1, ATT_HEADS)).reshape(DEPTH, 1, 512),
        "knw": jnp.tile(p["k_norm_w"], (1, ATT_KV_HEADS)).reshape(DEPTH, 1, LANES),
        "g512": g512, "g128": g512[0:LANES, 0:LANES],
        "conv_w": p["ssd_conv_w"], "conv_b": p["ssd_conv_b"].reshape(DEPTH, 1, CONV_CH),
        "dt_bias": lane_pad(p["ssd_dt_bias"]), "a_log": lane_pad(p["ssd_a_log"]),
        "ret_ld": lane_pad(p["ret_log_decay"], -1.0),
        "ret_nw": p["ret_norm_w"].reshape(DEPTH, 1, RET_WIDTH),
        "d_lane": jnp.repeat(p["ssd_d"], HEAD_DIM, axis=1).reshape(DEPTH, 1, SSD_WIDTH),
        "ssd_nw": p["ssd_norm_w"].reshape(DEPTH, 1, SSD_WIDTH),
        "ln_w": p["ln_w"].reshape(DEPTH, 2, 1, D_MODEL), "ln_b": p["ln_b"].reshape(DEPTH, 2, 1, D_MODEL),
        "router_bias": p["router_bias"].reshape(DEPTH, N_EXPERTS, 1),
        "sh_gu": jnp.concatenate([p["sh_w_gate"], p["sh_w_up"]], axis=2).astype(BF16),
        "sh_d": p["sh_w_down"].astype(BF16),
        "exp_w_gate": p["exp_w_gate"], "exp_w_up": p["exp_w_up"], "exp_w_down": p["exp_w_down"],
    }


def _rope_tables(n_tokens):
    t = jnp.arange(n_tokens, dtype=I32)
    rows = (t // GRID_W).astype(F32)
    cols = (t % GRID_W).astype(F32)
    inv = ROPE_THETA ** (-jnp.arange(ROPE_AXIS_FREQS, dtype=F32) / ROPE_AXIS_FREQS)
    ar = rows[:, None] * inv
    ac = cols[:, None] * inv
    cos = jnp.concatenate([jnp.cos(ar), jnp.cos(ar), jnp.cos(ac), jnp.cos(ac)], axis=1)
    sin = jnp.concatenate([-jnp.sin(ar), jnp.sin(ar), -jnp.sin(ac), jnp.sin(ac)], axis=1)
    return jnp.tile(cos, (1, 2)), jnp.tile(sin, (1, 2))


def _pair_states(st, kind):
    b = st.shape[0]
    out = jnp.zeros((b, 2, 2, LANES, LANES), F32)
    for pr in range(2):
        for tt in range(2):
            r0 = HEAD_DIM * (tt if kind == "ret" else pr)
            out = out.at[:, :, pr, r0:r0 + HEAD_DIM, HEAD_DIM * tt:HEAD_DIM * (tt + 1)].set(st[:, :, 2 * pr + tt])
    return out


def _unpair_states(m, kind):
    heads = []
    for pr in range(2):
        for tt in range(2):
            r0 = HEAD_DIM * (tt if kind == "ret" else pr)
            heads.append(m[:, :, pr, r0:r0 + HEAD_DIM, HEAD_DIM * tt:HEAD_DIM * (tt + 1)])
    return jnp.stack(heads, axis=2)


def _trunk_layer(x, modp, pw, layer, n_seq, seq_len, rope_tabs, cache, init, cfg):
    n_tok = x.shape[0]
    ctx = cache is None
    outs = _proj_in(x, modp, pw["w_in"], pw["qnw"], pw["knw"], pw["g512"], rope_tabs, layer, seq_len,
                    cfg["tt"], emit_cache=ctx)
    q, k, v, ret, ssd = outs[:5]
    att = _attention(q, k, v, cache, layer, n_seq, seq_len, cfg["tq"])
    sc = _scan(ret, ssd, pw, layer, init, n_seq, seq_len, cfg["chunk"], emit_state=ctx)
    x1, u2, lg_t = _proj_out(x, att, sc[0], sc[1], modp, pw["w_out"], pw["ln_w"], pw["ln_b"],
                                  pw["rwh"], pw["rwl"], layer, seq_len, cfg["tt"])
    idx_t, rank_t, wc, counts = _route(lg_t, pw["router_bias"], layer, cfg["tr"])
    cnt = counts[:, 0].astype(I32)
    start = jnp.cumsum(cnt) - cnt
    tok_sorted = _plan(start, idx_t, rank_t, cfg["tr"])
    u2_rows = u2.reshape(n_tok * OUT_CHUNKS, LANES)
    routed = _moe(start, cnt, tok_sorted, wc, u2_rows, pw["exp_w_gate"], pw["exp_w_up"], pw["exp_w_down"],
                  layer, n_tok, cfg["rows"], cfg["moe_vmem"])
    x_new = _ffn_out(x1, u2, routed, modp, pw["sh_gu"], pw["sh_d"], pw["ln_w"], pw["ln_b"], layer, seq_len, cfg["tt"])
    new_ctx = (outs[5], outs[6], sc[2], sc[3]) if ctx else None
    return x_new, new_ctx


def kernel(x_prompt, x_sample, cache_k, cache_v, state_ret, state_ssd, c, c_ctx,
           w_mod, b_mod, w_in, q_norm_w, k_norm_w, ret_log_decay, ret_norm_w,
           ssd_conv_w, ssd_conv_b, ssd_dt_bias, ssd_a_log, ssd_d, ssd_norm_w, w_out,
           ln_w, ln_b, router_w, router_bias, exp_w_gate, exp_w_up, exp_w_down,
           sh_w_gate, sh_w_up, sh_w_down):
    pw = _prepare_params({
        "w_in": w_in, "q_norm_w": q_norm_w, "k_norm_w": k_norm_w, "ret_log_decay": ret_log_decay,
        "ret_norm_w": ret_norm_w, "ssd_conv_w": ssd_conv_w, "ssd_conv_b": ssd_conv_b, "ssd_dt_bias": ssd_dt_bias,
        "ssd_a_log": ssd_a_log, "ssd_d": ssd_d, "ssd_norm_w": ssd_norm_w, "w_out": w_out, "ln_w": ln_w, "ln_b": ln_b,
        "router_w": router_w, "router_bias": router_bias, "exp_w_gate": exp_w_gate, "exp_w_up": exp_w_up,
        "exp_w_down": exp_w_down, "sh_w_gate": sh_w_gate, "sh_w_up": sh_w_up, "sh_w_down": sh_w_down})
    nb, seq, d = x_prompt.shape
    nd, dseq, _ = x_sample.shape
    cond = jnp.concatenate([c_ctx[None], c, jnp.zeros((8 - 1 - nd, d), F32)], axis=0)
    mod = _modulation(cond, w_mod, b_mod)
    rope_tabs = _rope_tables(dseq)
    past = cache_k.shape[2]
    ck = cache_k.reshape(nd, DEPTH, past, LANES)
    cv = cache_v.reshape(nd, DEPTH, past, LANES)
    cfg_ctx = {"tt": 512, "tq": seq, "chunk": seq, "tr": 512, "rows": MOE_ROWS, "moe_vmem": MOE_VMEM_MIB}
    cfg_lat = {"tt": 512, "tq": 512, "chunk": 256, "tr": 512, "rows": MOE_ROWS, "moe_vmem": MOE_VMEM_MIB}

    xc = x_prompt.reshape(nb * seq, d)
    xs = x_sample.reshape(nd * dseq, d)
    ks, vs, srs, sss = [], [], [], []
    for l in range(DEPTH):
        mod_ctx = mod[l, 0:1].reshape(1, 6, d)
        mod_lat = mod[l, 1:1 + nd].reshape(nd, 6, d)
        xc, (k_l, v_l, sr_l, ss_l) = _trunk_layer(xc, mod_ctx, pw, l, nb, seq, None, None, None, cfg_ctx)
        ks.append(k_l.reshape(nb, seq, ATT_KV_HEADS, HEAD_DIM))
        vs.append(v_l.reshape(nb, seq, ATT_KV_HEADS, HEAD_DIM))
        srs.append(_unpair_states(sr_l, "ret"))
        sss.append(_unpair_states(ss_l, "ssd"))
        init = (_pair_states(state_ret[:, l], "ret"), _pair_states(state_ssd[:, l], "ssd"))
        xs, _ = _trunk_layer(xs, mod_lat, pw, l, nd, dseq, rope_tabs, (ck, cv), init, cfg_lat)
    return (xc.reshape(nb, seq, d), xs.reshape(nd, dseq, d), jnp.stack(ks, axis=1), jnp.stack(vs, axis=1),
            jnp.stack(srs, axis=1), jnp.stack(sss, axis=1))
```

```python
import functools

import numpy as np
import jax
import jax.numpy as jnp
from jax import lax
from jax.experimental import pallas as pl
from jax.experimental.pallas import tpu as pltpu

F32 = jnp.float32
BF16 = jnp.bfloat16
I32 = jnp.int32
HIGHEST = lax.Precision.HIGHEST
NT_DIMS = (((1,), (1,)), ((), ()))

D_MODEL = 1024
DEPTH = 4
GRID_W = 64
HEAD_DIM = 64
ATT_HEADS = 8
ATT_KV_HEADS = 2
ATT_WIDTH = ATT_HEADS * HEAD_DIM
ROPE_THETA = 10000.0
ROPE_AXIS_FREQS = HEAD_DIM // 4
RET_HEADS = 4
RET_DK = 64
RET_WIDTH = 256
SSD_HEADS = 4
SSD_WIDTH = 256
SSD_STATE = 64
N_EXPERTS = 256
TOP_K = 8
ROUTER_GROUPS = 8
ROUTER_TOPK_GROUPS = 4
GROUP_SIZE = N_EXPERTS // ROUTER_GROUPS
EXPERT_FF = 256
SHARED_FF = 256
ROUTED_SCALE = 2.5
DEEPNORM_ALPHA = (2 * DEPTH) ** 0.25
IN_SIZES = (512, 128, 128, 256, 256, 256, 256, 256, 256, 128, 128, 8)

LANES = 128
SUBLANES = 8
MIB = 1024 * 1024

QKV_W = 768
RET_OFF, RET_W = 768, 1024
SSD_OFF, SSD_W = 1792, 896
IN_W_PAD = 2688
CONV_CH = 512

OUT_CHUNKS = D_MODEL // LANES
SCAN_CHUNK = 256
MOE_ROWS = 256
MOE_VMEM_MIB = 62


def _cparams(n_grid, vmem_mib):
    return pltpu.CompilerParams(dimension_semantics=("arbitrary",) * n_grid,
                                vmem_limit_bytes=vmem_mib * MIB)


def _dot(a, b):
    return jnp.dot(a, b, preferred_element_type=F32)


def _dot_nt(a, b):
    return lax.dot_general(a, b, NT_DIMS, preferred_element_type=F32)


def _silu(x):
    return x * jax.nn.sigmoid(x)


def _softplus(x):
    return jnp.maximum(x, 0.0) + jnp.log1p(jnp.exp(-jnp.abs(x)))


def _mod_spec(modp, tiles_per_seq):
    if modp.shape[0] == 1:
        return pl.BlockSpec((1, 6, D_MODEL), lambda i: (0, 0, 0))
    return pl.BlockSpec((1, 6, D_MODEL), lambda i: (i // tiles_per_seq, 0, 0))


def _group_sum(x, g):
    hi = x.astype(BF16)
    lo = (x - hi.astype(F32)).astype(BF16)
    return _dot(hi, g) + _dot(lo, g)


def _swap16(x):
    n = x.shape[1]
    lane = lax.broadcasted_iota(I32, x.shape, 1)
    return jnp.where((lane & 16) == 0, pltpu.roll(x, n - 16, 1), pltpu.roll(x, 16, 1))


def _layer_norm(h, w, b):
    mu = jnp.mean(h, axis=-1, keepdims=True)
    d = h - mu
    var = jnp.mean(d * d, axis=-1, keepdims=True)
    return d * lax.rsqrt(var + 1e-5) * w + b


def _mod_kernel(cond_ref, w_ref, b_ref, o_ref):
    c = cond_ref[...]
    o_ref[0] = jnp.dot(_silu(c), w_ref[0], precision=HIGHEST, preferred_element_type=F32) + b_ref[0]


def _modulation(cond8, w_mod, b_mod):
    tn = 1536
    n_col = w_mod.shape[2] // tn
    return pl.pallas_call(
        _mod_kernel,
        grid=(DEPTH, n_col),
        in_specs=[pl.BlockSpec((8, D_MODEL), lambda l, j: (0, 0)),
                  pl.BlockSpec((1, D_MODEL, tn), lambda l, j: (l, 0, j)),
                  pl.BlockSpec((1, 1, tn), lambda l, j: (l, 0, j))],
        out_specs=pl.BlockSpec((1, 8, tn), lambda l, j: (l, 0, j)),
        out_shape=jax.ShapeDtypeStruct((DEPTH, 8, w_mod.shape[2]), F32),
        compiler_params=_cparams(2, 32),
        name="modulation",
    )(cond8, w_mod, b_mod.reshape(DEPTH, 1, -1))


def _proj_in_kernel(*refs, rope, emit_cache):
    x_ref, mod_ref, w_ref, qnw_ref, knw_ref, g_ref = refs[:6]
    pos = 6
    if rope:
        cos_ref, sin_ref = refs[pos:pos + 2]
        pos += 2
    q_ref, k_ref, v_ref, ret_ref, ssd_ref = refs[pos:pos + 5]
    pos += 5
    u = x_ref[...] * (1.0 + mod_ref[0, 1:2, :]) + mod_ref[0, 0:1, :]
    ub = u.astype(BF16)
    q = _dot(ub, w_ref[:, 0:512])
    k = _dot(ub, w_ref[:, 512:640])
    v = _dot(ub, w_ref[:, 640:768])
    g = g_ref[...]
    inv_hd = 1.0 / HEAD_DIM
    qn = q * lax.rsqrt(_group_sum(q * q, g) * inv_hd + 1e-6) * qnw_ref[...]
    kn = k * lax.rsqrt(_group_sum(k * k, g[0:LANES, 0:LANES]) * inv_hd + 1e-6) * knw_ref[...]
    if emit_cache:
        kf_ref, vf_ref = refs[pos:pos + 2]
        kf_ref[...] = kn
        vf_ref[...] = v
    if rope:
        c = cos_ref[...]
        s = sin_ref[...]
        kn = kn * c + _swap16(kn) * s
        qn = qn * jnp.concatenate([c] * 4, axis=1) + _swap16(qn) * jnp.concatenate([s] * 4, axis=1)
    q_ref[...] = (qn * (HEAD_DIM ** -0.5)).astype(BF16)
    k_ref[...] = kn.astype(BF16)
    v_ref[...] = v.astype(BF16)
    ret_ref[...] = _dot(ub, w_ref[:, RET_OFF:RET_OFF + RET_W])
    ssd_ref[...] = _dot(ub, w_ref[:, SSD_OFF:SSD_OFF + SSD_W])


def _proj_in(x, modp, w_in_p, qnw, knw, g512, rope_tabs, layer, seq_len, tt, emit_cache):
    t = x.shape[0]
    tiles_per_seq = seq_len // tt
    rope = rope_tabs is not None
    row = lambda i: (i, 0)
    const2 = lambda i: (0, 0)
    in_specs = [pl.BlockSpec((tt, D_MODEL), row),
                _mod_spec(modp, tiles_per_seq),
                pl.BlockSpec((None, D_MODEL, IN_W_PAD), lambda i: (layer, 0, 0)),
                pl.BlockSpec((None, 1, 512), lambda i: (layer, 0, 0)),
                pl.BlockSpec((None, 1, LANES), lambda i: (layer, 0, 0)),
                pl.BlockSpec((512, 512), const2)]
    args = [x, modp, w_in_p, qnw, knw, g512]
    if rope:
        tab = pl.BlockSpec((tt, LANES), lambda i: (i % tiles_per_seq, 0))
        in_specs += [tab, tab]
        args += list(rope_tabs)
    out_shape = [jax.ShapeDtypeStruct((t, 512), BF16), jax.ShapeDtypeStruct((t, LANES), BF16),
                 jax.ShapeDtypeStruct((t, LANES), BF16), jax.ShapeDtypeStruct((t, RET_W), F32),
                 jax.ShapeDtypeStruct((t, SSD_W), F32)]
    out_specs = [pl.BlockSpec((tt, 512), row), pl.BlockSpec((tt, LANES), row), pl.BlockSpec((tt, LANES), row),
                 pl.BlockSpec((tt, RET_W), row), pl.BlockSpec((tt, SSD_W), row)]
    if emit_cache:
        out_shape += [jax.ShapeDtypeStruct((t, LANES), F32)] * 2
        out_specs += [pl.BlockSpec((tt, LANES), row)] * 2
    return pl.pallas_call(
        functools.partial(_proj_in_kernel, rope=rope, emit_cache=emit_cache),
        grid=(t // tt,), in_specs=in_specs, out_specs=out_specs, out_shape=out_shape,
        compiler_params=_cparams(1, 48), name="proj_in",
    )(*args)


def _attn_kernel(*refs, with_cache, group):
    q_ref, k_ref, v_ref = refs[:3]
    n_in = 5 if with_cache else 3
    o_ref = refs[n_in]
    s_refs = refs[n_in + 1:n_in + 3]
    p_refs = refs[n_in + 3:n_in + 5]
    tq = q_ref.shape[0]
    lane = lax.broadcasted_iota(I32, (1, LANES), 1)
    first = lane < HEAD_DIM
    parts = [(k_ref[...], v_ref[...])]
    if with_cache:
        parts.append((refs[3][...].astype(BF16), refs[4][...].astype(BF16)))
    bounds = np.cumsum([0] + [kk.shape[0] for kk, _ in parts])
    zero = jnp.zeros((), BF16)
    kmask = [[jnp.where(first, kk, zero), jnp.where(first, zero, kk)] for kk, _ in parts]
    one = jnp.ones((), BF16)
    vaug = [[jnp.where(first, vv, one), jnp.where(first, one, vv)] for _, vv in parts]
    head = 0
    for j in range(ATT_HEADS // 2):
        qb = q_ref[:, LANES * j:LANES * (j + 1)]
        outs = []
        for g in range(ATT_KV_HEADS):
            s_ref, p_ref = s_refs[head % 2], p_refs[head % 2]
            head += 1
            for p, km in enumerate(kmask):
                s_ref[:, bounds[p]:bounds[p + 1]] = _dot_nt(qb, km[g])
            for r0 in range(0, tq, group):
                s = s_ref[r0:r0 + group, :]
                p_ref[r0:r0 + group, :] = jnp.exp(s - s.max(axis=-1, keepdims=True)).astype(BF16)
            acc = None
            for p, va in enumerate(vaug):
                o = _dot(p_ref[:, bounds[p]:bounds[p + 1]], va[g])
                acc = o if acc is None else acc + o
            outs.append(acc / pltpu.roll(acc, HEAD_DIM, 1))
        o_ref[:, LANES * j:LANES * (j + 1)] = jnp.where(first, outs[0], outs[1]).astype(BF16)


def _attention(q, k, v, cache, layer, n_seq, seq_len, tq):
    t = q.shape[0]
    nq = seq_len // tq
    in_specs = [pl.BlockSpec((tq, 512), lambda b, i: (b * nq + i, 0)),
                pl.BlockSpec((seq_len, LANES), lambda b, i: (b, 0)),
                pl.BlockSpec((seq_len, LANES), lambda b, i: (b, 0))]
    args = [q, k, v]
    n_keys = seq_len
    if cache is not None:
        past = cache[0].shape[2]
        n_keys += past
        cspec = pl.BlockSpec((None, None, past, LANES), lambda b, i: (b, layer, 0, 0))
        in_specs += [cspec, cspec]
        args += list(cache)
    return pl.pallas_call(
        functools.partial(_attn_kernel, with_cache=cache is not None, group=2 * SUBLANES),
        grid=(n_seq, nq), in_specs=in_specs,
        out_specs=pl.BlockSpec((tq, 512), lambda b, i: (b * nq + i, 0)),
        out_shape=jax.ShapeDtypeStruct((t, 512), BF16),
        scratch_shapes=[pltpu.VMEM((tq, n_keys), F32)] * 2 + [pltpu.VMEM((tq, n_keys), BF16)] * 2,
        compiler_params=_cparams(2, 56), name="attention",
    )(*args)


def _scan_kernel(*refs, n, chunk, has_init, emit_state):
    (ret_ref, ssd_ref, hp_ref, hn_ref, cw_ref, cb_ref, dtb_ref, alog_ref, rld_ref,
     rnw_ref, dl_ref, snw_ref, g_ref) = refs[:13]
    pos = 13
    if has_init:
        s0r_ref, s0s_ref = refs[pos:pos + 2]
        pos += 2
    reto_ref, ssdo_ref = refs[pos:pos + 2]
    pos += 2
    if emit_state:
        str_ref, sts_ref = refs[pos:pos + 2]
        pos += 2
    of_ref, st_ref = refs[pos:pos + 2]

    C = chunk
    s = pl.program_id(1)
    bwd = s >= n
    c = jnp.where(bwd, 2 * n - 1 - s, s)

    def init_state(direction):
        if has_init:
            st_ref[0:2] = s0s_ref[0, direction]
            st_ref[2:4] = s0r_ref[0, direction]
        else:
            st_ref[...] = jnp.zeros(st_ref.shape, F32)

    @pl.when(s == 0)
    def _():
        init_state(0)

    @pl.when(s == n)
    def _():
        init_state(1)

    lane = lax.broadcasted_iota(I32, (1, LANES), 1)
    first = lane < HEAD_DIM
    ret = ret_ref[...]
    ssd = ssd_ref[...]

    xin = ssd[:, 0:CONV_CH]
    rowi = lax.broadcasted_iota(I32, (C, 1), 0)
    prev_row = jnp.where(c == 0, 0.0, hp_ref[SUBLANES - 1:SUBLANES, :])
    next_row = jnp.where(c == n - 1, 0.0, hn_ref[0:1, :])
    xp = jnp.where(rowi == 0, prev_row, pltpu.roll(xin, 1, 0))
    xn = jnp.where(rowi == C - 1, next_row, pltpu.roll(xin, C - 1, 0))
    xbc = _silu(xp * cw_ref[0:1, :] + xin * cw_ref[1:2, :] + xn * cw_ref[2:3, :] + cb_ref[...])
    sx = xbc[:, 0:256]
    sb = xbc[:, 256:384]
    sc = xbc[:, 384:512]

    def this_dir(a):
        return jnp.where(bwd, pltpu.roll(a, LANES - SSD_HEADS, 1), a)

    dt_all = _softplus(ssd[:, 768:896] + dtb_ref[...])
    la_ssd = this_dir(dt_all * (-jnp.exp(alog_ref[...])))
    dt_dir = this_dir(dt_all)
    lg = jnp.broadcast_to(jnp.log1p(-jnp.exp(rld_ref[...])), (SUBLANES, LANES))
    lg_dir = pltpu.roll(this_dir(lg), SSD_HEADS, 1)[0:1, :]
    la = jnp.where(lane < 4, la_ssd, jnp.where(lane < 8, lg_dir, 0.0))
    ii = lax.broadcasted_iota(I32, (C, C), 0)
    jj = lax.broadcasted_iota(I32, (C, C), 1)
    msk = jnp.where(bwd, jj - ii, ii - jj) >= 0
    cum = jnp.dot(msk.astype(F32), la, precision=HIGHEST, preferred_element_type=F32)
    last = jnp.where(bwd, cum[0:1, :], cum[C - 1:C, :])
    acol = jnp.where(lane < 8, cum, jnp.where(lane < 12, pltpu.roll(dt_dir, 8, 1), 0.0))
    arow = acol.T

    def decay(h, scale):
        diff = cum[:, h:h + 1] - arow[h:h + 1, :]
        return jnp.where(msk, jnp.exp(jnp.where(msk, diff, 0.0)), 0.0) * scale

    def lanes2(a0, a1):
        return jnp.where(first, a0, a1)

    ii2 = lax.broadcasted_iota(I32, (LANES, LANES), 0)
    jj2 = lax.broadcasted_iota(I32, (LANES, LANES), 1)
    blockdiag = ((ii2 < HEAD_DIM) == (jj2 < HEAD_DIM)).astype(F32)

    o_pairs = []
    for kind in ("ssd", "ret"):
        for p in range(2):
            slot = p if kind == "ssd" else 2 + p
            heads = (2 * p, 2 * p + 1)
            hl = [h if kind == "ssd" else 4 + h for h in heads]
            if kind == "ssd":
                grp = first if p == 0 else jnp.logical_not(first)
                qm = jnp.where(grp, sc, 0.0).astype(BF16)
                km = jnp.where(grp, sb, 0.0)
                v128 = sx[:, LANES * p:LANES * (p + 1)]
                sg = _dot_nt(qm, sb.astype(BF16))
                scores = [sg, sg]
                rs = [arow[8 + h:9 + h, :] for h in heads]
                cs = [acol[:, 8 + h:9 + h] for h in heads]
                q_inter = qm
            else:
                q128 = ret[:, LANES * p:LANES * (p + 1)]
                km = ret[:, 256 + LANES * p:256 + LANES * (p + 1)]
                v128 = ret[:, 512 + LANES * p:512 + LANES * (p + 1)]
                kb = km.astype(BF16)
                scores = [_dot_nt(jnp.where(first, q128, 0.0).astype(BF16), kb),
                          _dot_nt(jnp.where(first, 0.0, q128).astype(BF16), kb)]
                rs = [RET_DK ** -0.5] * 2
                cs = [RET_DK ** -0.5] * 2
                q_inter = q128.astype(BF16)
            vb = v128.astype(BF16)
            intra = [_dot((scores[t] * decay(hl[t], rs[t])).astype(BF16), vb) for t in range(2)]
            state = st_ref[slot]
            ecum = lanes2(jnp.exp(cum[:, hl[0]:hl[0] + 1]), jnp.exp(cum[:, hl[1]:hl[1] + 1]))
            o_pairs.append(lanes2(intra[0], intra[1]) + _dot(q_inter, state.astype(BF16)) * ecum)
            wv = lanes2(jnp.exp(last[:, hl[0]:hl[0] + 1] - cum[:, hl[0]:hl[0] + 1]) * cs[0],
                        jnp.exp(last[:, hl[1]:hl[1] + 1] - cum[:, hl[1]:hl[1] + 1]) * cs[1])
            snew = _dot(km.T.astype(BF16), (v128 * wv).astype(BF16))
            if kind == "ret":
                snew = snew * blockdiag
            elast = lanes2(jnp.exp(last[:, hl[0]:hl[0] + 1]), jnp.exp(last[:, hl[1]:hl[1] + 1]))
            st_ref[slot] = elast * state + snew
    o_all = jnp.concatenate(o_pairs, axis=1)
    rows = pl.ds(pl.multiple_of(c * C, C), C)

    @pl.when(jnp.logical_not(bwd))
    def _():
        of_ref[rows, :] = o_all

    @pl.when(bwd)
    def _():
        tot = of_ref[rows, :] + o_all
        g = g_ref[...]
        inv = 1.0 / HEAD_DIM
        for p in range(2):
            o = tot[:, 256 + LANES * p:256 + LANES * (p + 1)]
            d = o - _group_sum(o, g) * inv
            var = _group_sum(d * d, g) * inv
            rn = d * lax.rsqrt(var + 1e-5) * rnw_ref[:, LANES * p:LANES * (p + 1)]
            gate = ret[:, 768 + LANES * p:768 + LANES * (p + 1)]
            reto_ref[:, LANES * p:LANES * (p + 1)] = (rn * _silu(gate)).astype(BF16)
        y = (tot[:, 0:256] + dl_ref[...] * sx) * _silu(ssd[:, 512:768])
        ms = jnp.mean(y * y, axis=-1, keepdims=True)
        ssdo_ref[...] = (y * lax.rsqrt(ms + 1e-6) * snw_ref[...]).astype(BF16)

    if emit_state:
        @pl.when(s == n - 1)
        def _():
            sts_ref[0, 0] = st_ref[0:2]
            str_ref[0, 0] = st_ref[2:4]

        @pl.when(s == 2 * n - 1)
        def _():
            sts_ref[0, 1] = st_ref[0:2]
            str_ref[0, 1] = st_ref[2:4]


def _scan(ret, ssd, pw, layer, init, n_seq, seq_len, chunk, emit_state):
    t = ret.shape[0]
    n = seq_len // chunk
    rows8 = t // SUBLANES

    def cidx(s):
        return jnp.where(s >= n, 2 * n - 1 - s, s)

    def oidx(s):
        return jnp.where(s >= n, 2 * n - 1 - s, n - 1)

    par = lambda w: pl.BlockSpec((None,) + w, lambda b, s: (layer,) + (0,) * len(w))
    in_specs = [pl.BlockSpec((chunk, RET_W), lambda b, s: (b * n + cidx(s), 0)),
                pl.BlockSpec((chunk, SSD_W), lambda b, s: (b * n + cidx(s), 0)),
                pl.BlockSpec((SUBLANES, CONV_CH),
                             lambda b, s: (jnp.maximum((b * seq_len + cidx(s) * chunk) // SUBLANES - 1, 0), 0)),
                pl.BlockSpec((SUBLANES, CONV_CH),
                             lambda b, s: (jnp.minimum((b * seq_len + (cidx(s) + 1) * chunk) // SUBLANES, rows8 - 1), 0)),
                par((3, CONV_CH)), par((1, CONV_CH)), par((1, LANES)), par((1, LANES)), par((1, LANES)),
                par((1, RET_WIDTH)), par((1, SSD_WIDTH)), par((1, SSD_WIDTH)),
                pl.BlockSpec((LANES, LANES), lambda b, s: (0, 0))]
    args = [ret, ssd, ssd, ssd, pw["conv_w"], pw["conv_b"], pw["dt_bias"], pw["a_log"], pw["ret_ld"],
            pw["ret_nw"], pw["d_lane"], pw["ssd_nw"], pw["g128"]]
    st_spec = pl.BlockSpec((1, 2, 2, LANES, LANES), lambda b, s: (b, 0, 0, 0, 0))
    if init is not None:
        in_specs += [st_spec, st_spec]
        args += list(init)
    out_specs = [pl.BlockSpec((chunk, RET_WIDTH), lambda b, s: (b * n + oidx(s), 0)),
                 pl.BlockSpec((chunk, SSD_WIDTH), lambda b, s: (b * n + oidx(s), 0))]
    out_shape = [jax.ShapeDtypeStruct((t, RET_WIDTH), BF16), jax.ShapeDtypeStruct((t, SSD_WIDTH), BF16)]
    if emit_state:
        out_specs += [st_spec, st_spec]
        out_shape += [jax.ShapeDtypeStruct((n_seq, 2, 2, LANES, LANES), F32)] * 2
    return pl.pallas_call(
        functools.partial(_scan_kernel, n=n, chunk=chunk, has_init=init is not None, emit_state=emit_state),
        grid=(n_seq, 2 * n), in_specs=in_specs, out_specs=out_specs, out_shape=out_shape,
        scratch_shapes=[pltpu.VMEM((seq_len, 512), F32), pltpu.VMEM((4, LANES, LANES), F32)],
        compiler_params=_cparams(2, 48), name="scan",
    )(*args)


def _proj_out_kernel(x_ref, att_ref, ro_ref, so_ref, mod_ref, wo_ref, lnw_ref, lnb_ref, rwh_ref, rwl_ref,
                     x1_ref, u2_ref, lg_ref):
    mix = (_dot(att_ref[...], wo_ref[0:512, :]) + _dot(ro_ref[...], wo_ref[512:768, :])
           + _dot(so_ref[...], wo_ref[768:1024, :]))
    h = DEEPNORM_ALPHA * x_ref[...] + mod_ref[0, 2:3, :] * mix
    x1 = _layer_norm(h, lnw_ref[...], lnb_ref[...])
    x1_ref[...] = x1
    u2 = x1 * (1.0 + mod_ref[0, 4:5, :]) + mod_ref[0, 3:4, :]
    hi = u2.astype(BF16)
    u2_ref[...] = hi
    lo = (u2 - hi.astype(F32)).astype(BF16)
    rwh = rwh_ref[...]
    lg_ref[...] = _dot_nt(rwh, hi) + _dot_nt(rwh, lo) + _dot_nt(rwl_ref[...], hi)


def _proj_out(x, att, ro, so, modp, w_out_p, ln_w, ln_b, rwh, rwl, layer, seq_len, tt):
    t = x.shape[0]
    tiles_per_seq = seq_len // tt
    row = lambda i: (i, 0)
    lsel = lambda *w: pl.BlockSpec((None,) + w, lambda i: (layer,) + (0,) * len(w))
    in_specs = [pl.BlockSpec((tt, D_MODEL), row), pl.BlockSpec((tt, 512), row),
                pl.BlockSpec((tt, RET_WIDTH), row), pl.BlockSpec((tt, SSD_WIDTH), row),
                _mod_spec(modp, tiles_per_seq),
                lsel(D_MODEL, D_MODEL),
                pl.BlockSpec((None, None, 1, D_MODEL), lambda i: (layer, 0, 0, 0)),
                pl.BlockSpec((None, None, 1, D_MODEL), lambda i: (layer, 0, 0, 0)),
                lsel(N_EXPERTS, D_MODEL), lsel(N_EXPERTS, D_MODEL)]
    out_shape = [jax.ShapeDtypeStruct((t, D_MODEL), F32), jax.ShapeDtypeStruct((t, D_MODEL), BF16),
                 jax.ShapeDtypeStruct((N_EXPERTS, t), F32)]
    out_specs = [pl.BlockSpec((tt, D_MODEL), row), pl.BlockSpec((tt, D_MODEL), row),
                 pl.BlockSpec((N_EXPERTS, tt), lambda i: (0, i))]
    return pl.pallas_call(
        _proj_out_kernel, grid=(t // tt,), in_specs=in_specs, out_specs=out_specs, out_shape=out_shape,
        compiler_params=_cparams(1, 48), name="proj_out",
    )(x, att, ro, so, modp, w_out_p, ln_w, ln_b, rwh, rwl)


def _route_kernel(lg_ref, bias_ref, idx_ref, rank_ref, wc_ref, cnt_ref, carry_ref):
    i = pl.program_id(0)

    @pl.when(i == 0)
    def _():
        carry_ref[...] = jnp.zeros(carry_ref.shape, F32)

    scores = jax.nn.sigmoid(lg_ref[...])
    sel = scores + bias_ref[...]
    n_e, tr = sel.shape
    neg = -jnp.inf
    sub = lax.broadcasted_iota(I32, (GROUP_SIZE, tr), 0).astype(F32)
    blocks = [sel[GROUP_SIZE * g:GROUP_SIZE * (g + 1), :] for g in range(ROUTER_GROUPS)]
    gscore = []
    for blk in blocks:
        m1 = blk.max(axis=0, keepdims=True)
        pos1 = jnp.min(jnp.where(blk == m1, sub, float(GROUP_SIZE)), axis=0, keepdims=True)
        m2 = jnp.max(jnp.where(sub == pos1, neg, blk), axis=0, keepdims=True)
        gscore.append(m1 + m2)
    masked = []
    for g in range(ROUTER_GROUPS):
        beat = jnp.zeros_like(gscore[g])
        for h in range(ROUTER_GROUPS):
            if h == g:
                continue
            wins = (gscore[h] >= gscore[g]) if h < g else (gscore[h] > gscore[g])
            beat = beat + wins.astype(F32)
        masked.append(jnp.where(beat < float(ROUTER_TOPK_GROUPS), blocks[g], neg))
    cur = jnp.concatenate(masked, axis=0)
    eio = lax.broadcasted_iota(I32, (n_e, tr), 0).astype(F32)
    chosen = jnp.zeros((n_e, tr), F32)
    picks = []
    for _ in range(TOP_K):
        m = cur.max(axis=0, keepdims=True)
        ik = jnp.min(jnp.where(cur == m, eio, float(n_e)), axis=0, keepdims=True)
        hit = eio == ik
        picks.append((ik, hit))
        cur = jnp.where(hit, neg, cur)
        chosen = jnp.where(hit, 1.0, chosen)
    picked = chosen * scores
    wc_ref[:, 0, :] = picked / jnp.sum(picked, axis=0, keepdims=True) * ROUTED_SCALE
    r_ = lax.broadcasted_iota(I32, (tr, tr), 0)
    c_ = lax.broadcasted_iota(I32, (tr, tr), 1)
    chosen_b = chosen.astype(BF16)
    before = _dot(chosen_b, (r_ < c_).astype(BF16)) + carry_ref[:, 0:1]
    for k, (ik, hit) in enumerate(picks):
        idx_ref[k:k + 1, :] = ik.astype(I32)
        rank_ref[k:k + 1, :] = jnp.sum(jnp.where(hit, before, 0.0), axis=0, keepdims=True).astype(I32)
    carry_ref[...] = carry_ref[...] + _dot(chosen_b, jnp.ones((tr, LANES), BF16))
    cnt_ref[...] = carry_ref[...]


def _route(lg_t, bias, layer, tr):
    t = lg_t.shape[1]
    col = lambda i: (0, i)
    return pl.pallas_call(
        _route_kernel, grid=(t // tr,),
        in_specs=[pl.BlockSpec((N_EXPERTS, tr), col),
                  pl.BlockSpec((None, N_EXPERTS, 1), lambda i: (layer, 0, 0))],
        out_specs=[pl.BlockSpec((TOP_K, tr), col), pl.BlockSpec((TOP_K, tr), col),
                   pl.BlockSpec((N_EXPERTS, 1, tr), lambda i: (0, 0, i)),
                   pl.BlockSpec((N_EXPERTS, LANES), lambda i: (0, 0))],
        out_shape=[jax.ShapeDtypeStruct((TOP_K, t), I32), jax.ShapeDtypeStruct((TOP_K, t), I32),
                   jax.ShapeDtypeStruct((N_EXPERTS, 1, t), F32), jax.ShapeDtypeStruct((N_EXPERTS, LANES), F32)],
        scratch_shapes=[pltpu.VMEM((N_EXPERTS, LANES), F32)],
        compiler_params=_cparams(1, 48), name="route",
    )(lg_t, bias)


def _slots_kernel(start_ref, idx_ref, rank_ref, slot_ref):
    n_e = start_ref.shape[0]
    tp = idx_ref.shape[1]
    eio = lax.broadcasted_iota(I32, (n_e, tp), 0)
    start = start_ref[...]
    for k in range(TOP_K):
        first = jnp.sum(jnp.where(eio == idx_ref[k:k + 1, :], start, 0.0), axis=0, keepdims=True)
        slot_ref[k:k + 1, :] = first.astype(I32) + rank_ref[k:k + 1, :]


def _plan_kernel(slot_ref, tok_ref, *, tp):
    base = pl.program_id(0) * tp

    def body(j, carry):
        for k in range(TOP_K):
            tok_ref[slot_ref[k, j]] = base + j
        return carry

    lax.fori_loop(0, tp, body, 0)


def _plan(start, idx_t, rank_t, tp):
    t = idx_t.shape[1]
    col = lambda i: (0, i)
    slots = pl.pallas_call(
        _slots_kernel, grid=(t // tp,),
        in_specs=[pl.BlockSpec((N_EXPERTS, 1), lambda i: (0, 0)),
                  pl.BlockSpec((TOP_K, tp), col), pl.BlockSpec((TOP_K, tp), col)],
        out_specs=pl.BlockSpec((TOP_K, tp), col),
        out_shape=jax.ShapeDtypeStruct((TOP_K, t), I32),
        compiler_params=_cparams(1, 32), name="slots",
    )(start.astype(F32).reshape(N_EXPERTS, 1), idx_t, rank_t)
    return pl.pallas_call(
        functools.partial(_plan_kernel, tp=tp), grid=(t // tp,),
        in_specs=[pl.BlockSpec((TOP_K, tp), col, memory_space=pltpu.SMEM)],
        out_specs=pl.BlockSpec(memory_space=pltpu.SMEM),
        out_shape=jax.ShapeDtypeStruct((t * TOP_K,), I32),
        compiler_params=_cparams(1, 32), name="plan",
    )(slots)


def _moe_kernel(start_ref, cnt_ref, tok_ref, wc_ref, x_hbm, wg_ref, wu_ref, wd_ref, out_hbm,
                xv_ref, acc_ref, xt_ref, y_ref, trow_ref, sem, *, n_tok, rows, sub, batch):
    e = pl.program_id(0)
    n_chunks = OUT_CHUNKS
    pair_rows = 2 * n_chunks
    stride = rows + SUBLANES

    @pl.when(e == 0)
    def _():
        cp = pltpu.make_async_copy(x_hbm, xv_ref.at[pl.ds(0, n_tok * n_chunks), :], sem)
        cp.start()
        acc_ref[...] = jnp.zeros(acc_ref.shape, F32)
        xt_ref[...] = jnp.zeros(xt_ref.shape, F32)
        xv_ref[pl.ds(n_tok * n_chunks, pair_rows), :] = jnp.zeros((pair_rows, LANES), BF16)
        cp.wait()

    n = cnt_ref[e]
    first = start_ref[e]
    wg = wg_ref[...].astype(BF16)
    wu = wu_ref[...].astype(BF16)
    wd = wd_ref[...].astype(BF16)

    def block(b, carry):
        base = first + b * rows
        left = jnp.minimum(n - b * rows, rows)

        def gather_row(r, t):
            trow_ref[r] = t
            pair = xv_ref[pl.ds(pl.multiple_of((t >> 1) * pair_rows, pair_rows), pair_rows), :].astype(F32)
            xt_ref[pl.ds(r, n_chunks, stride=stride), :] = jnp.where(
                (t & 1) == 1, pair[n_chunks:pair_rows], pair[0:n_chunks])

        for s0 in range(0, rows, sub):
            @pl.when(s0 + sub <= left)
            def _():
                for r in range(s0, s0 + sub):
                    gather_row(r, tok_ref[base + r])

            @pl.when(jnp.logical_and(s0 < left, left < s0 + sub))
            def _():
                for r in range(s0, s0 + sub):
                    valid = r < left
                    gather_row(r, jnp.where(valid, tok_ref[jnp.where(valid, base + r, 0)], n_tok))

        xb = jnp.concatenate([xt_ref[pl.ds(c * stride, rows), :] for c in range(n_chunks)], axis=1).astype(BF16)
        h = (_silu(_dot(xb, wg)) * _dot(xb, wu)).astype(BF16)
        y = _dot(h, wd)
        for c in range(n_chunks):
            y_ref[pl.ds(c * stride, rows), :] = y[:, LANES * c:LANES * (c + 1)]
        for s0 in range(0, rows, sub):
            @pl.when(s0 < left)
            def _():
                for r0 in range(s0, s0 + sub, batch):
                    pending = []
                    for r in range(r0, r0 + batch):
                        t = trow_ref[r]
                        wt = wc_ref[0, 0, jnp.minimum(t, n_tok - 1)]
                        dst = pl.ds(pl.multiple_of(t * n_chunks, n_chunks), n_chunks)
                        pending.append((dst, acc_ref[dst, :] + wt * y_ref[pl.ds(r, n_chunks, stride=stride), :]))
                    for dst, val in pending:
                        acc_ref[dst, :] = val
        return carry

    lax.fori_loop(0, (n + rows - 1) // rows, block, 0)

    @pl.when(e == pl.num_programs(0) - 1)
    def _():
        cp = pltpu.make_async_copy(acc_ref.at[pl.ds(0, n_tok * n_chunks), :], out_hbm, sem)
        cp.start()
        cp.wait()


def _moe(start, cnt, tok_sorted, wc, x_rows, w_gate, w_up, w_down, layer, n_tok, rows, vmem_mib):
    n_chunks = OUT_CHUNKS
    wspec = lambda a, b: pl.BlockSpec((None, None, a, b), lambda e, st, ct: (layer, e, 0, 0))
    grid_spec = pltpu.PrefetchScalarGridSpec(
        num_scalar_prefetch=2, grid=(N_EXPERTS,),
        in_specs=[pl.BlockSpec(memory_space=pltpu.SMEM),
                  pl.BlockSpec((1, 1, n_tok), lambda e, st, ct: (e, 0, 0), memory_space=pltpu.SMEM),
                  pl.BlockSpec(memory_space=pl.ANY),
                  wspec(D_MODEL, EXPERT_FF), wspec(D_MODEL, EXPERT_FF), wspec(EXPERT_FF, D_MODEL)],
        out_specs=pl.BlockSpec(memory_space=pl.ANY),
        scratch_shapes=[pltpu.VMEM(((n_tok + 2) * n_chunks, LANES), BF16),
                        pltpu.VMEM(((n_tok + 1) * n_chunks, LANES), F32),
                        pltpu.VMEM((n_chunks * (rows + SUBLANES), LANES), F32),
                        pltpu.VMEM((n_chunks * (rows + SUBLANES), LANES), F32),
                        pltpu.SMEM((rows,), I32),
                        pltpu.SemaphoreType.DMA(())])
    return pl.pallas_call(
        functools.partial(_moe_kernel, n_tok=n_tok, rows=rows, sub=64, batch=8),
        grid_spec=grid_spec, out_shape=jax.ShapeDtypeStruct((n_tok * n_chunks, LANES), F32),
        compiler_params=_cparams(1, vmem_mib), name="moe",
    )(start, cnt, tok_sorted, wc, x_rows, w_gate, w_up, w_down)


def _ffn_out_kernel(x1_ref, u2_ref, routed_ref, mod_ref, wgu_ref, wd_ref, lnw_ref, lnb_ref, o_ref):
    gu = _dot(u2_ref[...], wgu_ref[...])
    shared = _dot((_silu(gu[:, 0:SHARED_FF]) * gu[:, SHARED_FF:]).astype(BF16), wd_ref[...])
    tt = x1_ref.shape[0]
    routed = jnp.concatenate([routed_ref[pl.ds(c, tt, stride=OUT_CHUNKS), :] for c in range(OUT_CHUNKS)], axis=1)
    h = DEEPNORM_ALPHA * x1_ref[...] + mod_ref[0, 5:6, :] * (routed + shared)
    o_ref[...] = _layer_norm(h, lnw_ref[...], lnb_ref[...])


def _ffn_out(x1, u2, routed, modp, sh_gu, sh_d, ln_w, ln_b, layer, seq_len, tt):
    t = x1.shape[0]
    tiles_per_seq = seq_len // tt
    row = lambda i: (i, 0)
    lsel = lambda *w: pl.BlockSpec((None,) + w, lambda i: (layer,) + (0,) * len(w))
    ln_spec = pl.BlockSpec((None, None, 1, D_MODEL), lambda i: (layer, 1, 0, 0))
    return pl.pallas_call(
        _ffn_out_kernel, grid=(t // tt,),
        in_specs=[pl.BlockSpec((tt, D_MODEL), row), pl.BlockSpec((tt, D_MODEL), row),
                  pl.BlockSpec((tt * OUT_CHUNKS, LANES), row),
                  _mod_spec(modp, tiles_per_seq),
                  lsel(D_MODEL, 2 * SHARED_FF), lsel(SHARED_FF, D_MODEL), ln_spec, ln_spec],
        out_specs=pl.BlockSpec((tt, D_MODEL), row),
        out_shape=jax.ShapeDtypeStruct((t, D_MODEL), F32),
        compiler_params=_cparams(1, 48), name="ffn_out",
    )(x1, u2, routed, modp, sh_gu, sh_d, ln_w, ln_b)


def _q_perm():
    cols = []
    for j in range(ATT_HEADS // 2):
        for half in range(2):
            h = j + (ATT_HEADS // 2) * half
            cols.extend(range(h * HEAD_DIM, (h + 1) * HEAD_DIM))
    return np.asarray(cols, np.int32)


def _prepare_params(p):
    off = np.concatenate([[0], np.cumsum(IN_SIZES)])
    seg = lambda k: np.arange(off[k], off[k + 1], dtype=np.int32)
    qp = _q_perm()
    cols = np.concatenate([qp, seg(1), seg(2), seg(3), seg(4), seg(5), seg(6), seg(7), seg(9), seg(10), seg(8), seg(11)])
    w_in = jnp.take(p["w_in"], jnp.asarray(cols), axis=2)
    w_in = jnp.pad(w_in, ((0, 0), (0, 0), (0, IN_W_PAD - w_in.shape[2]))).astype(BF16)
    rows = np.concatenate([qp, np.arange(ATT_WIDTH, D_MODEL, dtype=np.int32)])
    w_out = jnp.take(p["w_out"], jnp.asarray(rows), axis=1).astype(BF16)
    rwt = jnp.swapaxes(p["router_w"], 1, 2)
    rwh = rwt.astype(BF16)
    rwl = (rwt - rwh.astype(F32)).astype(BF16)

    def lane_pad(a, fill=0.0):
        a = a.reshape(DEPTH, 1, -1)
        return jnp.pad(a, ((0, 0), (0, 0), (0, LANES - a.shape[2])), constant_values=fill)

    gi = np.arange(512) // HEAD_DIM
    g512 = jnp.asarray(gi[:, None] == gi[None, :], BF16)
    return {
        "w_in": w_in, "w_out": w_out, "rwh": rwh, "rwl": rwl,
        "qnw": jnp.tile(p["q_norm_w"], (1, ATT_HEADS)).reshape(DEPTH, 1, 512),
        "knw": jnp.tile(p["k_norm_w"], (1, ATT_KV_HEADS)).reshape(DEPTH, 1, LANES),
        "g512": g512, "g128": g512[0:LANES, 0:LANES],
        "conv_w": p["ssd_conv_w"], "conv_b": p["ssd_conv_b"].reshape(DEPTH, 1, CONV_CH),
        "dt_bias": lane_pad(p["ssd_dt_bias"]), "a_log": lane_pad(p["ssd_a_log"]),
        "ret_ld": lane_pad(p["ret_log_decay"], -1.0),
        "ret_nw": p["ret_norm_w"].reshape(DEPTH, 1, RET_WIDTH),
        "d_lane": jnp.repeat(p["ssd_d"], HEAD_DIM, axis=1).reshape(DEPTH, 1, SSD_WIDTH),
        "ssd_nw": p["ssd_norm_w"].reshape(DEPTH, 1, SSD_WIDTH),
        "ln_w": p["ln_w"].reshape(DEPTH, 2, 1, D_MODEL), "ln_b": p["ln_b"].reshape(DEPTH, 2, 1, D_MODEL),
        "router_bias": p["router_bias"].reshape(DEPTH, N_EXPERTS, 1),
        "sh_gu": jnp.concatenate([p["sh_w_gate"], p["sh_w_up"]], axis=2).astype(BF16),
        "sh_d": p["sh_w_down"].astype(BF16),
        "exp_w_gate": p["exp_w_gate"], "exp_w_up": p["exp_w_up"], "exp_w_down": p["exp_w_down"],
    }


def _rope_tables(n_tokens):
    t = jnp.arange(n_tokens, dtype=I32)
    rows = (t // GRID_W).astype(F32)
    cols = (t % GRID_W).astype(F32)
    inv = ROPE_THETA ** (-jnp.arange(ROPE_AXIS_FREQS, dtype=F32) / ROPE_AXIS_FREQS)
    ar = rows[:, None] * inv
    ac = cols[:, None] * inv
    cos = jnp.concatenate([jnp.cos(ar), jnp.cos(ar), jnp.cos(ac), jnp.cos(ac)], axis=1)
    sin = jnp.concatenate([-jnp.sin(ar), jnp.sin(ar), -jnp.sin(ac), jnp.sin(ac)], axis=1)
    return jnp.tile(cos, (1, 2)), jnp.tile(sin, (1, 2))


def _pair_states(st, kind):
    b = st.shape[0]
    out = jnp.zeros((b, 2, 2, LANES, LANES), F32)
    for pr in range(2):
        for tt in range(2):
            r0 = HEAD_DIM * (tt if kind == "ret" else pr)
            out = out.at[:, :, pr, r0:r0 + HEAD_DIM, HEAD_DIM * tt:HEAD_DIM * (tt + 1)].set(st[:, :, 2 * pr + tt])
    return out


def _unpair_states(m, kind):
    heads = []
    for pr in range(2):
        for tt in range(2):
            r0 = HEAD_DIM * (tt if kind == "ret" else pr)
            heads.append(m[:, :, pr, r0:r0 + HEAD_DIM, HEAD_DIM * tt:HEAD_DIM * (tt + 1)])
    return jnp.stack(heads, axis=2)


def _trunk_layer(x, modp, pw, layer, n_seq, seq_len, rope_tabs, cache, init, cfg):
    n_tok = x.shape[0]
    ctx = cache is None
    outs = _proj_in(x, modp, pw["w_in"], pw["qnw"], pw["knw"], pw["g512"], rope_tabs, layer, seq_len,
                    cfg["tt"], emit_cache=ctx)
    q, k, v, ret, ssd = outs[:5]
    att = _attention(q, k, v, cache, layer, n_seq, seq_len, cfg["tq"])
    sc = _scan(ret, ssd, pw, layer, init, n_seq, seq_len, cfg["chunk"], emit_state=ctx)
    x1, u2, lg_t = _proj_out(x, att, sc[0], sc[1], modp, pw["w_out"], pw["ln_w"], pw["ln_b"],
                                  pw["rwh"], pw["rwl"], layer, seq_len, cfg["tt"])
    idx_t, rank_t, wc, counts = _route(lg_t, pw["router_bias"], layer, cfg["tr"])
    cnt = counts[:, 0].astype(I32)
    start = jnp.cumsum(cnt) - cnt
    tok_sorted = _plan(start, idx_t, rank_t, cfg["tr"])
    u2_rows = u2.reshape(n_tok * OUT_CHUNKS, LANES)
    routed = _moe(start, cnt, tok_sorted, wc, u2_rows, pw["exp_w_gate"], pw["exp_w_up"], pw["exp_w_down"],
                  layer, n_tok, cfg["rows"], cfg["moe_vmem"])
    x_new = _ffn_out(x1, u2, routed, modp, pw["sh_gu"], pw["sh_d"], pw["ln_w"], pw["ln_b"], layer, seq_len, cfg["tt"])
    new_ctx = (outs[5], outs[6], sc[2], sc[3]) if ctx else None
    return x_new, new_ctx


def kernel(x_prompt, x_sample, cache_k, cache_v, state_ret, state_ssd, c, c_ctx,
           w_mod, b_mod, w_in, q_norm_w, k_norm_w, ret_log_decay, ret_norm_w,
           ssd_conv_w, ssd_conv_b, ssd_dt_bias, ssd_a_log, ssd_d, ssd_norm_w, w_out,
           ln_w, ln_b, router_w, router_bias, exp_w_gate, exp_w_up, exp_w_down,
           sh_w_gate, sh_w_up, sh_w_down):
    pw = _prepare_params({
        "w_in": w_in, "q_norm_w": q_norm_w, "k_norm_w": k_norm_w, "ret_log_decay": ret_log_decay,
        "ret_norm_w": ret_norm_w, "ssd_conv_w": ssd_conv_w, "ssd_conv_b": ssd_conv_b, "ssd_dt_bias": ssd_dt_bias,
        "ssd_a_log": ssd_a_log, "ssd_d": ssd_d, "ssd_norm_w": ssd_norm_w, "w_out": w_out, "ln_w": ln_w, "ln_b": ln_b,
        "router_w": router_w, "router_bias": router_bias, "exp_w_gate": exp_w_gate, "exp_w_up": exp_w_up,
        "exp_w_down": exp_w_down, "sh_w_gate": sh_w_gate, "sh_w_up": sh_w_up, "sh_w_down": sh_w_down})
    nb, seq, d = x_prompt.shape
    nd, dseq, _ = x_sample.shape
    cond = jnp.concatenate([c_ctx[None], c, jnp.zeros((8 - 1 - nd, d), F32)], axis=0)
    mod = _modulation(cond, w_mod, b_mod)
    rope_tabs = _rope_tables(dseq)
    past = cache_k.shape[2]
    ck = cache_k.reshape(nd, DEPTH, past, LANES)
    cv = cache_v.reshape(nd, DEPTH, past, LANES)
    cfg_ctx = {"tt": 512, "tq": seq, "chunk": SCAN_CHUNK, "tr": 512, "rows": MOE_ROWS, "moe_vmem": MOE_VMEM_MIB}
    cfg_lat = {"tt": 512, "tq": 512, "chunk": SCAN_CHUNK, "tr": 512, "rows": MOE_ROWS, "moe_vmem": MOE_VMEM_MIB}

    xc = x_prompt.reshape(nb * seq, d)
    xs = x_sample.reshape(nd * dseq, d)
    ks, vs, srs, sss = [], [], [], []
    for l in range(DEPTH):
        mod_ctx = mod[l, 0:1].reshape(1, 6, d)
        mod_lat = mod[l, 1:1 + nd].reshape(nd, 6, d)
        xc, (k_l, v_l, sr_l, ss_l) = _trunk_layer(xc, mod_ctx, pw, l, nb, seq, None, None, None, cfg_ctx)
        ks.append(k_l.reshape(nb, seq, ATT_KV_HEADS, HEAD_DIM))
        vs.append(v_l.reshape(nb, seq, ATT_KV_HEADS, HEAD_DIM))
        srs.append(_unpair_states(sr_l, "ret"))
        sss.append(_unpair_states(ss_l, "ssd"))
        init = (_pair_states(state_ret[:, l], "ret"), _pair_states(state_ssd[:, l], "ssd"))
        xs, _ = _trunk_layer(xs, mod_lat, pw, l, nd, dseq, rope_tabs, (ck, cv), init, cfg_lat)
    return (xc.reshape(nb, seq, d), xs.reshape(nd, dseq, d), jnp.stack(ks, axis=1), jnp.stack(vs, axis=1),
            jnp.stack(srs, axis=1), jnp.stack(sss, axis=1))
```

```python
import functools

import numpy as np
import jax
import jax.numpy as jnp
from jax import lax
from jax.experimental import pallas as pl
from jax.experimental.pallas import tpu as pltpu

F32 = jnp.float32
BF16 = jnp.bfloat16
I32 = jnp.int32
HIGHEST = lax.Precision.HIGHEST
NT_DIMS = (((1,), (1,)), ((), ()))

D_MODEL = 1024
DEPTH = 4
GRID_W = 64
HEAD_DIM = 64
ATT_HEADS = 8
ATT_KV_HEADS = 2
ATT_WIDTH = ATT_HEADS * HEAD_DIM
ROPE_THETA = 10000.0
ROPE_AXIS_FREQS = HEAD_DIM // 4
RET_HEADS = 4
RET_DK = 64
RET_WIDTH = 256
SSD_HEADS = 4
SSD_WIDTH = 256
SSD_STATE = 64
N_EXPERTS = 256
TOP_K = 8
ROUTER_GROUPS = 8
ROUTER_TOPK_GROUPS = 4
GROUP_SIZE = N_EXPERTS // ROUTER_GROUPS
EXPERT_FF = 256
SHARED_FF = 256
ROUTED_SCALE = 2.5
DEEPNORM_ALPHA = (2 * DEPTH) ** 0.25
IN_SIZES = (512, 128, 128, 256, 256, 256, 256, 256, 256, 128, 128, 8)

LANES = 128
SUBLANES = 8
MIB = 1024 * 1024

QKV_W = 768
RET_OFF, RET_W = 768, 1024
SSD_OFF, SSD_W = 1792, 896
IN_W_PAD = 2688
CONV_CH = 512

OUT_CHUNKS = D_MODEL // LANES
SCAN_CHUNK = 256
MOE_ROWS = 320
MOE_VMEM_MIB = 62


def _cparams(n_grid, vmem_mib):
    return pltpu.CompilerParams(dimension_semantics=("arbitrary",) * n_grid,
                                vmem_limit_bytes=vmem_mib * MIB)


def _dot(a, b):
    return jnp.dot(a, b, preferred_element_type=F32)


def _dot_nt(a, b):
    return lax.dot_general(a, b, NT_DIMS, preferred_element_type=F32)


def _silu(x):
    return x * jax.nn.sigmoid(x)


def _softplus(x):
    return jnp.maximum(x, 0.0) + jnp.log1p(jnp.exp(-jnp.abs(x)))


def _mod_spec(modp, tiles_per_seq):
    if modp.shape[0] == 1:
        return pl.BlockSpec((1, 6, D_MODEL), lambda i: (0, 0, 0))
    return pl.BlockSpec((1, 6, D_MODEL), lambda i: (i // tiles_per_seq, 0, 0))


def _group_sum(x, g):
    hi = x.astype(BF16)
    lo = (x - hi.astype(F32)).astype(BF16)
    return _dot(hi, g) + _dot(lo, g)


def _swap16(x):
    n = x.shape[1]
    lane = lax.broadcasted_iota(I32, x.shape, 1)
    return jnp.where((lane & 16) == 0, pltpu.roll(x, n - 16, 1), pltpu.roll(x, 16, 1))


def _layer_norm(h, w, b):
    mu = jnp.mean(h, axis=-1, keepdims=True)
    d = h - mu
    var = jnp.mean(d * d, axis=-1, keepdims=True)
    return d * lax.rsqrt(var + 1e-5) * w + b


def _mod_kernel(cond_ref, w_ref, b_ref, o_ref):
    c = cond_ref[...]
    o_ref[0] = jnp.dot(_silu(c), w_ref[0], precision=HIGHEST, preferred_element_type=F32) + b_ref[0]


def _modulation(cond8, w_mod, b_mod):
    tn = 1536
    n_col = w_mod.shape[2] // tn
    return pl.pallas_call(
        _mod_kernel,
        grid=(DEPTH, n_col),
        in_specs=[pl.BlockSpec((8, D_MODEL), lambda l, j: (0, 0)),
                  pl.BlockSpec((1, D_MODEL, tn), lambda l, j: (l, 0, j)),
                  pl.BlockSpec((1, 1, tn), lambda l, j: (l, 0, j))],
        out_specs=pl.BlockSpec((1, 8, tn), lambda l, j: (l, 0, j)),
        out_shape=jax.ShapeDtypeStruct((DEPTH, 8, w_mod.shape[2]), F32),
        compiler_params=_cparams(2, 32),
        name="modulation",
    )(cond8, w_mod, b_mod.reshape(DEPTH, 1, -1))


def _proj_in_kernel(*refs, rope, emit_cache):
    x_ref, mod_ref, w_ref, qnw_ref, knw_ref, g_ref = refs[:6]
    pos = 6
    if rope:
        cos_ref, sin_ref = refs[pos:pos + 2]
        pos += 2
    q_ref, k_ref, v_ref, ret_ref, ssd_ref = refs[pos:pos + 5]
    pos += 5
    u = x_ref[...] * (1.0 + mod_ref[0, 1:2, :]) + mod_ref[0, 0:1, :]
    ub = u.astype(BF16)
    q = _dot(ub, w_ref[:, 0:512])
    k = _dot(ub, w_ref[:, 512:640])
    v = _dot(ub, w_ref[:, 640:768])
    g = g_ref[...]
    inv_hd = 1.0 / HEAD_DIM
    qn = q * lax.rsqrt(_group_sum(q * q, g) * inv_hd + 1e-6) * qnw_ref[...]
    kn = k * lax.rsqrt(_group_sum(k * k, g[0:LANES, 0:LANES]) * inv_hd + 1e-6) * knw_ref[...]
    if emit_cache:
        kf_ref, vf_ref = refs[pos:pos + 2]
        kf_ref[...] = kn
        vf_ref[...] = v
    if rope:
        c = cos_ref[...]
        s = sin_ref[...]
        kn = kn * c + _swap16(kn) * s
        qn = qn * jnp.concatenate([c] * 4, axis=1) + _swap16(qn) * jnp.concatenate([s] * 4, axis=1)
    q_ref[...] = (qn * (HEAD_DIM ** -0.5)).astype(BF16)
    k_ref[...] = kn.astype(BF16)
    v_ref[...] = v.astype(BF16)
    ret_ref[...] = _dot(ub, w_ref[:, RET_OFF:RET_OFF + RET_W])
    ssd_ref[...] = _dot(ub, w_ref[:, SSD_OFF:SSD_OFF + SSD_W])


def _proj_in(x, modp, w_in_p, qnw, knw, g512, rope_tabs, layer, seq_len, tt, emit_cache):
    t = x.shape[0]
    tiles_per_seq = seq_len // tt
    rope = rope_tabs is not None
    row = lambda i: (i, 0)
    const2 = lambda i: (0, 0)
    in_specs = [pl.BlockSpec((tt, D_MODEL), row),
                _mod_spec(modp, tiles_per_seq),
                pl.BlockSpec((None, D_MODEL, IN_W_PAD), lambda i: (layer, 0, 0)),
                pl.BlockSpec((None, 1, 512), lambda i: (layer, 0, 0)),
                pl.BlockSpec((None, 1, LANES), lambda i: (layer, 0, 0)),
                pl.BlockSpec((512, 512), const2)]
    args = [x, modp, w_in_p, qnw, knw, g512]
    if rope:
        tab = pl.BlockSpec((tt, LANES), lambda i: (i % tiles_per_seq, 0))
        in_specs += [tab, tab]
        args += list(rope_tabs)
    out_shape = [jax.ShapeDtypeStruct((t, 512), BF16), jax.ShapeDtypeStruct((t, LANES), BF16),
                 jax.ShapeDtypeStruct((t, LANES), BF16), jax.ShapeDtypeStruct((t, RET_W), F32),
                 jax.ShapeDtypeStruct((t, SSD_W), F32)]
    out_specs = [pl.BlockSpec((tt, 512), row), pl.BlockSpec((tt, LANES), row), pl.BlockSpec((tt, LANES), row),
                 pl.BlockSpec((tt, RET_W), row), pl.BlockSpec((tt, SSD_W), row)]
    if emit_cache:
        out_shape += [jax.ShapeDtypeStruct((t, LANES), F32)] * 2
        out_specs += [pl.BlockSpec((tt, LANES), row)] * 2
    return pl.pallas_call(
        functools.partial(_proj_in_kernel, rope=rope, emit_cache=emit_cache),
        grid=(t // tt,), in_specs=in_specs, out_specs=out_specs, out_shape=out_shape,
        compiler_params=_cparams(1, 48), name="proj_in",
    )(*args)


def _attn_kernel(*refs, with_cache, group):
    q_ref, k_ref, v_ref = refs[:3]
    n_in = 5 if with_cache else 3
    o_ref = refs[n_in]
    s_refs = refs[n_in + 1:n_in + 3]
    p_refs = refs[n_in + 3:n_in + 5]
    tq = q_ref.shape[0]
    lane = lax.broadcasted_iota(I32, (1, LANES), 1)
    first = lane < HEAD_DIM
    parts = [(k_ref[...], v_ref[...])]
    if with_cache:
        parts.append((refs[3][...].astype(BF16), refs[4][...].astype(BF16)))
    bounds = np.cumsum([0] + [kk.shape[0] for kk, _ in parts])
    zero = jnp.zeros((), BF16)
    kmask = [[jnp.where(first, kk, zero), jnp.where(first, zero, kk)] for kk, _ in parts]
    one = jnp.ones((), BF16)
    vaug = [[jnp.where(first, vv, one), jnp.where(first, one, vv)] for _, vv in parts]
    head = 0
    for j in range(ATT_HEADS // 2):
        qb = q_ref[:, LANES * j:LANES * (j + 1)]
        outs = []
        for g in range(ATT_KV_HEADS):
            s_ref, p_ref = s_refs[head % 2], p_refs[head % 2]
            head += 1
            for p, km in enumerate(kmask):
                s_ref[:, bounds[p]:bounds[p + 1]] = _dot_nt(qb, km[g])
            for r0 in range(0, tq, group):
                s = s_ref[r0:r0 + group, :]
                p_ref[r0:r0 + group, :] = jnp.exp(s - s.max(axis=-1, keepdims=True)).astype(BF16)
            acc = None
            for p, va in enumerate(vaug):
                o = _dot(p_ref[:, bounds[p]:bounds[p + 1]], va[g])
                acc = o if acc is None else acc + o
            outs.append(acc / pltpu.roll(acc, HEAD_DIM, 1))
        o_ref[:, LANES * j:LANES * (j + 1)] = jnp.where(first, outs[0], outs[1]).astype(BF16)


def _attention(q, k, v, cache, layer, n_seq, seq_len, tq):
    t = q.shape[0]
    nq = seq_len // tq
    in_specs = [pl.BlockSpec((tq, 512), lambda b, i: (b * nq + i, 0)),
                pl.BlockSpec((seq_len, LANES), lambda b, i: (b, 0)),
                pl.BlockSpec((seq_len, LANES), lambda b, i: (b, 0))]
    args = [q, k, v]
    n_keys = seq_len
    if cache is not None:
        past = cache[0].shape[2]
        n_keys += past
        cspec = pl.BlockSpec((None, None, past, LANES), lambda b, i: (b, layer, 0, 0))
        in_specs += [cspec, cspec]
        args += list(cache)
    return pl.pallas_call(
        functools.partial(_attn_kernel, with_cache=cache is not None, group=2 * SUBLANES),
        grid=(n_seq, nq), in_specs=in_specs,
        out_specs=pl.BlockSpec((tq, 512), lambda b, i: (b * nq + i, 0)),
        out_shape=jax.ShapeDtypeStruct((t, 512), BF16),
        scratch_shapes=[pltpu.VMEM((tq, n_keys), F32)] * 2 + [pltpu.VMEM((tq, n_keys), BF16)] * 2,
        compiler_params=_cparams(2, 56), name="attention",
    )(*args)


def _scan_kernel(*refs, n, chunk, has_init, emit_state):
    (ret_ref, ssd_ref, hp_ref, hn_ref, cw_ref, cb_ref, dtb_ref, alog_ref, rld_ref,
     rnw_ref, dl_ref, snw_ref, g_ref) = refs[:13]
    pos = 13
    if has_init:
        s0r_ref, s0s_ref = refs[pos:pos + 2]
        pos += 2
    reto_ref, ssdo_ref = refs[pos:pos + 2]
    pos += 2
    if emit_state:
        str_ref, sts_ref = refs[pos:pos + 2]
        pos += 2
    of_ref, st_ref = refs[pos:pos + 2]

    C = chunk
    s = pl.program_id(1)
    bwd = s >= n
    c = jnp.where(bwd, 2 * n - 1 - s, s)

    def init_state(direction):
        if has_init:
            st_ref[0:2] = s0s_ref[0, direction]
            st_ref[2:4] = s0r_ref[0, direction]
        else:
            st_ref[...] = jnp.zeros(st_ref.shape, F32)

    @pl.when(s == 0)
    def _():
        init_state(0)

    @pl.when(s == n)
    def _():
        init_state(1)

    lane = lax.broadcasted_iota(I32, (1, LANES), 1)
    first = lane < HEAD_DIM
    ret = ret_ref[...]
    ssd = ssd_ref[...]

    xin = ssd[:, 0:CONV_CH]
    rowi = lax.broadcasted_iota(I32, (C, 1), 0)
    prev_row = jnp.where(c == 0, 0.0, hp_ref[SUBLANES - 1:SUBLANES, :])
    next_row = jnp.where(c == n - 1, 0.0, hn_ref[0:1, :])
    xp = jnp.where(rowi == 0, prev_row, pltpu.roll(xin, 1, 0))
    xn = jnp.where(rowi == C - 1, next_row, pltpu.roll(xin, C - 1, 0))
    xbc = _silu(xp * cw_ref[0:1, :] + xin * cw_ref[1:2, :] + xn * cw_ref[2:3, :] + cb_ref[...])
    sx = xbc[:, 0:256]
    sb = xbc[:, 256:384]
    sc = xbc[:, 384:512]

    def this_dir(a):
        return jnp.where(bwd, pltpu.roll(a, LANES - SSD_HEADS, 1), a)

    dt_all = _softplus(ssd[:, 768:896] + dtb_ref[...])
    la_ssd = this_dir(dt_all * (-jnp.exp(alog_ref[...])))
    dt_dir = this_dir(dt_all)
    lg = jnp.broadcast_to(jnp.log1p(-jnp.exp(rld_ref[...])), (SUBLANES, LANES))
    lg_dir = pltpu.roll(this_dir(lg), SSD_HEADS, 1)[0:1, :]
    la = jnp.where(lane < 4, la_ssd, jnp.where(lane < 8, lg_dir, 0.0))
    ii = lax.broadcasted_iota(I32, (C, C), 0)
    jj = lax.broadcasted_iota(I32, (C, C), 1)
    msk = jnp.where(bwd, jj - ii, ii - jj) >= 0
    cum = jnp.dot(msk.astype(F32), la, precision=HIGHEST, preferred_element_type=F32)
    last = jnp.where(bwd, cum[0:1, :], cum[C - 1:C, :])
    acol = jnp.where(lane < 8, cum, jnp.where(lane < 12, pltpu.roll(dt_dir, 8, 1), 0.0))
    arow = acol.T

    def decay(h, scale):
        diff = cum[:, h:h + 1] - arow[h:h + 1, :]
        return jnp.where(msk, jnp.exp(jnp.where(msk, diff, 0.0)), 0.0) * scale

    def lanes2(a0, a1):
        return jnp.where(first, a0, a1)

    ii2 = lax.broadcasted_iota(I32, (LANES, LANES), 0)
    jj2 = lax.broadcasted_iota(I32, (LANES, LANES), 1)
    blockdiag = ((ii2 < HEAD_DIM) == (jj2 < HEAD_DIM)).astype(F32)

    o_pairs = []
    for kind in ("ssd", "ret"):
        for p in range(2):
            slot = p if kind == "ssd" else 2 + p
            heads = (2 * p, 2 * p + 1)
            hl = [h if kind == "ssd" else 4 + h for h in heads]
            if kind == "ssd":
                grp = first if p == 0 else jnp.logical_not(first)
                qm = jnp.where(grp, sc, 0.0).astype(BF16)
                km = jnp.where(grp, sb, 0.0)
                v128 = sx[:, LANES * p:LANES * (p + 1)]
                sg = _dot_nt(qm, sb.astype(BF16))
                scores = [sg, sg]
                rs = [arow[8 + h:9 + h, :] for h in heads]
                cs = [acol[:, 8 + h:9 + h] for h in heads]
                q_inter = qm
            else:
                q128 = ret[:, LANES * p:LANES * (p + 1)]
                km = ret[:, 256 + LANES * p:256 + LANES * (p + 1)]
                v128 = ret[:, 512 + LANES * p:512 + LANES * (p + 1)]
                kb = km.astype(BF16)
                scores = [_dot_nt(jnp.where(first, q128, 0.0).astype(BF16), kb),
                          _dot_nt(jnp.where(first, 0.0, q128).astype(BF16), kb)]
                rs = [RET_DK ** -0.5] * 2
                cs = [RET_DK ** -0.5] * 2
                q_inter = q128.astype(BF16)
            vb = v128.astype(BF16)
            intra = [_dot((scores[t] * decay(hl[t], rs[t])).astype(BF16), vb) for t in range(2)]
            state = st_ref[slot]
            ecum = lanes2(jnp.exp(cum[:, hl[0]:hl[0] + 1]), jnp.exp(cum[:, hl[1]:hl[1] + 1]))
            o_pairs.append(lanes2(intra[0], intra[1]) + _dot(q_inter, state.astype(BF16)) * ecum)
            wv = lanes2(jnp.exp(last[:, hl[0]:hl[0] + 1] - cum[:, hl[0]:hl[0] + 1]) * cs[0],
                        jnp.exp(last[:, hl[1]:hl[1] + 1] - cum[:, hl[1]:hl[1] + 1]) * cs[1])
            snew = _dot(km.T.astype(BF16), (v128 * wv).astype(BF16))
            if kind == "ret":
                snew = snew * blockdiag
            elast = lanes2(jnp.exp(last[:, hl[0]:hl[0] + 1]), jnp.exp(last[:, hl[1]:hl[1] + 1]))
            st_ref[slot] = elast * state + snew
    o_all = jnp.concatenate(o_pairs, axis=1)
    rows = pl.ds(pl.multiple_of(c * C, C), C)

    @pl.when(jnp.logical_not(bwd))
    def _():
        of_ref[rows, :] = o_all

    @pl.when(bwd)
    def _():
        tot = of_ref[rows, :] + o_all
        g = g_ref[...]
        inv = 1.0 / HEAD_DIM
        for p in range(2):
            o = tot[:, 256 + LANES * p:256 + LANES * (p + 1)]
            d = o - _group_sum(o, g) * inv
            var = _group_sum(d * d, g) * inv
            rn = d * lax.rsqrt(var + 1e-5) * rnw_ref[:, LANES * p:LANES * (p + 1)]
            gate = ret[:, 768 + LANES * p:768 + LANES * (p + 1)]
            reto_ref[:, LANES * p:LANES * (p + 1)] = (rn * _silu(gate)).astype(BF16)
        y = (tot[:, 0:256] + dl_ref[...] * sx) * _silu(ssd[:, 512:768])
        ms = jnp.mean(y * y, axis=-1, keepdims=True)
        ssdo_ref[...] = (y * lax.rsqrt(ms + 1e-6) * snw_ref[...]).astype(BF16)

    if emit_state:
        @pl.when(s == n - 1)
        def _():
            sts_ref[0, 0] = st_ref[0:2]
            str_ref[0, 0] = st_ref[2:4]

        @pl.when(s == 2 * n - 1)
        def _():
            sts_ref[0, 1] = st_ref[0:2]
            str_ref[0, 1] = st_ref[2:4]


def _scan(ret, ssd, pw, layer, init, n_seq, seq_len, chunk, emit_state):
    t = ret.shape[0]
    n = seq_len // chunk
    rows8 = t // SUBLANES

    def cidx(s):
        return jnp.where(s >= n, 2 * n - 1 - s, s)

    def oidx(s):
        return jnp.where(s >= n, 2 * n - 1 - s, n - 1)

    par = lambda w: pl.BlockSpec((None,) + w, lambda b, s: (layer,) + (0,) * len(w))
    in_specs = [pl.BlockSpec((chunk, RET_W), lambda b, s: (b * n + cidx(s), 0)),
                pl.BlockSpec((chunk, SSD_W), lambda b, s: (b * n + cidx(s), 0)),
                pl.BlockSpec((SUBLANES, CONV_CH),
                             lambda b, s: (jnp.maximum((b * seq_len + cidx(s) * chunk) // SUBLANES - 1, 0), 0)),
                pl.BlockSpec((SUBLANES, CONV_CH),
                             lambda b, s: (jnp.minimum((b * seq_len + (cidx(s) + 1) * chunk) // SUBLANES, rows8 - 1), 0)),
                par((3, CONV_CH)), par((1, CONV_CH)), par((1, LANES)), par((1, LANES)), par((1, LANES)),
                par((1, RET_WIDTH)), par((1, SSD_WIDTH)), par((1, SSD_WIDTH)),
                pl.BlockSpec((LANES, LANES), lambda b, s: (0, 0))]
    args = [ret, ssd, ssd, ssd, pw["conv_w"], pw["conv_b"], pw["dt_bias"], pw["a_log"], pw["ret_ld"],
            pw["ret_nw"], pw["d_lane"], pw["ssd_nw"], pw["g128"]]
    st_spec = pl.BlockSpec((1, 2, 2, LANES, LANES), lambda b, s: (b, 0, 0, 0, 0))
    if init is not None:
        in_specs += [st_spec, st_spec]
        args += list(init)
    out_specs = [pl.BlockSpec((chunk, RET_WIDTH), lambda b, s: (b * n + oidx(s), 0)),
                 pl.BlockSpec((chunk, SSD_WIDTH), lambda b, s: (b * n + oidx(s), 0))]
    out_shape = [jax.ShapeDtypeStruct((t, RET_WIDTH), BF16), jax.ShapeDtypeStruct((t, SSD_WIDTH), BF16)]
    if emit_state:
        out_specs += [st_spec, st_spec]
        out_shape += [jax.ShapeDtypeStruct((n_seq, 2, 2, LANES, LANES), F32)] * 2
    return pl.pallas_call(
        functools.partial(_scan_kernel, n=n, chunk=chunk, has_init=init is not None, emit_state=emit_state),
        grid=(n_seq, 2 * n), in_specs=in_specs, out_specs=out_specs, out_shape=out_shape,
        scratch_shapes=[pltpu.VMEM((seq_len, 512), F32), pltpu.VMEM((4, LANES, LANES), F32)],
        compiler_params=_cparams(2, 48), name="scan",
    )(*args)


def _proj_out_kernel(x_ref, att_ref, ro_ref, so_ref, mod_ref, wo_ref, lnw_ref, lnb_ref, rwh_ref, rwl_ref,
                     x1_ref, u2_ref, lg_ref):
    mix = (_dot(att_ref[...], wo_ref[0:512, :]) + _dot(ro_ref[...], wo_ref[512:768, :])
           + _dot(so_ref[...], wo_ref[768:1024, :]))
    h = DEEPNORM_ALPHA * x_ref[...] + mod_ref[0, 2:3, :] * mix
    x1 = _layer_norm(h, lnw_ref[...], lnb_ref[...])
    x1_ref[...] = x1
    u2 = x1 * (1.0 + mod_ref[0, 4:5, :]) + mod_ref[0, 3:4, :]
    hi = u2.astype(BF16)
    u2_ref[...] = hi
    lo = (u2 - hi.astype(F32)).astype(BF16)
    rwh = rwh_ref[...]
    lg_ref[...] = _dot_nt(rwh, hi) + _dot_nt(rwh, lo) + _dot_nt(rwl_ref[...], hi)


def _proj_out(x, att, ro, so, modp, w_out_p, ln_w, ln_b, rwh, rwl, layer, seq_len, tt):
    t = x.shape[0]
    tiles_per_seq = seq_len // tt
    row = lambda i: (i, 0)
    lsel = lambda *w: pl.BlockSpec((None,) + w, lambda i: (layer,) + (0,) * len(w))
    in_specs = [pl.BlockSpec((tt, D_MODEL), row), pl.BlockSpec((tt, 512), row),
                pl.BlockSpec((tt, RET_WIDTH), row), pl.BlockSpec((tt, SSD_WIDTH), row),
                _mod_spec(modp, tiles_per_seq),
                lsel(D_MODEL, D_MODEL),
                pl.BlockSpec((None, None, 1, D_MODEL), lambda i: (layer, 0, 0, 0)),
                pl.BlockSpec((None, None, 1, D_MODEL), lambda i: (layer, 0, 0, 0)),
                lsel(N_EXPERTS, D_MODEL), lsel(N_EXPERTS, D_MODEL)]
    out_shape = [jax.ShapeDtypeStruct((t, D_MODEL), F32), jax.ShapeDtypeStruct((t, D_MODEL), BF16),
                 jax.ShapeDtypeStruct((N_EXPERTS, t), F32)]
    out_specs = [pl.BlockSpec((tt, D_MODEL), row), pl.BlockSpec((tt, D_MODEL), row),
                 pl.BlockSpec((N_EXPERTS, tt), lambda i: (0, i))]
    return pl.pallas_call(
        _proj_out_kernel, grid=(t // tt,), in_specs=in_specs, out_specs=out_specs, out_shape=out_shape,
        compiler_params=_cparams(1, 48), name="proj_out",
    )(x, att, ro, so, modp, w_out_p, ln_w, ln_b, rwh, rwl)


def _route_kernel(lg_ref, bias_ref, idx_ref, rank_ref, wc_ref, cnt_ref, carry_ref):
    i = pl.program_id(0)

    @pl.when(i == 0)
    def _():
        carry_ref[...] = jnp.zeros(carry_ref.shape, F32)

    scores = jax.nn.sigmoid(lg_ref[...])
    sel = scores + bias_ref[...]
    n_e, tr = sel.shape
    neg = -jnp.inf
    sub = lax.broadcasted_iota(I32, (GROUP_SIZE, tr), 0).astype(F32)
    blocks = [sel[GROUP_SIZE * g:GROUP_SIZE * (g + 1), :] for g in range(ROUTER_GROUPS)]
    gscore = []
    for blk in blocks:
        m1 = blk.max(axis=0, keepdims=True)
        pos1 = jnp.min(jnp.where(blk == m1, sub, float(GROUP_SIZE)), axis=0, keepdims=True)
        m2 = jnp.max(jnp.where(sub == pos1, neg, blk), axis=0, keepdims=True)
        gscore.append(m1 + m2)
    masked = []
    for g in range(ROUTER_GROUPS):
        beat = jnp.zeros_like(gscore[g])
        for h in range(ROUTER_GROUPS):
            if h == g:
                continue
            wins = (gscore[h] >= gscore[g]) if h < g else (gscore[h] > gscore[g])
            beat = beat + wins.astype(F32)
        masked.append(jnp.where(beat < float(ROUTER_TOPK_GROUPS), blocks[g], neg))
    cur = jnp.concatenate(masked, axis=0)
    eio = lax.broadcasted_iota(I32, (n_e, tr), 0).astype(F32)
    chosen = jnp.zeros((n_e, tr), F32)
    picks = []
    for _ in range(TOP_K):
        m = cur.max(axis=0, keepdims=True)
        ik = jnp.min(jnp.where(cur == m, eio, float(n_e)), axis=0, keepdims=True)
        hit = eio == ik
        picks.append((ik, hit))
        cur = jnp.where(hit, neg, cur)
        chosen = jnp.where(hit, 1.0, chosen)
    picked = chosen * scores
    wc_ref[:, 0, :] = picked / jnp.sum(picked, axis=0, keepdims=True) * ROUTED_SCALE
    r_ = lax.broadcasted_iota(I32, (tr, tr), 0)
    c_ = lax.broadcasted_iota(I32, (tr, tr), 1)
    chosen_b = chosen.astype(BF16)
    before = _dot(chosen_b, (r_ < c_).astype(BF16)) + carry_ref[:, 0:1]
    for k, (ik, hit) in enumerate(picks):
        idx_ref[k:k + 1, :] = ik.astype(I32)
        rank_ref[k:k + 1, :] = jnp.sum(jnp.where(hit, before, 0.0), axis=0, keepdims=True).astype(I32)
    carry_ref[...] = carry_ref[...] + _dot(chosen_b, jnp.ones((tr, LANES), BF16))
    cnt_ref[...] = carry_ref[...]


def _route(lg_t, bias, layer, tr):
    t = lg_t.shape[1]
    col = lambda i: (0, i)
    return pl.pallas_call(
        _route_kernel, grid=(t // tr,),
        in_specs=[pl.BlockSpec((N_EXPERTS, tr), col),
                  pl.BlockSpec((None, N_EXPERTS, 1), lambda i: (layer, 0, 0))],
        out_specs=[pl.BlockSpec((TOP_K, tr), col), pl.BlockSpec((TOP_K, tr), col),
                   pl.BlockSpec((N_EXPERTS, 1, tr), lambda i: (0, 0, i)),
                   pl.BlockSpec((N_EXPERTS, LANES), lambda i: (0, 0))],
        out_shape=[jax.ShapeDtypeStruct((TOP_K, t), I32), jax.ShapeDtypeStruct((TOP_K, t), I32),
                   jax.ShapeDtypeStruct((N_EXPERTS, 1, t), F32), jax.ShapeDtypeStruct((N_EXPERTS, LANES), F32)],
        scratch_shapes=[pltpu.VMEM((N_EXPERTS, LANES), F32)],
        compiler_params=_cparams(1, 48), name="route",
    )(lg_t, bias)


def _slots_kernel(start_ref, idx_ref, rank_ref, slot_ref):
    n_e = start_ref.shape[0]
    tp = idx_ref.shape[1]
    eio = lax.broadcasted_iota(I32, (n_e, tp), 0)
    start = start_ref[...]
    for k in range(TOP_K):
        first = jnp.sum(jnp.where(eio == idx_ref[k:k + 1, :], start, 0.0), axis=0, keepdims=True)
        slot_ref[k:k + 1, :] = first.astype(I32) + rank_ref[k:k + 1, :]


def _plan_kernel(slot_ref, tok_ref, *, tp):
    base = pl.program_id(0) * tp

    def body(j, carry):
        for k in range(TOP_K):
            tok_ref[slot_ref[k, j]] = base + j
        return carry

    lax.fori_loop(0, tp, body, 0)


def _plan(start, idx_t, rank_t, tp):
    t = idx_t.shape[1]
    col = lambda i: (0, i)
    slots = pl.pallas_call(
        _slots_kernel, grid=(t // tp,),
        in_specs=[pl.BlockSpec((N_EXPERTS, 1), lambda i: (0, 0)),
                  pl.BlockSpec((TOP_K, tp), col), pl.BlockSpec((TOP_K, tp), col)],
        out_specs=pl.BlockSpec((TOP_K, tp), col),
        out_shape=jax.ShapeDtypeStruct((TOP_K, t), I32),
        compiler_params=_cparams(1, 32), name="slots",
    )(start.astype(F32).reshape(N_EXPERTS, 1), idx_t, rank_t)
    return pl.pallas_call(
        functools.partial(_plan_kernel, tp=tp), grid=(t // tp,),
        in_specs=[pl.BlockSpec((TOP_K, tp), col, memory_space=pltpu.SMEM)],
        out_specs=pl.BlockSpec(memory_space=pltpu.SMEM),
        out_shape=jax.ShapeDtypeStruct((t * TOP_K,), I32),
        compiler_params=_cparams(1, 32), name="plan",
    )(slots)


def _moe_kernel(start_ref, cnt_ref, tok_ref, wc_ref, x_hbm, wg_ref, wu_ref, wd_ref, out_hbm,
                xv_ref, acc_ref, xt_ref, y_ref, trow_ref, sem, *, n_tok, rows, sub, batch):
    e = pl.program_id(0)
    n_chunks = OUT_CHUNKS
    pair_rows = 2 * n_chunks
    stride = rows + SUBLANES

    @pl.when(e == 0)
    def _():
        cp = pltpu.make_async_copy(x_hbm, xv_ref.at[pl.ds(0, n_tok * n_chunks), :], sem)
        cp.start()
        acc_ref[...] = jnp.zeros(acc_ref.shape, F32)
        xt_ref[...] = jnp.zeros(xt_ref.shape, F32)
        xv_ref[pl.ds(n_tok * n_chunks, pair_rows), :] = jnp.zeros((pair_rows, LANES), BF16)
        cp.wait()

    n = cnt_ref[e]
    first = start_ref[e]
    wg = wg_ref[...].astype(BF16)
    wu = wu_ref[...].astype(BF16)
    wd = wd_ref[...].astype(BF16)

    def block(b, carry):
        base = first + b * rows
        left = jnp.minimum(n - b * rows, rows)

        def gather_row(r, t):
            trow_ref[r] = t
            pair = xv_ref[pl.ds(pl.multiple_of((t >> 1) * pair_rows, pair_rows), pair_rows), :].astype(F32)
            xt_ref[pl.ds(r, n_chunks, stride=stride), :] = jnp.where(
                (t & 1) == 1, pair[n_chunks:pair_rows], pair[0:n_chunks])

        for s0 in range(0, rows, sub):
            @pl.when(s0 + sub <= left)
            def _():
                for r in range(s0, s0 + sub):
                    gather_row(r, tok_ref[base + r])

            @pl.when(jnp.logical_and(s0 < left, left < s0 + sub))
            def _():
                for r in range(s0, s0 + sub):
                    valid = r < left
                    gather_row(r, jnp.where(valid, tok_ref[jnp.where(valid, base + r, 0)], n_tok))

        n_sub = (left + sub - 1) // sub
        for m in range(1, rows // sub + 1):
            @pl.when(n_sub == m)
            def _():
                rm = m * sub
                xb = jnp.concatenate([xt_ref[pl.ds(c * stride, rm), :] for c in range(n_chunks)], axis=1).astype(BF16)
                h = (_silu(_dot(xb, wg)) * _dot(xb, wu)).astype(BF16)
                y = _dot(h, wd)
                for c in range(n_chunks):
                    y_ref[pl.ds(c * stride, rm), :] = y[:, LANES * c:LANES * (c + 1)]
        for s0 in range(0, rows, sub):
            @pl.when(s0 < left)
            def _():
                for r0 in range(s0, s0 + sub, batch):
                    pending = []
                    for r in range(r0, r0 + batch):
                        t = trow_ref[r]
                        wt = wc_ref[0, 0, jnp.minimum(t, n_tok - 1)]
                        dst = pl.ds(pl.multiple_of(t * n_chunks, n_chunks), n_chunks)
                        pending.append((dst, acc_ref[dst, :] + wt * y_ref[pl.ds(r, n_chunks, stride=stride), :]))
                    for dst, val in pending:
                        acc_ref[dst, :] = val
        return carry

    lax.fori_loop(0, (n + rows - 1) // rows, block, 0)

    @pl.when(e == pl.num_programs(0) - 1)
    def _():
        cp = pltpu.make_async_copy(acc_ref.at[pl.ds(0, n_tok * n_chunks), :], out_hbm, sem)
        cp.start()
        cp.wait()


def _moe(start, cnt, tok_sorted, wc, x_rows, w_gate, w_up, w_down, layer, n_tok, rows, vmem_mib):
    n_chunks = OUT_CHUNKS
    wspec = lambda a, b: pl.BlockSpec((None, None, a, b), lambda e, st, ct: (layer, e, 0, 0))
    grid_spec = pltpu.PrefetchScalarGridSpec(
        num_scalar_prefetch=2, grid=(N_EXPERTS,),
        in_specs=[pl.BlockSpec(memory_space=pltpu.SMEM),
                  pl.BlockSpec((1, 1, n_tok), lambda e, st, ct: (e, 0, 0), memory_space=pltpu.SMEM),
                  pl.BlockSpec(memory_space=pl.ANY),
                  wspec(D_MODEL, EXPERT_FF), wspec(D_MODEL, EXPERT_FF), wspec(EXPERT_FF, D_MODEL)],
        out_specs=pl.BlockSpec(memory_space=pl.ANY),
        scratch_shapes=[pltpu.VMEM(((n_tok + 2) * n_chunks, LANES), BF16),
                        pltpu.VMEM(((n_tok + 1) * n_chunks, LANES), F32),
                        pltpu.VMEM((n_chunks * (rows + SUBLANES), LANES), F32),
                        pltpu.VMEM((n_chunks * (rows + SUBLANES), LANES), F32),
                        pltpu.SMEM((rows,), I32),
                        pltpu.SemaphoreType.DMA(())])
    return pl.pallas_call(
        functools.partial(_moe_kernel, n_tok=n_tok, rows=rows, sub=64, batch=8),
        grid_spec=grid_spec, out_shape=jax.ShapeDtypeStruct((n_tok * n_chunks, LANES), F32),
        compiler_params=_cparams(1, vmem_mib), name="moe",
    )(start, cnt, tok_sorted, wc, x_rows, w_gate, w_up, w_down)


def _ffn_out_kernel(x1_ref, u2_ref, routed_ref, mod_ref, wgu_ref, wd_ref, lnw_ref, lnb_ref, o_ref):
    gu = _dot(u2_ref[...], wgu_ref[...])
    shared = _dot((_silu(gu[:, 0:SHARED_FF]) * gu[:, SHARED_FF:]).astype(BF16), wd_ref[...])
    tt = x1_ref.shape[0]
    routed = jnp.concatenate([routed_ref[pl.ds(c, tt, stride=OUT_CHUNKS), :] for c in range(OUT_CHUNKS)], axis=1)
    h = DEEPNORM_ALPHA * x1_ref[...] + mod_ref[0, 5:6, :] * (routed + shared)
    o_ref[...] = _layer_norm(h, lnw_ref[...], lnb_ref[...])


def _ffn_out(x1, u2, routed, modp, sh_gu, sh_d, ln_w, ln_b, layer, seq_len, tt):
    t = x1.shape[0]
    tiles_per_seq = seq_len // tt
    row = lambda i: (i, 0)
    lsel = lambda *w: pl.BlockSpec((None,) + w, lambda i: (layer,) + (0,) * len(w))
    ln_spec = pl.BlockSpec((None, None, 1, D_MODEL), lambda i: (layer, 1, 0, 0))
    return pl.pallas_call(
        _ffn_out_kernel, grid=(t // tt,),
        in_specs=[pl.BlockSpec((tt, D_MODEL), row), pl.BlockSpec((tt, D_MODEL), row),
                  pl.BlockSpec((tt * OUT_CHUNKS, LANES), row),
                  _mod_spec(modp, tiles_per_seq),
                  lsel(D_MODEL, 2 * SHARED_FF), lsel(SHARED_FF, D_MODEL), ln_spec, ln_spec],
        out_specs=pl.BlockSpec((tt, D_MODEL), row),
        out_shape=jax.ShapeDtypeStruct((t, D_MODEL), F32),
        compiler_params=_cparams(1, 48), name="ffn_out",
    )(x1, u2, routed, modp, sh_gu, sh_d, ln_w, ln_b)


def _q_perm():
    cols = []
    for j in range(ATT_HEADS // 2):
        for half in range(2):
            h = j + (ATT_HEADS // 2) * half
            cols.extend(range(h * HEAD_DIM, (h + 1) * HEAD_DIM))
    return np.asarray(cols, np.int32)


def _prepare_params(p):
    off = np.concatenate([[0], np.cumsum(IN_SIZES)])
    seg = lambda k: np.arange(off[k], off[k + 1], dtype=np.int32)
    qp = _q_perm()
    cols = np.concatenate([qp, seg(1), seg(2), seg(3), seg(4), seg(5), seg(6), seg(7), seg(9), seg(10), seg(8), seg(11)])
    w_in = jnp.take(p["w_in"], jnp.asarray(cols), axis=2)
    w_in = jnp.pad(w_in, ((0, 0), (0, 0), (0, IN_W_PAD - w_in.shape[2]))).astype(BF16)
    rows = np.concatenate([qp, np.arange(ATT_WIDTH, D_MODEL, dtype=np.int32)])
    w_out = jnp.take(p["w_out"], jnp.asarray(rows), axis=1).astype(BF16)
    rwt = jnp.swapaxes(p["router_w"], 1, 2)
    rwh = rwt.astype(BF16)
    rwl = (rwt - rwh.astype(F32)).astype(BF16)

    def lane_pad(a, fill=0.0):
        a = a.reshape(DEPTH, 1, -1)
        return jnp.pad(a, ((0, 0), (0, 0), (0, LANES - a.shape[2])), constant_values=fill)

    gi = np.arange(512) // HEAD_DIM
    g512 = jnp.asarray(gi[:, None] == gi[None, :], BF16)
    return {
        "w_in": w_in, "w_out": w_out, "rwh": rwh, "rwl": rwl,
        "qnw": jnp.tile(p["q_norm_w"], (1, ATT_HEADS)).reshape(DEPTH, 1, 512),
        "knw": jnp.tile(p["k_norm_w"], (1, ATT_KV_HEADS)).reshape(DEPTH, 1, LANES),
        "g512": g512, "g128": g512[0:LANES, 0:LANES],
        "conv_w": p["ssd_conv_w"], "conv_b": p["ssd_conv_b"].reshape(DEPTH, 1, CONV_CH),
        "dt_bias": lane_pad(p["ssd_dt_bias"]), "a_log": lane_pad(p["ssd_a_log"]),
        "ret_ld": lane_pad(p["ret_log_decay"], -1.0),
        "ret_nw": p["ret_norm_w"].reshape(DEPTH, 1, RET_WIDTH),
        "d_lane": jnp.repeat(p["ssd_d"], HEAD_DIM, axis=1).reshape(DEPTH, 1, SSD_WIDTH),
        "ssd_nw": p["ssd_norm_w"].reshape(DEPTH, 1, SSD_WIDTH),
        "ln_w": p["ln_w"].reshape(DEPTH, 2, 1, D_MODEL), "ln_b": p["ln_b"].reshape(DEPTH, 2, 1, D_MODEL),
        "router_bias": p["router_bias"].reshape(DEPTH, N_EXPERTS, 1),
        "sh_gu": jnp.concatenate([p["sh_w_gate"], p["sh_w_up"]], axis=2).astype(BF16),
        "sh_d": p["sh_w_down"].astype(BF16),
        "exp_w_gate": p["exp_w_gate"], "exp_w_up": p["exp_w_up"], "exp_w_down": p["exp_w_down"],
    }


def _rope_tables(n_tokens):
    t = jnp.arange(n_tokens, dtype=I32)
    rows = (t // GRID_W).astype(F32)
    cols = (t % GRID_W).astype(F32)
    inv = ROPE_THETA ** (-jnp.arange(ROPE_AXIS_FREQS, dtype=F32) / ROPE_AXIS_FREQS)
    ar = rows[:, None] * inv
    ac = cols[:, None] * inv
    cos = jnp.concatenate([jnp.cos(ar), jnp.cos(ar), jnp.cos(ac), jnp.cos(ac)], axis=1)
    sin = jnp.concatenate([-jnp.sin(ar), jnp.sin(ar), -jnp.sin(ac), jnp.sin(ac)], axis=1)
    return jnp.tile(cos, (1, 2)), jnp.tile(sin, (1, 2))


def _pair_states(st, kind):
    b = st.shape[0]
    out = jnp.zeros((b, 2, 2, LANES, LANES), F32)
    for pr in range(2):
        for tt in range(2):
            r0 = HEAD_DIM * (tt if kind == "ret" else pr)
            out = out.at[:, :, pr, r0:r0 + HEAD_DIM, HEAD_DIM * tt:HEAD_DIM * (tt + 1)].set(st[:, :, 2 * pr + tt])
    return out


def _unpair_states(m, kind):
    heads = []
    for pr in range(2):
        for tt in range(2):
            r0 = HEAD_DIM * (tt if kind == "ret" else pr)
            heads.append(m[:, :, pr, r0:r0 + HEAD_DIM, HEAD_DIM * tt:HEAD_DIM * (tt + 1)])
    return jnp.stack(heads, axis=2)


def _trunk_layer(x, modp, pw, layer, n_seq, seq_len, rope_tabs, cache, init, cfg):
    n_tok = x.shape[0]
    ctx = cache is None
    outs = _proj_in(x, modp, pw["w_in"], pw["qnw"], pw["knw"], pw["g512"], rope_tabs, layer, seq_len,
                    cfg["tt"], emit_cache=ctx)
    q, k, v, ret, ssd = outs[:5]
    att = _attention(q, k, v, cache, layer, n_seq, seq_len, cfg["tq"])
    sc = _scan(ret, ssd, pw, layer, init, n_seq, seq_len, cfg["chunk"], emit_state=ctx)
    x1, u2, lg_t = _proj_out(x, att, sc[0], sc[1], modp, pw["w_out"], pw["ln_w"], pw["ln_b"],
                                  pw["rwh"], pw["rwl"], layer, seq_len, cfg["tt"])
    idx_t, rank_t, wc, counts = _route(lg_t, pw["router_bias"], layer, cfg["tr"])
    cnt = counts[:, 0].astype(I32)
    start = jnp.cumsum(cnt) - cnt
    tok_sorted = _plan(start, idx_t, rank_t, cfg["tr"])
    u2_rows = u2.reshape(n_tok * OUT_CHUNKS, LANES)
    routed = _moe(start, cnt, tok_sorted, wc, u2_rows, pw["exp_w_gate"], pw["exp_w_up"], pw["exp_w_down"],
                  layer, n_tok, cfg["rows"], cfg["moe_vmem"])
    x_new = _ffn_out(x1, u2, routed, modp, pw["sh_gu"], pw["sh_d"], pw["ln_w"], pw["ln_b"], layer, seq_len, cfg["tt"])
    new_ctx = (outs[5], outs[6], sc[2], sc[3]) if ctx else None
    return x_new, new_ctx


def kernel(x_prompt, x_sample, cache_k, cache_v, state_ret, state_ssd, c, c_ctx,
           w_mod, b_mod, w_in, q_norm_w, k_norm_w, ret_log_decay, ret_norm_w,
           ssd_conv_w, ssd_conv_b, ssd_dt_bias, ssd_a_log, ssd_d, ssd_norm_w, w_out,
           ln_w, ln_b, router_w, router_bias, exp_w_gate, exp_w_up, exp_w_down,
           sh_w_gate, sh_w_up, sh_w_down):
    pw = _prepare_params({
        "w_in": w_in, "q_norm_w": q_norm_w, "k_norm_w": k_norm_w, "ret_log_decay": ret_log_decay,
        "ret_norm_w": ret_norm_w, "ssd_conv_w": ssd_conv_w, "ssd_conv_b": ssd_conv_b, "ssd_dt_bias": ssd_dt_bias,
        "ssd_a_log": ssd_a_log, "ssd_d": ssd_d, "ssd_norm_w": ssd_norm_w, "w_out": w_out, "ln_w": ln_w, "ln_b": ln_b,
        "router_w": router_w, "router_bias": router_bias, "exp_w_gate": exp_w_gate, "exp_w_up": exp_w_up,
        "exp_w_down": exp_w_down, "sh_w_gate": sh_w_gate, "sh_w_up": sh_w_up, "sh_w_down": sh_w_down})
    nb, seq, d = x_prompt.shape
    nd, dseq, _ = x_sample.shape
    cond = jnp.concatenate([c_ctx[None], c, jnp.zeros((8 - 1 - nd, d), F32)], axis=0)
    mod = _modulation(cond, w_mod, b_mod)
    rope_tabs = _rope_tables(dseq)
    past = cache_k.shape[2]
    ck = cache_k.reshape(nd, DEPTH, past, LANES)
    cv = cache_v.reshape(nd, DEPTH, past, LANES)
    cfg_ctx = {"tt": 512, "tq": seq, "chunk": SCAN_CHUNK, "tr": 512, "rows": MOE_ROWS, "moe_vmem": MOE_VMEM_MIB}
    cfg_lat = {"tt": 512, "tq": 512, "chunk": SCAN_CHUNK, "tr": 512, "rows": MOE_ROWS, "moe_vmem": MOE_VMEM_MIB}

    xc = x_prompt.reshape(nb * seq, d)
    xs = x_sample.reshape(nd * dseq, d)
    ks, vs, srs, sss = [], [], [], []
    for l in range(DEPTH):
        mod_ctx = mod[l, 0:1].reshape(1, 6, d)
        mod_lat = mod[l, 1:1 + nd].reshape(nd, 6, d)
        xc, (k_l, v_l, sr_l, ss_l) = _trunk_layer(xc, mod_ctx, pw, l, nb, seq, None, None, None, cfg_ctx)
        ks.append(k_l.reshape(nb, seq, ATT_KV_HEADS, HEAD_DIM))
        vs.append(v_l.reshape(nb, seq, ATT_KV_HEADS, HEAD_DIM))
        srs.append(_unpair_states(sr_l, "ret"))
        sss.append(_unpair_states(ss_l, "ssd"))
        init = (_pair_states(state_ret[:, l], "ret"), _pair_states(state_ssd[:, l], "ssd"))
        xs, _ = _trunk_layer(xs, mod_lat, pw, l, nd, dseq, rope_tabs, (ck, cv), init, cfg_lat)
    return (xc.reshape(nb, seq, d), xs.reshape(nd, dseq, d), jnp.stack(ks, axis=1), jnp.stack(vs, axis=1),
            jnp.stack(srs, axis=1), jnp.stack(sss, axis=1))
```

```python
import functools

import numpy as np
import jax
import jax.numpy as jnp
from jax import lax
from jax.experimental import pallas as pl
from jax.experimental.pallas import tpu as pltpu

F32 = jnp.float32
BF16 = jnp.bfloat16
I32 = jnp.int32
HIGHEST = lax.Precision.HIGHEST
NT_DIMS = (((1,), (1,)), ((), ()))

D_MODEL = 1024
DEPTH = 4
GRID_W = 64
HEAD_DIM = 64
ATT_HEADS = 8
ATT_KV_HEADS = 2
ATT_WIDTH = ATT_HEADS * HEAD_DIM
ROPE_THETA = 10000.0
ROPE_AXIS_FREQS = HEAD_DIM // 4
RET_HEADS = 4
RET_DK = 64
RET_WIDTH = 256
SSD_HEADS = 4
SSD_WIDTH = 256
SSD_STATE = 64
N_EXPERTS = 256
TOP_K = 8
ROUTER_GROUPS = 8
ROUTER_TOPK_GROUPS = 4
GROUP_SIZE = N_EXPERTS // ROUTER_GROUPS
EXPERT_FF = 256
SHARED_FF = 256
ROUTED_SCALE = 2.5
DEEPNORM_ALPHA = (2 * DEPTH) ** 0.25
IN_SIZES = (512, 128, 128, 256, 256, 256, 256, 256, 256, 128, 128, 8)

LANES = 128
SUBLANES = 8
MIB = 1024 * 1024

QKV_W = 768
RET_OFF, RET_W = 768, 1024
SSD_OFF, SSD_W = 1792, 896
IN_W_PAD = 2688
CONV_CH = 512

OUT_CHUNKS = D_MODEL // LANES
PLAN_BLOCK = 4096
SCAN_CHUNK = 256
MOE_ROWS = 320
MOE_VMEM_MIB = 62


def _cparams(n_grid, vmem_mib):
    return pltpu.CompilerParams(dimension_semantics=("arbitrary",) * n_grid,
                                vmem_limit_bytes=vmem_mib * MIB)


def _dot(a, b):
    return jnp.dot(a, b, preferred_element_type=F32)


def _dot_nt(a, b):
    return lax.dot_general(a, b, NT_DIMS, preferred_element_type=F32)


def _silu(x):
    return x * jax.nn.sigmoid(x)


def _softplus(x):
    return jnp.maximum(x, 0.0) + jnp.log1p(jnp.exp(-jnp.abs(x)))


def _mod_spec(modp, tiles_per_seq):
    if modp.shape[0] == 1:
        return pl.BlockSpec((1, 6, D_MODEL), lambda i: (0, 0, 0))
    return pl.BlockSpec((1, 6, D_MODEL), lambda i: (i // tiles_per_seq, 0, 0))


def _group_sum(x, g):
    hi = x.astype(BF16)
    lo = (x - hi.astype(F32)).astype(BF16)
    return _dot(hi, g) + _dot(lo, g)


def _swap16(x):
    n = x.shape[1]
    lane = lax.broadcasted_iota(I32, x.shape, 1)
    return jnp.where((lane & 16) == 0, pltpu.roll(x, n - 16, 1), pltpu.roll(x, 16, 1))


def _layer_norm(h, w, b):
    mu = jnp.mean(h, axis=-1, keepdims=True)
    d = h - mu
    var = jnp.mean(d * d, axis=-1, keepdims=True)
    return d * lax.rsqrt(var + 1e-5) * w + b


def _mod_kernel(cond_ref, w_ref, b_ref, o_ref):
    c = cond_ref[...]
    o_ref[0] = jnp.dot(_silu(c), w_ref[0], precision=HIGHEST, preferred_element_type=F32) + b_ref[0]


def _modulation(cond8, w_mod, b_mod):
    tn = 1536
    n_col = w_mod.shape[2] // tn
    return pl.pallas_call(
        _mod_kernel,
        grid=(DEPTH, n_col),
        in_specs=[pl.BlockSpec((8, D_MODEL), lambda l, j: (0, 0)),
                  pl.BlockSpec((1, D_MODEL, tn), lambda l, j: (l, 0, j)),
                  pl.BlockSpec((1, 1, tn), lambda l, j: (l, 0, j))],
        out_specs=pl.BlockSpec((1, 8, tn), lambda l, j: (l, 0, j)),
        out_shape=jax.ShapeDtypeStruct((DEPTH, 8, w_mod.shape[2]), F32),
        compiler_params=_cparams(2, 32),
        name="modulation",
    )(cond8, w_mod, b_mod.reshape(DEPTH, 1, -1))


def _proj_in_kernel(*refs, rope, emit_cache):
    x_ref, mod_ref, w_ref, qnw_ref, knw_ref, g_ref = refs[:6]
    pos = 6
    if rope:
        cos_ref, sin_ref = refs[pos:pos + 2]
        pos += 2
    q_ref, k_ref, v_ref, ret_ref, ssd_ref = refs[pos:pos + 5]
    pos += 5
    u = x_ref[...] * (1.0 + mod_ref[0, 1:2, :]) + mod_ref[0, 0:1, :]
    ub = u.astype(BF16)
    q = _dot(ub, w_ref[:, 0:512])
    k = _dot(ub, w_ref[:, 512:640])
    v = _dot(ub, w_ref[:, 640:768])
    g = g_ref[...]
    inv_hd = 1.0 / HEAD_DIM
    qn = q * lax.rsqrt(_group_sum(q * q, g) * inv_hd + 1e-6) * qnw_ref[...]
    kn = k * lax.rsqrt(_group_sum(k * k, g[0:LANES, 0:LANES]) * inv_hd + 1e-6) * knw_ref[...]
    if emit_cache:
        kf_ref, vf_ref = refs[pos:pos + 2]
        kf_ref[...] = kn
        vf_ref[...] = v
    if rope:
        c = cos_ref[...]
        s = sin_ref[...]
        kn = kn * c + _swap16(kn) * s
        qn = qn * jnp.concatenate([c] * 4, axis=1) + _swap16(qn) * jnp.concatenate([s] * 4, axis=1)
    q_ref[...] = (qn * (HEAD_DIM ** -0.5)).astype(BF16)
    k_ref[...] = kn.astype(BF16)
    v_ref[...] = v.astype(BF16)
    ret_ref[...] = _dot(ub, w_ref[:, RET_OFF:RET_OFF + RET_W])
    ssd_ref[...] = _dot(ub, w_ref[:, SSD_OFF:SSD_OFF + SSD_W])


def _proj_in(x, modp, w_in_p, qnw, knw, g512, rope_tabs, layer, seq_len, tt, emit_cache):
    t = x.shape[0]
    tiles_per_seq = seq_len // tt
    rope = rope_tabs is not None
    row = lambda i: (i, 0)
    const2 = lambda i: (0, 0)
    in_specs = [pl.BlockSpec((tt, D_MODEL), row),
                _mod_spec(modp, tiles_per_seq),
                pl.BlockSpec((None, D_MODEL, IN_W_PAD), lambda i: (layer, 0, 0)),
                pl.BlockSpec((None, 1, 512), lambda i: (layer, 0, 0)),
                pl.BlockSpec((None, 1, LANES), lambda i: (layer, 0, 0)),
                pl.BlockSpec((512, 512), const2)]
    args = [x, modp, w_in_p, qnw, knw, g512]
    if rope:
        tab = pl.BlockSpec((tt, LANES), lambda i: (i % tiles_per_seq, 0))
        in_specs += [tab, tab]
        args += list(rope_tabs)
    out_shape = [jax.ShapeDtypeStruct((t, 512), BF16), jax.ShapeDtypeStruct((t, LANES), BF16),
                 jax.ShapeDtypeStruct((t, LANES), BF16), jax.ShapeDtypeStruct((t, RET_W), F32),
                 jax.ShapeDtypeStruct((t, SSD_W), F32)]
    out_specs = [pl.BlockSpec((tt, 512), row), pl.BlockSpec((tt, LANES), row), pl.BlockSpec((tt, LANES), row),
                 pl.BlockSpec((tt, RET_W), row), pl.BlockSpec((tt, SSD_W), row)]
    if emit_cache:
        out_shape += [jax.ShapeDtypeStruct((t, LANES), F32)] * 2
        out_specs += [pl.BlockSpec((tt, LANES), row)] * 2
    return pl.pallas_call(
        functools.partial(_proj_in_kernel, rope=rope, emit_cache=emit_cache),
        grid=(t // tt,), in_specs=in_specs, out_specs=out_specs, out_shape=out_shape,
        compiler_params=_cparams(1, 48), name="proj_in",
    )(*args)


def _attn_kernel(*refs, with_cache, group):
    q_ref, k_ref, v_ref = refs[:3]
    n_in = 5 if with_cache else 3
    o_ref = refs[n_in]
    s_refs = refs[n_in + 1:n_in + 3]
    p_refs = refs[n_in + 3:n_in + 5]
    tq = q_ref.shape[0]
    lane = lax.broadcasted_iota(I32, (1, LANES), 1)
    first = lane < HEAD_DIM
    parts = [(k_ref[...], v_ref[...])]
    if with_cache:
        parts.append((refs[3][...].astype(BF16), refs[4][...].astype(BF16)))
    bounds = np.cumsum([0] + [kk.shape[0] for kk, _ in parts])
    zero = jnp.zeros((), BF16)
    kmask = [[jnp.where(first, kk, zero), jnp.where(first, zero, kk)] for kk, _ in parts]
    one = jnp.ones((), BF16)
    vaug = [[jnp.where(first, vv, one), jnp.where(first, one, vv)] for _, vv in parts]
    head = 0
    for j in range(ATT_HEADS // 2):
        qb = q_ref[:, LANES * j:LANES * (j + 1)]
        outs = []
        for g in range(ATT_KV_HEADS):
            s_ref, p_ref = s_refs[head % 2], p_refs[head % 2]
            head += 1
            for p, km in enumerate(kmask):
                s_ref[:, bounds[p]:bounds[p + 1]] = _dot_nt(qb, km[g])
            for r0 in range(0, tq, group):
                s = s_ref[r0:r0 + group, :]
                p_ref[r0:r0 + group, :] = jnp.exp(s - s.max(axis=-1, keepdims=True)).astype(BF16)
            acc = None
            for p, va in enumerate(vaug):
                o = _dot(p_ref[:, bounds[p]:bounds[p + 1]], va[g])
                acc = o if acc is None else acc + o
            outs.append(acc / pltpu.roll(acc, HEAD_DIM, 1))
        o_ref[:, LANES * j:LANES * (j + 1)] = jnp.where(first, outs[0], outs[1]).astype(BF16)


def _attention(q, k, v, cache, layer, n_seq, seq_len, tq):
    t = q.shape[0]
    nq = seq_len // tq
    in_specs = [pl.BlockSpec((tq, 512), lambda b, i: (b * nq + i, 0)),
                pl.BlockSpec((seq_len, LANES), lambda b, i: (b, 0)),
                pl.BlockSpec((seq_len, LANES), lambda b, i: (b, 0))]
    args = [q, k, v]
    n_keys = seq_len
    if cache is not None:
        past = cache[0].shape[2]
        n_keys += past
        cspec = pl.BlockSpec((None, None, past, LANES), lambda b, i: (b, layer, 0, 0))
        in_specs += [cspec, cspec]
        args += list(cache)
    return pl.pallas_call(
        functools.partial(_attn_kernel, with_cache=cache is not None, group=2 * SUBLANES),
        grid=(n_seq, nq), in_specs=in_specs,
        out_specs=pl.BlockSpec((tq, 512), lambda b, i: (b * nq + i, 0)),
        out_shape=jax.ShapeDtypeStruct((t, 512), BF16),
        scratch_shapes=[pltpu.VMEM((tq, n_keys), F32)] * 2 + [pltpu.VMEM((tq, n_keys), BF16)] * 2,
        compiler_params=_cparams(2, 56), name="attention",
    )(*args)


def _scan_kernel(*refs, n, chunk, has_init, emit_state):
    (ret_ref, ssd_ref, hp_ref, hn_ref, cw_ref, cb_ref, dtb_ref, alog_ref, rld_ref,
     rnw_ref, dl_ref, snw_ref, g_ref) = refs[:13]
    pos = 13
    if has_init:
        s0r_ref, s0s_ref = refs[pos:pos + 2]
        pos += 2
    reto_ref, ssdo_ref = refs[pos:pos + 2]
    pos += 2
    if emit_state:
        str_ref, sts_ref = refs[pos:pos + 2]
        pos += 2
    of_ref, st_ref = refs[pos:pos + 2]

    C = chunk
    s = pl.program_id(1)
    bwd = s >= n
    c = jnp.where(bwd, 2 * n - 1 - s, s)

    def init_state(direction):
        if has_init:
            st_ref[0:2] = s0s_ref[0, direction]
            st_ref[2:4] = s0r_ref[0, direction]
        else:
            st_ref[...] = jnp.zeros(st_ref.shape, F32)

    @pl.when(s == 0)
    def _():
        init_state(0)

    @pl.when(s == n)
    def _():
        init_state(1)

    lane = lax.broadcasted_iota(I32, (1, LANES), 1)
    first = lane < HEAD_DIM
    ret = ret_ref[...]
    ssd = ssd_ref[...]

    xin = ssd[:, 0:CONV_CH]
    rowi = lax.broadcasted_iota(I32, (C, 1), 0)
    prev_row = jnp.where(c == 0, 0.0, hp_ref[SUBLANES - 1:SUBLANES, :])
    next_row = jnp.where(c == n - 1, 0.0, hn_ref[0:1, :])
    xp = jnp.where(rowi == 0, prev_row, pltpu.roll(xin, 1, 0))
    xn = jnp.where(rowi == C - 1, next_row, pltpu.roll(xin, C - 1, 0))
    xbc = _silu(xp * cw_ref[0:1, :] + xin * cw_ref[1:2, :] + xn * cw_ref[2:3, :] + cb_ref[...])
    sx = xbc[:, 0:256]
    sb = xbc[:, 256:384]
    sc = xbc[:, 384:512]

    def this_dir(a):
        return jnp.where(bwd, pltpu.roll(a, LANES - SSD_HEADS, 1), a)

    dt_all = _softplus(ssd[:, 768:896] + dtb_ref[...])
    la_ssd = this_dir(dt_all * (-jnp.exp(alog_ref[...])))
    dt_dir = this_dir(dt_all)
    lg = jnp.broadcast_to(jnp.log1p(-jnp.exp(rld_ref[...])), (SUBLANES, LANES))
    lg_dir = pltpu.roll(this_dir(lg), SSD_HEADS, 1)[0:1, :]
    la = jnp.where(lane < 4, la_ssd, jnp.where(lane < 8, lg_dir, 0.0))
    ii = lax.broadcasted_iota(I32, (C, C), 0)
    jj = lax.broadcasted_iota(I32, (C, C), 1)
    msk = jnp.where(bwd, jj - ii, ii - jj) >= 0
    cum = jnp.dot(msk.astype(F32), la, precision=HIGHEST, preferred_element_type=F32)
    last = jnp.where(bwd, cum[0:1, :], cum[C - 1:C, :])
    acol = jnp.where(lane < 8, cum, jnp.where(lane < 12, pltpu.roll(dt_dir, 8, 1), 0.0))
    arow = acol.T

    def decay(h, scale):
        diff = cum[:, h:h + 1] - arow[h:h + 1, :]
        return jnp.where(msk, jnp.exp(jnp.where(msk, diff, 0.0)), 0.0) * scale

    def lanes2(a0, a1):
        return jnp.where(first, a0, a1)

    ii2 = lax.broadcasted_iota(I32, (LANES, LANES), 0)
    jj2 = lax.broadcasted_iota(I32, (LANES, LANES), 1)
    blockdiag = ((ii2 < HEAD_DIM) == (jj2 < HEAD_DIM)).astype(F32)

    o_pairs = []
    for kind in ("ssd", "ret"):
        for p in range(2):
            slot = p if kind == "ssd" else 2 + p
            heads = (2 * p, 2 * p + 1)
            hl = [h if kind == "ssd" else 4 + h for h in heads]
            if kind == "ssd":
                grp = first if p == 0 else jnp.logical_not(first)
                qm = jnp.where(grp, sc, 0.0).astype(BF16)
                km = jnp.where(grp, sb, 0.0)
                v128 = sx[:, LANES * p:LANES * (p + 1)]
                sg = _dot_nt(qm, sb.astype(BF16))
                scores = [sg, sg]
                rs = [arow[8 + h:9 + h, :] for h in heads]
                cs = [acol[:, 8 + h:9 + h] for h in heads]
                q_inter = qm
            else:
                q128 = ret[:, LANES * p:LANES * (p + 1)]
                km = ret[:, 256 + LANES * p:256 + LANES * (p + 1)]
                v128 = ret[:, 512 + LANES * p:512 + LANES * (p + 1)]
                kb = km.astype(BF16)
                scores = [_dot_nt(jnp.where(first, q128, 0.0).astype(BF16), kb),
                          _dot_nt(jnp.where(first, 0.0, q128).astype(BF16), kb)]
                rs = [RET_DK ** -0.5] * 2
                cs = [RET_DK ** -0.5] * 2
                q_inter = q128.astype(BF16)
            vb = v128.astype(BF16)
            intra = [_dot((scores[t] * decay(hl[t], rs[t])).astype(BF16), vb) for t in range(2)]
            state = st_ref[slot]
            ecum = lanes2(jnp.exp(cum[:, hl[0]:hl[0] + 1]), jnp.exp(cum[:, hl[1]:hl[1] + 1]))
            o_pairs.append(lanes2(intra[0], intra[1]) + _dot(q_inter, state.astype(BF16)) * ecum)
            wv = lanes2(jnp.exp(last[:, hl[0]:hl[0] + 1] - cum[:, hl[0]:hl[0] + 1]) * cs[0],
                        jnp.exp(last[:, hl[1]:hl[1] + 1] - cum[:, hl[1]:hl[1] + 1]) * cs[1])
            snew = _dot(km.T.astype(BF16), (v128 * wv).astype(BF16))
            if kind == "ret":
                snew = snew * blockdiag
            elast = lanes2(jnp.exp(last[:, hl[0]:hl[0] + 1]), jnp.exp(last[:, hl[1]:hl[1] + 1]))
            st_ref[slot] = elast * state + snew
    o_all = jnp.concatenate(o_pairs, axis=1)
    rows = pl.ds(pl.multiple_of(c * C, C), C)

    @pl.when(jnp.logical_not(bwd))
    def _():
        of_ref[rows, :] = o_all

    @pl.when(bwd)
    def _():
        tot = of_ref[rows, :] + o_all
        g = g_ref[...]
        inv = 1.0 / HEAD_DIM
        for p in range(2):
            o = tot[:, 256 + LANES * p:256 + LANES * (p + 1)]
            d = o - _group_sum(o, g) * inv
            var = _group_sum(d * d, g) * inv
            rn = d * lax.rsqrt(var + 1e-5) * rnw_ref[:, LANES * p:LANES * (p + 1)]
            gate = ret[:, 768 + LANES * p:768 + LANES * (p + 1)]
            reto_ref[:, LANES * p:LANES * (p + 1)] = (rn * _silu(gate)).astype(BF16)
        y = (tot[:, 0:256] + dl_ref[...] * sx) * _silu(ssd[:, 512:768])
        ms = jnp.mean(y * y, axis=-1, keepdims=True)
        ssdo_ref[...] = (y * lax.rsqrt(ms + 1e-6) * snw_ref[...]).astype(BF16)

    if emit_state:
        @pl.when(s == n - 1)
        def _():
            sts_ref[0, 0] = st_ref[0:2]
            str_ref[0, 0] = st_ref[2:4]

        @pl.when(s == 2 * n - 1)
        def _():
            sts_ref[0, 1] = st_ref[0:2]
            str_ref[0, 1] = st_ref[2:4]


def _scan(ret, ssd, pw, layer, init, n_seq, seq_len, chunk, emit_state):
    t = ret.shape[0]
    n = seq_len // chunk
    rows8 = t // SUBLANES

    def cidx(s):
        return jnp.where(s >= n, 2 * n - 1 - s, s)

    def oidx(s):
        return jnp.where(s >= n, 2 * n - 1 - s, n - 1)

    par = lambda w: pl.BlockSpec((None,) + w, lambda b, s: (layer,) + (0,) * len(w))
    in_specs = [pl.BlockSpec((chunk, RET_W), lambda b, s: (b * n + cidx(s), 0)),
                pl.BlockSpec((chunk, SSD_W), lambda b, s: (b * n + cidx(s), 0)),
                pl.BlockSpec((SUBLANES, CONV_CH),
                             lambda b, s: (jnp.maximum((b * seq_len + cidx(s) * chunk) // SUBLANES - 1, 0), 0)),
                pl.BlockSpec((SUBLANES, CONV_CH),
                             lambda b, s: (jnp.minimum((b * seq_len + (cidx(s) + 1) * chunk) // SUBLANES, rows8 - 1), 0)),
                par((3, CONV_CH)), par((1, CONV_CH)), par((1, LANES)), par((1, LANES)), par((1, LANES)),
                par((1, RET_WIDTH)), par((1, SSD_WIDTH)), par((1, SSD_WIDTH)),
                pl.BlockSpec((LANES, LANES), lambda b, s: (0, 0))]
    args = [ret, ssd, ssd, ssd, pw["conv_w"], pw["conv_b"], pw["dt_bias"], pw["a_log"], pw["ret_ld"],
            pw["ret_nw"], pw["d_lane"], pw["ssd_nw"], pw["g128"]]
    st_spec = pl.BlockSpec((1, 2, 2, LANES, LANES), lambda b, s: (b, 0, 0, 0, 0))
    if init is not None:
        in_specs += [st_spec, st_spec]
        args += list(init)
    out_specs = [pl.BlockSpec((chunk, RET_WIDTH), lambda b, s: (b * n + oidx(s), 0)),
                 pl.BlockSpec((chunk, SSD_WIDTH), lambda b, s: (b * n + oidx(s), 0))]
    out_shape = [jax.ShapeDtypeStruct((t, RET_WIDTH), BF16), jax.ShapeDtypeStruct((t, SSD_WIDTH), BF16)]
    if emit_state:
        out_specs += [st_spec, st_spec]
        out_shape += [jax.ShapeDtypeStruct((n_seq, 2, 2, LANES, LANES), F32)] * 2
    return pl.pallas_call(
        functools.partial(_scan_kernel, n=n, chunk=chunk, has_init=init is not None, emit_state=emit_state),
        grid=(n_seq, 2 * n), in_specs=in_specs, out_specs=out_specs, out_shape=out_shape,
        scratch_shapes=[pltpu.VMEM((seq_len, 512), F32), pltpu.VMEM((4, LANES, LANES), F32)],
        compiler_params=_cparams(2, 48), name="scan",
    )(*args)


def _proj_out_kernel(x_ref, att_ref, ro_ref, so_ref, mod_ref, wo_ref, lnw_ref, lnb_ref, rwh_ref, rwl_ref,
                     x1_ref, u2_ref, u2c_ref, lg_ref, rows_ref):
    mix = (_dot(att_ref[...], wo_ref[0:512, :]) + _dot(ro_ref[...], wo_ref[512:768, :])
           + _dot(so_ref[...], wo_ref[768:1024, :]))
    h = DEEPNORM_ALPHA * x_ref[...] + mod_ref[0, 2:3, :] * mix
    x1 = _layer_norm(h, lnw_ref[...], lnb_ref[...])
    x1_ref[...] = x1
    u2 = x1 * (1.0 + mod_ref[0, 4:5, :]) + mod_ref[0, 3:4, :]
    hi = u2.astype(BF16)
    u2_ref[...] = hi
    for c in range(OUT_CHUNKS):
        rows_ref[pl.ds(c, u2.shape[0], stride=OUT_CHUNKS), :] = u2[:, LANES * c:LANES * (c + 1)]
    u2c_ref[...] = rows_ref[...].astype(BF16)
    lo = (u2 - hi.astype(F32)).astype(BF16)
    rwh = rwh_ref[...]
    lg_ref[...] = _dot_nt(rwh, hi) + _dot_nt(rwh, lo) + _dot_nt(rwl_ref[...], hi)


def _proj_out(x, att, ro, so, modp, w_out_p, ln_w, ln_b, rwh, rwl, layer, seq_len, tt):
    t = x.shape[0]
    tiles_per_seq = seq_len // tt
    row = lambda i: (i, 0)
    lsel = lambda *w: pl.BlockSpec((None,) + w, lambda i: (layer,) + (0,) * len(w))
    in_specs = [pl.BlockSpec((tt, D_MODEL), row), pl.BlockSpec((tt, 512), row),
                pl.BlockSpec((tt, RET_WIDTH), row), pl.BlockSpec((tt, SSD_WIDTH), row),
                _mod_spec(modp, tiles_per_seq),
                lsel(D_MODEL, D_MODEL),
                pl.BlockSpec((None, None, 1, D_MODEL), lambda i: (layer, 0, 0, 0)),
                pl.BlockSpec((None, None, 1, D_MODEL), lambda i: (layer, 0, 0, 0)),
                lsel(N_EXPERTS, D_MODEL), lsel(N_EXPERTS, D_MODEL)]
    out_shape = [jax.ShapeDtypeStruct((t, D_MODEL), F32), jax.ShapeDtypeStruct((t, D_MODEL), BF16),
                 jax.ShapeDtypeStruct((t * OUT_CHUNKS, LANES), BF16), jax.ShapeDtypeStruct((N_EXPERTS, t), F32)]
    out_specs = [pl.BlockSpec((tt, D_MODEL), row), pl.BlockSpec((tt, D_MODEL), row),
                 pl.BlockSpec((tt * OUT_CHUNKS, LANES), row), pl.BlockSpec((N_EXPERTS, tt), lambda i: (0, i))]
    return pl.pallas_call(
        _proj_out_kernel, grid=(t // tt,), in_specs=in_specs, out_specs=out_specs, out_shape=out_shape,
        scratch_shapes=[pltpu.VMEM((tt * OUT_CHUNKS, LANES), F32)],
        compiler_params=_cparams(1, 48), name="proj_out",
    )(x, att, ro, so, modp, w_out_p, ln_w, ln_b, rwh, rwl)


def _route_kernel(lg_ref, bias_ref, idx_ref, rank_ref, wc_ref, cnt_ref, carry_ref):
    i = pl.program_id(0)

    @pl.when(i == 0)
    def _():
        carry_ref[...] = jnp.zeros(carry_ref.shape, F32)

    scores = jax.nn.sigmoid(lg_ref[...])
    sel = scores + bias_ref[...]
    n_e, tr = sel.shape
    neg = -jnp.inf
    sub = lax.broadcasted_iota(I32, (GROUP_SIZE, tr), 0).astype(F32)
    blocks = [sel[GROUP_SIZE * g:GROUP_SIZE * (g + 1), :] for g in range(ROUTER_GROUPS)]
    gscore = []
    for blk in blocks:
        m1 = blk.max(axis=0, keepdims=True)
        pos1 = jnp.min(jnp.where(blk == m1, sub, float(GROUP_SIZE)), axis=0, keepdims=True)
        m2 = jnp.max(jnp.where(sub == pos1, neg, blk), axis=0, keepdims=True)
        gscore.append(m1 + m2)
    masked = []
    for g in range(ROUTER_GROUPS):
        beat = jnp.zeros_like(gscore[g])
        for h in range(ROUTER_GROUPS):
            if h == g:
                continue
            wins = (gscore[h] >= gscore[g]) if h < g else (gscore[h] > gscore[g])
            beat = beat + wins.astype(F32)
        masked.append(jnp.where(beat < float(ROUTER_TOPK_GROUPS), blocks[g], neg))
    cur = jnp.concatenate(masked, axis=0)
    eio = lax.broadcasted_iota(I32, (n_e, tr), 0).astype(F32)
    chosen = jnp.zeros((n_e, tr), F32)
    picks = []
    for _ in range(TOP_K):
        m = cur.max(axis=0, keepdims=True)
        ik = jnp.min(jnp.where(cur == m, eio, float(n_e)), axis=0, keepdims=True)
        hit = eio == ik
        picks.append((ik, hit))
        cur = jnp.where(hit, neg, cur)
        chosen = jnp.where(hit, 1.0, chosen)
    picked = chosen * scores
    wc_ref[:, 0, :] = picked / jnp.sum(picked, axis=0, keepdims=True) * ROUTED_SCALE
    r_ = lax.broadcasted_iota(I32, (tr, tr), 0)
    c_ = lax.broadcasted_iota(I32, (tr, tr), 1)
    chosen_b = chosen.astype(BF16)
    before = _dot(chosen_b, (r_ < c_).astype(BF16)) + carry_ref[:, 0:1]
    for k, (ik, hit) in enumerate(picks):
        idx_ref[k:k + 1, :] = ik.astype(I32)
        rank_ref[k:k + 1, :] = jnp.sum(jnp.where(hit, before, 0.0), axis=0, keepdims=True).astype(I32)
    carry_ref[...] = carry_ref[...] + _dot(chosen_b, jnp.ones((tr, LANES), BF16))
    cnt_ref[...] = carry_ref[...]


def _route(lg_t, bias, layer, tr):
    t = lg_t.shape[1]
    col = lambda i: (0, i)
    return pl.pallas_call(
        _route_kernel, grid=(t // tr,),
        in_specs=[pl.BlockSpec((N_EXPERTS, tr), col),
                  pl.BlockSpec((None, N_EXPERTS, 1), lambda i: (layer, 0, 0))],
        out_specs=[pl.BlockSpec((TOP_K, tr), col), pl.BlockSpec((TOP_K, tr), col),
                   pl.BlockSpec((N_EXPERTS, 1, tr), lambda i: (0, 0, i)),
                   pl.BlockSpec((N_EXPERTS, LANES), lambda i: (0, 0))],
        out_shape=[jax.ShapeDtypeStruct((TOP_K, t), I32), jax.ShapeDtypeStruct((TOP_K, t), I32),
                   jax.ShapeDtypeStruct((N_EXPERTS, 1, t), F32), jax.ShapeDtypeStruct((N_EXPERTS, LANES), F32)],
        scratch_shapes=[pltpu.VMEM((N_EXPERTS, LANES), F32)],
        compiler_params=_cparams(1, 48), name="route",
    )(lg_t, bias)


def _slots_kernel(start_ref, idx_ref, rank_ref, slot_ref):
    n_e = start_ref.shape[0]
    tp = idx_ref.shape[1]
    eio = lax.broadcasted_iota(I32, (n_e, tp), 0)
    start = start_ref[...]
    for k in range(TOP_K):
        first = jnp.sum(jnp.where(eio == idx_ref[k:k + 1, :], start, 0.0), axis=0, keepdims=True)
        slot_ref[k:k + 1, :] = first.astype(I32) + rank_ref[k:k + 1, :]


def _plan_kernel(slot_ref, tok_ref, *, n_tok, blk, unroll):
    first_tok = lax.rem(pl.program_id(0) * blk, n_tok)

    def body(io, carry):
        i0 = io * unroll
        for u in range(unroll):
            tok_ref[slot_ref[i0 + u]] = first_tok + i0 + u
        return carry

    lax.fori_loop(0, blk // unroll, body, 0)


def _plan(start, idx_t, rank_t, tp):
    t = idx_t.shape[1]
    col = lambda i: (0, i)
    slots = pl.pallas_call(
        _slots_kernel, grid=(t // tp,),
        in_specs=[pl.BlockSpec((N_EXPERTS, 1), lambda i: (0, 0)),
                  pl.BlockSpec((TOP_K, tp), col), pl.BlockSpec((TOP_K, tp), col)],
        out_specs=pl.BlockSpec((TOP_K, tp), col),
        out_shape=jax.ShapeDtypeStruct((TOP_K, t), I32),
        compiler_params=_cparams(1, 32), name="slots",
    )(start.astype(F32).reshape(N_EXPERTS, 1), idx_t, rank_t)
    blk = min(t, PLAN_BLOCK)
    assert t % blk == 0
    return pl.pallas_call(
        functools.partial(_plan_kernel, n_tok=t, blk=blk, unroll=16), grid=(t * TOP_K // blk,),
        in_specs=[pl.BlockSpec((blk,), lambda i: (i,), memory_space=pltpu.SMEM)],
        out_specs=pl.BlockSpec(memory_space=pltpu.SMEM),
        out_shape=jax.ShapeDtypeStruct((t * TOP_K,), I32),
        compiler_params=_cparams(1, 32), name="plan",
    )(slots.reshape(t * TOP_K))


def _moe_kernel(start_ref, cnt_ref, tok_ref, wc_ref, x_hbm, wg_ref, wu_ref, wd_ref, out_hbm,
                xv_ref, acc_ref, xt_ref, y_ref, trow_ref, sem, *, n_tok, rows, sub, batch):
    e = pl.program_id(0)
    n_chunks = OUT_CHUNKS
    pair_rows = 2 * n_chunks
    stride = rows + SUBLANES

    @pl.when(e == 0)
    def _():
        cp = pltpu.make_async_copy(x_hbm, xv_ref.at[pl.ds(0, n_tok // 2)], sem)
        cp.start()
        acc_ref[...] = jnp.zeros(acc_ref.shape, F32)
        xt_ref[...] = jnp.zeros(xt_ref.shape, F32)
        xv_ref[n_tok // 2] = jnp.zeros((pair_rows, LANES), BF16)
        cp.wait()

    n = cnt_ref[e]
    first = start_ref[e]
    wg = wg_ref[...].astype(BF16)
    wu = wu_ref[...].astype(BF16)
    wd = wd_ref[...].astype(BF16)

    def block(b, carry):
        base = first + b * rows
        left = jnp.minimum(n - b * rows, rows)

        def gather_row(r, t):
            trow_ref[r] = t
            pair = xv_ref[t >> 1].astype(F32)
            xt_ref[pl.ds(r, n_chunks, stride=stride), :] = jnp.where(
                (t & 1) == 1, pair[n_chunks:pair_rows], pair[0:n_chunks])

        for s0 in range(0, rows, sub):
            @pl.when(s0 + sub <= left)
            def _():
                for r in range(s0, s0 + sub):
                    gather_row(r, tok_ref[base + r])

            @pl.when(jnp.logical_and(s0 < left, left < s0 + sub))
            def _():
                for r in range(s0, s0 + sub):
                    valid = r < left
                    gather_row(r, jnp.where(valid, tok_ref[jnp.where(valid, base + r, 0)], n_tok))

        n_sub = (left + sub - 1) // sub
        for m in range(1, rows // sub + 1):
            @pl.when(n_sub == m)
            def _():
                rm = m * sub
                xb = jnp.concatenate([xt_ref[pl.ds(c * stride, rm), :] for c in range(n_chunks)], axis=1).astype(BF16)
                h = (_silu(_dot(xb, wg)) * _dot(xb, wu)).astype(BF16)
                y = _dot(h, wd)
                for c in range(n_chunks):
                    y_ref[pl.ds(c * stride, rm), :] = y[:, LANES * c:LANES * (c + 1)]
        for s0 in range(0, rows, sub):
            @pl.when(s0 < left)
            def _():
                for r0 in range(s0, s0 + sub, batch):
                    pending = []
                    for r in range(r0, r0 + batch):
                        t = trow_ref[r]
                        wt = wc_ref[0, 0, jnp.minimum(t, n_tok - 1)]
                        pending.append((t, acc_ref[t] + wt * y_ref[pl.ds(r, n_chunks, stride=stride), :]))
                    for t, val in pending:
                        acc_ref[t] = val
        return carry

    lax.fori_loop(0, (n + rows - 1) // rows, block, 0)

    @pl.when(e == pl.num_programs(0) - 1)
    def _():
        cp = pltpu.make_async_copy(acc_ref.at[pl.ds(0, n_tok)], out_hbm, sem)
        cp.start()
        cp.wait()


def _moe(start, cnt, tok_sorted, wc, x_rows, w_gate, w_up, w_down, layer, n_tok, rows, vmem_mib):
    n_chunks = OUT_CHUNKS
    wspec = lambda a, b: pl.BlockSpec((None, None, a, b), lambda e, st, ct: (layer, e, 0, 0))
    grid_spec = pltpu.PrefetchScalarGridSpec(
        num_scalar_prefetch=2, grid=(N_EXPERTS,),
        in_specs=[pl.BlockSpec(memory_space=pltpu.SMEM),
                  pl.BlockSpec((1, 1, n_tok), lambda e, st, ct: (e, 0, 0), memory_space=pltpu.SMEM),
                  pl.BlockSpec(memory_space=pl.ANY),
                  wspec(D_MODEL, EXPERT_FF), wspec(D_MODEL, EXPERT_FF), wspec(EXPERT_FF, D_MODEL)],
        out_specs=pl.BlockSpec(memory_space=pl.ANY),
        scratch_shapes=[pltpu.VMEM((n_tok // 2 + 1, 2 * n_chunks, LANES), BF16),
                        pltpu.VMEM((n_tok + 1, n_chunks, LANES), F32),
                        pltpu.VMEM((n_chunks * (rows + SUBLANES), LANES), F32),
                        pltpu.VMEM((n_chunks * (rows + SUBLANES), LANES), F32),
                        pltpu.SMEM((rows,), I32),
                        pltpu.SemaphoreType.DMA(())])
    return pl.pallas_call(
        functools.partial(_moe_kernel, n_tok=n_tok, rows=rows, sub=64, batch=8),
        grid_spec=grid_spec, out_shape=jax.ShapeDtypeStruct((n_tok, n_chunks, LANES), F32),
        compiler_params=_cparams(1, vmem_mib), name="moe",
    )(start, cnt, tok_sorted, wc, x_rows.reshape(n_tok // 2, 2 * n_chunks, LANES), w_gate, w_up, w_down
      ).reshape(n_tok * n_chunks, LANES)


def _ffn_out_kernel(x1_ref, u2_ref, routed_ref, mod_ref, wgu_ref, wd_ref, lnw_ref, lnb_ref, o_ref):
    gu = _dot(u2_ref[...], wgu_ref[...])
    shared = _dot((_silu(gu[:, 0:SHARED_FF]) * gu[:, SHARED_FF:]).astype(BF16), wd_ref[...])
    tt = x1_ref.shape[0]
    routed = jnp.concatenate([routed_ref[pl.ds(c, tt, stride=OUT_CHUNKS), :] for c in range(OUT_CHUNKS)], axis=1)
    h = DEEPNORM_ALPHA * x1_ref[...] + mod_ref[0, 5:6, :] * (routed + shared)
    o_ref[...] = _layer_norm(h, lnw_ref[...], lnb_ref[...])


def _ffn_out(x1, u2, routed, modp, sh_gu, sh_d, ln_w, ln_b, layer, seq_len, tt):
    t = x1.shape[0]
    tiles_per_seq = seq_len // tt
    row = lambda i: (i, 0)
    lsel = lambda *w: pl.BlockSpec((None,) + w, lambda i: (layer,) + (0,) * len(w))
    ln_spec = pl.BlockSpec((None, None, 1, D_MODEL), lambda i: (layer, 1, 0, 0))
    return pl.pallas_call(
        _ffn_out_kernel, grid=(t // tt,),
        in_specs=[pl.BlockSpec((tt, D_MODEL), row), pl.BlockSpec((tt, D_MODEL), row),
                  pl.BlockSpec((tt * OUT_CHUNKS, LANES), row),
                  _mod_spec(modp, tiles_per_seq),
                  lsel(D_MODEL, 2 * SHARED_FF), lsel(SHARED_FF, D_MODEL), ln_spec, ln_spec],
        out_specs=pl.BlockSpec((tt, D_MODEL), row),
        out_shape=jax.ShapeDtypeStruct((t, D_MODEL), F32),
        compiler_params=_cparams(1, 48), name="ffn_out",
    )(x1, u2, routed, modp, sh_gu, sh_d, ln_w, ln_b)


def _q_perm():
    cols = []
    for j in range(ATT_HEADS // 2):
        for half in range(2):
            h = j + (ATT_HEADS // 2) * half
            cols.extend(range(h * HEAD_DIM, (h + 1) * HEAD_DIM))
    return np.asarray(cols, np.int32)


def _prepare_params(p):
    off = np.concatenate([[0], np.cumsum(IN_SIZES)])
    seg = lambda k: np.arange(off[k], off[k + 1], dtype=np.int32)
    qp = _q_perm()
    cols = np.concatenate([qp, seg(1), seg(2), seg(3), seg(4), seg(5), seg(6), seg(7), seg(9), seg(10), seg(8), seg(11)])
    w_in = jnp.take(p["w_in"], jnp.asarray(cols), axis=2)
    w_in = jnp.pad(w_in, ((0, 0), (0, 0), (0, IN_W_PAD - w_in.shape[2]))).astype(BF16)
    rows = np.concatenate([qp, np.arange(ATT_WIDTH, D_MODEL, dtype=np.int32)])
    w_out = jnp.take(p["w_out"], jnp.asarray(rows), axis=1).astype(BF16)
    rwt = jnp.swapaxes(p["router_w"], 1, 2)
    rwh = rwt.astype(BF16)
    rwl = (rwt - rwh.astype(F32)).astype(BF16)

    def lane_pad(a, fill=0.0):
        a = a.reshape(DEPTH, 1, -1)
        return jnp.pad(a, ((0, 0), (0, 0), (0, LANES - a.shape[2])), constant_values=fill)

    gi = np.arange(512) // HEAD_DIM
    g512 = jnp.asarray(gi[:, None] == gi[None, :], BF16)
    return {
        "w_in": w_in, "w_out": w_out, "rwh": rwh, "rwl": rwl,
        "qnw": jnp.tile(p["q_norm_w"], (1, ATT_HEADS)).reshape(DEPTH, 1, 512),
        "knw": jnp.tile(p["k_norm_w"], (1, ATT_KV_HEADS)).reshape(DEPTH, 1, LANES),
        "g512": g512, "g128": g512[0:LANES, 0:LANES],
        "conv_w": p["ssd_conv_w"], "conv_b": p["ssd_conv_b"].reshape(DEPTH, 1, CONV_CH),
        "dt_bias": lane_pad(p["ssd_dt_bias"]), "a_log": lane_pad(p["ssd_a_log"]),
        "ret_ld": lane_pad(p["ret_log_decay"], -1.0),
        "ret_nw": p["ret_norm_w"].reshape(DEPTH, 1, RET_WIDTH),
        "d_lane": jnp.repeat(p["ssd_d"], HEAD_DIM, axis=1).reshape(DEPTH, 1, SSD_WIDTH),
        "ssd_nw": p["ssd_norm_w"].reshape(DEPTH, 1, SSD_WIDTH),
        "ln_w": p["ln_w"].reshape(DEPTH, 2, 1, D_MODEL), "ln_b": p["ln_b"].reshape(DEPTH, 2, 1, D_MODEL),
        "router_bias": p["router_bias"].reshape(DEPTH, N_EXPERTS, 1),
        "sh_gu": jnp.concatenate([p["sh_w_gate"], p["sh_w_up"]], axis=2).astype(BF16),
        "sh_d": p["sh_w_down"].astype(BF16),
        "exp_w_gate": p["exp_w_gate"], "exp_w_up": p["exp_w_up"], "exp_w_down": p["exp_w_down"],
    }


def _rope_tables(n_tokens):
    t = jnp.arange(n_tokens, dtype=I32)
    rows = (t // GRID_W).astype(F32)
    cols = (t % GRID_W).astype(F32)
    inv = ROPE_THETA ** (-jnp.arange(ROPE_AXIS_FREQS, dtype=F32) / ROPE_AXIS_FREQS)
    ar = rows[:, None] * inv
    ac = cols[:, None] * inv
    cos = jnp.concatenate([jnp.cos(ar), jnp.cos(ar), jnp.cos(ac), jnp.cos(ac)], axis=1)
    sin = jnp.concatenate([-jnp.sin(ar), jnp.sin(ar), -jnp.sin(ac), jnp.sin(ac)], axis=1)
    return jnp.tile(cos, (1, 2)), jnp.tile(sin, (1, 2))


def _pair_states(st, kind):
    b = st.shape[0]
    out = jnp.zeros((b, 2, 2, LANES, LANES), F32)
    for pr in range(2):
        for tt in range(2):
            r0 = HEAD_DIM * (tt if kind == "ret" else pr)
            out = out.at[:, :, pr, r0:r0 + HEAD_DIM, HEAD_DIM * tt:HEAD_DIM * (tt + 1)].set(st[:, :, 2 * pr + tt])
    return out


def _unpair_states(m, kind):
    heads = []
    for pr in range(2):
        for tt in range(2):
            r0 = HEAD_DIM * (tt if kind == "ret" else pr)
            heads.append(m[:, :, pr, r0:r0 + HEAD_DIM, HEAD_DIM * tt:HEAD_DIM * (tt + 1)])
    return jnp.stack(heads, axis=2)


def _trunk_layer(x, modp, pw, layer, n_seq, seq_len, rope_tabs, cache, init, cfg):
    n_tok = x.shape[0]
    ctx = cache is None
    outs = _proj_in(x, modp, pw["w_in"], pw["qnw"], pw["knw"], pw["g512"], rope_tabs, layer, seq_len,
                    cfg["tt"], emit_cache=ctx)
    q, k, v, ret, ssd = outs[:5]
    att = _attention(q, k, v, cache, layer, n_seq, seq_len, cfg["tq"])
    sc = _scan(ret, ssd, pw, layer, init, n_seq, seq_len, cfg["chunk"], emit_state=ctx)
    x1, u2, u2_rows, lg_t = _proj_out(x, att, sc[0], sc[1], modp, pw["w_out"], pw["ln_w"], pw["ln_b"],
                                  pw["rwh"], pw["rwl"], layer, seq_len, cfg["tt"])
    idx_t, rank_t, wc, counts = _route(lg_t, pw["router_bias"], layer, cfg["tr"])
    cnt = counts[:, 0].astype(I32)
    start = jnp.cumsum(cnt) - cnt
    tok_sorted = _plan(start, idx_t, rank_t, cfg["tr"])
    routed = _moe(start, cnt, tok_sorted, wc, u2_rows, pw["exp_w_gate"], pw["exp_w_up"], pw["exp_w_down"],
                  layer, n_tok, cfg["rows"], cfg["moe_vmem"])
    x_new = _ffn_out(x1, u2, routed, modp, pw["sh_gu"], pw["sh_d"], pw["ln_w"], pw["ln_b"], layer, seq_len, cfg["tt"])
    new_ctx = (outs[5], outs[6], sc[2], sc[3]) if ctx else None
    return x_new, new_ctx


def kernel(x_prompt, x_sample, cache_k, cache_v, state_ret, state_ssd, c, c_ctx,
           w_mod, b_mod, w_in, q_norm_w, k_norm_w, ret_log_decay, ret_norm_w,
           ssd_conv_w, ssd_conv_b, ssd_dt_bias, ssd_a_log, ssd_d, ssd_norm_w, w_out,
           ln_w, ln_b, router_w, router_bias, exp_w_gate, exp_w_up, exp_w_down,
           sh_w_gate, sh_w_up, sh_w_down):
    pw = _prepare_params({
        "w_in": w_in, "q_norm_w": q_norm_w, "k_norm_w": k_norm_w, "ret_log_decay": ret_log_decay,
        "ret_norm_w": ret_norm_w, "ssd_conv_w": ssd_conv_w, "ssd_conv_b": ssd_conv_b, "ssd_dt_bias": ssd_dt_bias,
        "ssd_a_log": ssd_a_log, "ssd_d": ssd_d, "ssd_norm_w": ssd_norm_w, "w_out": w_out, "ln_w": ln_w, "ln_b": ln_b,
        "router_w": router_w, "router_bias": router_bias, "exp_w_gate": exp_w_gate, "exp_w_up": exp_w_up,
        "exp_w_down": exp_w_down, "sh_w_gate": sh_w_gate, "sh_w_up": sh_w_up, "sh_w_down": sh_w_down})
    nb, seq, d = x_prompt.shape
    nd, dseq, _ = x_sample.shape
    cond = jnp.concatenate([c_ctx[None], c, jnp.zeros((8 - 1 - nd, d), F32)], axis=0)
    mod = _modulation(cond, w_mod, b_mod)
    rope_tabs = _rope_tables(dseq)
    past = cache_k.shape[2]
    ck = cache_k.reshape(nd, DEPTH, past, LANES)
    cv = cache_v.reshape(nd, DEPTH, past, LANES)
    cfg_ctx = {"tt": 512, "tq": seq, "chunk": SCAN_CHUNK, "tr": 512, "rows": MOE_ROWS, "moe_vmem": MOE_VMEM_MIB}
    cfg_lat = {"tt": 512, "tq": 512, "chunk": SCAN_CHUNK, "tr": 512, "rows": MOE_ROWS, "moe_vmem": MOE_VMEM_MIB}

    xc = x_prompt.reshape(nb * seq, d)
    xs = x_sample.reshape(nd * dseq, d)
    ks, vs, srs, sss = [], [], [], []
    for l in range(DEPTH):
        mod_ctx = mod[l, 0:1].reshape(1, 6, d)
        mod_lat = mod[l, 1:1 + nd].reshape(nd, 6, d)
        xc, (k_l, v_l, sr_l, ss_l) = _trunk_layer(xc, mod_ctx, pw, l, nb, seq, None, None, None, cfg_ctx)
        ks.append(k_l.reshape(nb, seq, ATT_KV_HEADS, HEAD_DIM))
        vs.append(v_l.reshape(nb, seq, ATT_KV_HEADS, HEAD_DIM))
        srs.append(_unpair_states(sr_l, "ret"))
        sss.append(_unpair_states(ss_l, "ssd"))
        init = (_pair_states(state_ret[:, l], "ret"), _pair_states(state_ssd[:, l], "ssd"))
        xs, _ = _trunk_layer(xs, mod_lat, pw, l, nd, dseq, rope_tabs, (ck, cv), init, cfg_lat)
    return (xc.reshape(nb, seq, d), xs.reshape(nd, dseq, d), jnp.stack(ks, axis=1), jnp.stack(vs, axis=1),
            jnp.stack(srs, axis=1), jnp.stack(sss, axis=1))
```

```python
import functools

import numpy as np
import jax
import jax.numpy as jnp
from jax import lax
from jax.experimental import pallas as pl
from jax.experimental.pallas import tpu as pltpu

F32 = jnp.float32
BF16 = jnp.bfloat16
I32 = jnp.int32
HIGHEST = lax.Precision.HIGHEST
NT_DIMS = (((1,), (1,)), ((), ()))

D_MODEL = 1024
DEPTH = 4
GRID_W = 64
HEAD_DIM = 64
ATT_HEADS = 8
ATT_KV_HEADS = 2
ATT_WIDTH = ATT_HEADS * HEAD_DIM
ROPE_THETA = 10000.0
ROPE_AXIS_FREQS = HEAD_DIM // 4
RET_HEADS = 4
RET_DK = 64
RET_WIDTH = 256
SSD_HEADS = 4
SSD_WIDTH = 256
SSD_STATE = 64
N_EXPERTS = 256
TOP_K = 8
ROUTER_GROUPS = 8
ROUTER_TOPK_GROUPS = 4
GROUP_SIZE = N_EXPERTS // ROUTER_GROUPS
EXPERT_FF = 256
SHARED_FF = 256
ROUTED_SCALE = 2.5
DEEPNORM_ALPHA = (2 * DEPTH) ** 0.25
LOG2_E = 1.4426950408889634
IN_SIZES = (512, 128, 128, 256, 256, 256, 256, 256, 256, 128, 128, 8)

LANES = 128
SUBLANES = 8
MIB = 1024 * 1024

QKV_W = 768
RET_OFF, RET_W = 768, 1024
SSD_OFF, SSD_W = 1792, 896
IN_W_PAD = 2688
CONV_CH = 512

OUT_CHUNKS = D_MODEL // LANES
PLAN_BLOCK = 4096
SCAN_CHUNK = 256
MOE_ROWS = 320
MOE_VMEM_MIB = 62


def _cparams(n_grid, vmem_mib):
    return pltpu.CompilerParams(dimension_semantics=("arbitrary",) * n_grid,
                                vmem_limit_bytes=vmem_mib * MIB)


def _dot(a, b):
    return jnp.dot(a, b, preferred_element_type=F32)


def _dot_nt(a, b):
    return lax.dot_general(a, b, NT_DIMS, preferred_element_type=F32)


def _silu(x):
    return x * jax.nn.sigmoid(x)


def _softplus(x):
    return jnp.maximum(x, 0.0) + jnp.log1p(jnp.exp(-jnp.abs(x)))


def _mod_spec(modp, tiles_per_seq):
    if modp.shape[0] == 1:
        return pl.BlockSpec((1, 6, D_MODEL), lambda i: (0, 0, 0))
    return pl.BlockSpec((1, 6, D_MODEL), lambda i: (i // tiles_per_seq, 0, 0))


def _group_sum(x, g):
    hi = x.astype(BF16)
    lo = (x - hi.astype(F32)).astype(BF16)
    return _dot(hi, g) + _dot(lo, g)


def _swap16(x):
    n = x.shape[1]
    lane = lax.broadcasted_iota(I32, x.shape, 1)
    return jnp.where((lane & 16) == 0, pltpu.roll(x, n - 16, 1), pltpu.roll(x, 16, 1))


def _layer_norm(h, w, b):
    mu = jnp.mean(h, axis=-1, keepdims=True)
    d = h - mu
    var = jnp.mean(d * d, axis=-1, keepdims=True)
    return d * lax.rsqrt(var + 1e-5) * w + b


def _mod_kernel(cond_ref, w_ref, b_ref, o_ref):
    c = cond_ref[...]
    o_ref[0] = jnp.dot(_silu(c), w_ref[0], precision=HIGHEST, preferred_element_type=F32) + b_ref[0]


def _modulation(cond8, w_mod, b_mod):
    tn = 1536
    n_col = w_mod.shape[2] // tn
    return pl.pallas_call(
        _mod_kernel,
        grid=(DEPTH, n_col),
        in_specs=[pl.BlockSpec((8, D_MODEL), lambda l, j: (0, 0)),
                  pl.BlockSpec((1, D_MODEL, tn), lambda l, j: (l, 0, j)),
                  pl.BlockSpec((1, 1, tn), lambda l, j: (l, 0, j))],
        out_specs=pl.BlockSpec((1, 8, tn), lambda l, j: (l, 0, j)),
        out_shape=jax.ShapeDtypeStruct((DEPTH, 8, w_mod.shape[2]), F32),
        compiler_params=_cparams(2, 32),
        name="modulation",
    )(cond8, w_mod, b_mod.reshape(DEPTH, 1, -1))


def _proj_in_kernel(*refs, rope, emit_cache):
    x_ref, mod_ref, w_ref, qnw_ref, knw_ref, g_ref = refs[:6]
    pos = 6
    if rope:
        cos_ref, sin_ref = refs[pos:pos + 2]
        pos += 2
    q_ref, k_ref, v_ref, ret_ref, ssd_ref = refs[pos:pos + 5]
    pos += 5
    u = x_ref[...] * (1.0 + mod_ref[0, 1:2, :]) + mod_ref[0, 0:1, :]
    ub = u.astype(BF16)
    q = _dot(ub, w_ref[:, 0:512])
    k = _dot(ub, w_ref[:, 512:640])
    v = _dot(ub, w_ref[:, 640:768])
    g = g_ref[...]
    inv_hd = 1.0 / HEAD_DIM
    qn = q * lax.rsqrt(_group_sum(q * q, g) * inv_hd + 1e-6) * qnw_ref[...]
    kn = k * lax.rsqrt(_group_sum(k * k, g[0:LANES, 0:LANES]) * inv_hd + 1e-6) * knw_ref[...]
    if emit_cache:
        kf_ref, vf_ref = refs[pos:pos + 2]
        kf_ref[...] = kn
        vf_ref[...] = v
    if rope:
        c = cos_ref[...]
        s = sin_ref[...]
        kn = kn * c + _swap16(kn) * s
        qn = qn * jnp.concatenate([c] * 4, axis=1) + _swap16(qn) * jnp.concatenate([s] * 4, axis=1)
    q_ref[...] = (qn * (HEAD_DIM ** -0.5 * LOG2_E)).astype(BF16)
    k_ref[...] = kn.astype(BF16)
    v_ref[...] = v.astype(BF16)
    ret_ref[...] = _dot(ub, w_ref[:, RET_OFF:RET_OFF + RET_W])
    ssd_ref[...] = _dot(ub, w_ref[:, SSD_OFF:SSD_OFF + SSD_W])


def _proj_in(x, modp, w_in_p, qnw, knw, g512, rope_tabs, layer, seq_len, tt, emit_cache):
    t = x.shape[0]
    tiles_per_seq = seq_len // tt
    rope = rope_tabs is not None
    row = lambda i: (i, 0)
    const2 = lambda i: (0, 0)
    in_specs = [pl.BlockSpec((tt, D_MODEL), row),
                _mod_spec(modp, tiles_per_seq),
                pl.BlockSpec((None, D_MODEL, IN_W_PAD), lambda i: (layer, 0, 0)),
                pl.BlockSpec((None, 1, 512), lambda i: (layer, 0, 0)),
                pl.BlockSpec((None, 1, LANES), lambda i: (layer, 0, 0)),
                pl.BlockSpec((512, 512), const2)]
    args = [x, modp, w_in_p, qnw, knw, g512]
    if rope:
        tab = pl.BlockSpec((tt, LANES), lambda i: (i % tiles_per_seq, 0))
        in_specs += [tab, tab]
        args += list(rope_tabs)
    out_shape = [jax.ShapeDtypeStruct((t, 512), BF16), jax.ShapeDtypeStruct((t, LANES), BF16),
                 jax.ShapeDtypeStruct((t, LANES), BF16), jax.ShapeDtypeStruct((t, RET_W), F32),
                 jax.ShapeDtypeStruct((t, SSD_W), F32)]
    out_specs = [pl.BlockSpec((tt, 512), row), pl.BlockSpec((tt, LANES), row), pl.BlockSpec((tt, LANES), row),
                 pl.BlockSpec((tt, RET_W), row), pl.BlockSpec((tt, SSD_W), row)]
    if emit_cache:
        out_shape += [jax.ShapeDtypeStruct((t, LANES), F32)] * 2
        out_specs += [pl.BlockSpec((tt, LANES), row)] * 2
    return pl.pallas_call(
        functools.partial(_proj_in_kernel, rope=rope, emit_cache=emit_cache),
        grid=(t // tt,), in_specs=in_specs, out_specs=out_specs, out_shape=out_shape,
        compiler_params=_cparams(1, 48), name="proj_in",
    )(*args)


def _attn_kernel(*refs, with_cache, group):
    q_ref, k_ref, v_ref = refs[:3]
    n_in = 5 if with_cache else 3
    o_ref = refs[n_in]
    s_refs = refs[n_in + 1:n_in + 3]
    p_refs = refs[n_in + 3:n_in + 5]
    tq = q_ref.shape[0]
    lane = lax.broadcasted_iota(I32, (1, LANES), 1)
    first = lane < HEAD_DIM
    parts = [(k_ref[...], v_ref[...])]
    if with_cache:
        parts.append((refs[3][...].astype(BF16), refs[4][...].astype(BF16)))
    bounds = np.cumsum([0] + [kk.shape[0] for kk, _ in parts])
    zero = jnp.zeros((), BF16)
    kmask = [[jnp.where(first, kk, zero), jnp.where(first, zero, kk)] for kk, _ in parts]
    one = jnp.ones((), BF16)
    vaug = [[jnp.where(first, vv, one), jnp.where(first, one, vv)] for _, vv in parts]
    head = 0
    for j in range(ATT_HEADS // 2):
        qb = q_ref[:, LANES * j:LANES * (j + 1)]
        outs = []
        for g in range(ATT_KV_HEADS):
            s_ref, p_ref = s_refs[head % 2], p_refs[head % 2]
            head += 1
            for p, km in enumerate(kmask):
                s_ref[:, bounds[p]:bounds[p + 1]] = _dot_nt(qb, km[g])
            for r0 in range(0, tq, group):
                s = s_ref[r0:r0 + group, :]
                p_ref[r0:r0 + group, :] = jnp.exp2(s - s.max(axis=-1, keepdims=True)).astype(BF16)
            acc = None
            for p, va in enumerate(vaug):
                o = _dot(p_ref[:, bounds[p]:bounds[p + 1]], va[g])
                acc = o if acc is None else acc + o
            outs.append(acc / pltpu.roll(acc, HEAD_DIM, 1))
        o_ref[:, LANES * j:LANES * (j + 1)] = jnp.where(first, outs[0], outs[1]).astype(BF16)


def _attention(q, k, v, cache, layer, n_seq, seq_len, tq):
    t = q.shape[0]
    nq = seq_len // tq
    in_specs = [pl.BlockSpec((tq, 512), lambda b, i: (b * nq + i, 0)),
                pl.BlockSpec((seq_len, LANES), lambda b, i: (b, 0)),
                pl.BlockSpec((seq_len, LANES), lambda b, i: (b, 0))]
    args = [q, k, v]
    n_keys = seq_len
    if cache is not None:
        past = cache[0].shape[2]
        n_keys += past
        cspec = pl.BlockSpec((None, None, past, LANES), lambda b, i: (b, layer, 0, 0))
        in_specs += [cspec, cspec]
        args += list(cache)
    return pl.pallas_call(
        functools.partial(_attn_kernel, with_cache=cache is not None, group=2 * SUBLANES),
        grid=(n_seq, nq), in_specs=in_specs,
        out_specs=pl.BlockSpec((tq, 512), lambda b, i: (b * nq + i, 0)),
        out_shape=jax.ShapeDtypeStruct((t, 512), BF16),
        scratch_shapes=[pltpu.VMEM((tq, n_keys), F32)] * 2 + [pltpu.VMEM((tq, n_keys), BF16)] * 2,
        compiler_params=_cparams(2, 56), name="attention",
    )(*args)


def _scan_kernel(*refs, n, chunk, has_init, emit_state):
    (ret_ref, ssd_ref, hp_ref, hn_ref, cw_ref, cb_ref, dtb_ref, alog_ref, rld_ref,
     rnw_ref, dl_ref, snw_ref, g_ref) = refs[:13]
    pos = 13
    if has_init:
        s0r_ref, s0s_ref = refs[pos:pos + 2]
        pos += 2
    reto_ref, ssdo_ref = refs[pos:pos + 2]
    pos += 2
    if emit_state:
        str_ref, sts_ref = refs[pos:pos + 2]
        pos += 2
    of_ref, st_ref, rdec_ref = refs[pos:pos + 3]

    C = chunk
    s = pl.program_id(1)
    bwd = s >= n
    c = jnp.where(bwd, 2 * n - 1 - s, s)

    def init_state(direction):
        if has_init:
            st_ref[0:2] = s0s_ref[0, direction]
            st_ref[2:4] = s0r_ref[0, direction]
        else:
            st_ref[...] = jnp.zeros(st_ref.shape, F32)

    @pl.when(s == 0)
    def _():
        init_state(0)

    @pl.when(s == n)
    def _():
        init_state(1)

    lane = lax.broadcasted_iota(I32, (1, LANES), 1)
    first = lane < HEAD_DIM
    ret = ret_ref[...]
    ssd = ssd_ref[...]

    xin = ssd[:, 0:CONV_CH]
    rowi = lax.broadcasted_iota(I32, (C, 1), 0)
    prev_row = jnp.where(c == 0, 0.0, hp_ref[SUBLANES - 1:SUBLANES, :])
    next_row = jnp.where(c == n - 1, 0.0, hn_ref[0:1, :])
    xp = jnp.where(rowi == 0, prev_row, pltpu.roll(xin, 1, 0))
    xn = jnp.where(rowi == C - 1, next_row, pltpu.roll(xin, C - 1, 0))
    xbc = _silu(xp * cw_ref[0:1, :] + xin * cw_ref[1:2, :] + xn * cw_ref[2:3, :] + cb_ref[...])
    sx = xbc[:, 0:256]
    sb = xbc[:, 256:384]
    sc = xbc[:, 384:512]

    def this_dir(a):
        return jnp.where(bwd, pltpu.roll(a, LANES - SSD_HEADS, 1), a)

    dt_all = _softplus(ssd[:, 768:896] + dtb_ref[...])
    la_ssd = this_dir(dt_all * (-jnp.exp(alog_ref[...])))
    dt_dir = this_dir(dt_all)
    lg = jnp.broadcast_to(jnp.log1p(-jnp.exp(rld_ref[...])), (SUBLANES, LANES))
    lg_dir = pltpu.roll(this_dir(lg), SSD_HEADS, 1)[0:1, :]
    la = jnp.where(lane < 4, la_ssd, jnp.where(lane < 8, lg_dir, 0.0))
    ii = lax.broadcasted_iota(I32, (C, C), 0)
    jj = lax.broadcasted_iota(I32, (C, C), 1)
    msk = jnp.where(bwd, jj - ii, ii - jj) >= 0
    cum = jnp.dot(msk.astype(F32), la, precision=HIGHEST, preferred_element_type=F32)
    last = jnp.where(bwd, cum[0:1, :], cum[C - 1:C, :])
    acol = jnp.where(lane < 8, cum, jnp.where(lane < 12, pltpu.roll(dt_dir, 8, 1), 0.0))
    arow = acol.T

    def decay(h, scale):
        diff = cum[:, h:h + 1] - arow[h:h + 1, :]
        return jnp.where(msk, jnp.exp(jnp.where(msk, diff, 0.0)), 0.0) * scale

    def lanes2(a0, a1):
        return jnp.where(first, a0, a1)

    @pl.when(jnp.logical_or(s == 0, s == n))
    def _():
        for h in range(RET_HEADS):
            rdec_ref[h] = decay(SSD_HEADS + h, RET_DK ** -0.5)

    ii2 = lax.broadcasted_iota(I32, (LANES, LANES), 0)
    jj2 = lax.broadcasted_iota(I32, (LANES, LANES), 1)
    blockdiag = ((ii2 < HEAD_DIM) == (jj2 < HEAD_DIM)).astype(F32)

    o_pairs = []
    for kind in ("ssd", "ret"):
        for p in range(2):
            slot = p if kind == "ssd" else 2 + p
            heads = (2 * p, 2 * p + 1)
            hl = [h if kind == "ssd" else 4 + h for h in heads]
            if kind == "ssd":
                grp = first if p == 0 else jnp.logical_not(first)
                qm = jnp.where(grp, sc, 0.0).astype(BF16)
                km = jnp.where(grp, sb, 0.0)
                v128 = sx[:, LANES * p:LANES * (p + 1)]
                sg = _dot_nt(qm, sb.astype(BF16))
                scores = [sg, sg]
                dec = [decay(h, arow[8 + h:9 + h, :]) for h in heads]
                cs = [acol[:, 8 + h:9 + h] for h in heads]
                q_inter = qm
            else:
                q128 = ret[:, LANES * p:LANES * (p + 1)]
                km = ret[:, 256 + LANES * p:256 + LANES * (p + 1)]
                v128 = ret[:, 512 + LANES * p:512 + LANES * (p + 1)]
                kb = km.astype(BF16)
                scores = [_dot_nt(jnp.where(first, q128, 0.0).astype(BF16), kb),
                          _dot_nt(jnp.where(first, 0.0, q128).astype(BF16), kb)]
                dec = [rdec_ref[h] for h in heads]
                cs = [RET_DK ** -0.5] * 2
                q_inter = q128.astype(BF16)
            vb = v128.astype(BF16)
            intra = [_dot((scores[t] * dec[t]).astype(BF16), vb) for t in range(2)]
            state = st_ref[slot]
            ecum = lanes2(jnp.exp(cum[:, hl[0]:hl[0] + 1]), jnp.exp(cum[:, hl[1]:hl[1] + 1]))
            o_pairs.append(lanes2(intra[0], intra[1]) + _dot(q_inter, state.astype(BF16)) * ecum)
            wv = lanes2(jnp.exp(last[:, hl[0]:hl[0] + 1] - cum[:, hl[0]:hl[0] + 1]) * cs[0],
                        jnp.exp(last[:, hl[1]:hl[1] + 1] - cum[:, hl[1]:hl[1] + 1]) * cs[1])
            snew = _dot(km.T.astype(BF16), (v128 * wv).astype(BF16))
            if kind == "ret":
                snew = snew * blockdiag
            elast = lanes2(jnp.exp(last[:, hl[0]:hl[0] + 1]), jnp.exp(last[:, hl[1]:hl[1] + 1]))
            st_ref[slot] = elast * state + snew
    o_all = jnp.concatenate(o_pairs, axis=1)
    rows = pl.ds(pl.multiple_of(c * C, C), C)

    @pl.when(jnp.logical_not(bwd))
    def _():
        of_ref[rows, :] = o_all

    @pl.when(bwd)
    def _():
        tot = of_ref[rows, :] + o_all
        g = g_ref[...]
        inv = 1.0 / HEAD_DIM
        for p in range(2):
            o = tot[:, 256 + LANES * p:256 + LANES * (p + 1)]
            d = o - _group_sum(o, g) * inv
            var = _group_sum(d * d, g) * inv
            rn = d * lax.rsqrt(var + 1e-5) * rnw_ref[:, LANES * p:LANES * (p + 1)]
            gate = ret[:, 768 + LANES * p:768 + LANES * (p + 1)]
            reto_ref[:, LANES * p:LANES * (p + 1)] = (rn * _silu(gate)).astype(BF16)
        y = (tot[:, 0:256] + dl_ref[...] * sx) * _silu(ssd[:, 512:768])
        ms = jnp.mean(y * y, axis=-1, keepdims=True)
        ssdo_ref[...] = (y * lax.rsqrt(ms + 1e-6) * snw_ref[...]).astype(BF16)

    if emit_state:
        @pl.when(s == n - 1)
        def _():
            sts_ref[0, 0] = st_ref[0:2]
            str_ref[0, 0] = st_ref[2:4]

        @pl.when(s == 2 * n - 1)
        def _():
            sts_ref[0, 1] = st_ref[0:2]
            str_ref[0, 1] = st_ref[2:4]


def _scan(ret, ssd, pw, layer, init, n_seq, seq_len, chunk, emit_state):
    t = ret.shape[0]
    n = seq_len // chunk
    rows8 = t // SUBLANES

    def cidx(s):
        return jnp.where(s >= n, 2 * n - 1 - s, s)

    def oidx(s):
        return jnp.where(s >= n, 2 * n - 1 - s, n - 1)

    par = lambda w: pl.BlockSpec((None,) + w, lambda b, s: (layer,) + (0,) * len(w))
    in_specs = [pl.BlockSpec((chunk, RET_W), lambda b, s: (b * n + cidx(s), 0)),
                pl.BlockSpec((chunk, SSD_W), lambda b, s: (b * n + cidx(s), 0)),
                pl.BlockSpec((SUBLANES, CONV_CH),
                             lambda b, s: (jnp.maximum((b * seq_len + cidx(s) * chunk) // SUBLANES - 1, 0), 0)),
                pl.BlockSpec((SUBLANES, CONV_CH),
                             lambda b, s: (jnp.minimum((b * seq_len + (cidx(s) + 1) * chunk) // SUBLANES, rows8 - 1), 0)),
                par((3, CONV_CH)), par((1, CONV_CH)), par((1, LANES)), par((1, LANES)), par((1, LANES)),
                par((1, RET_WIDTH)), par((1, SSD_WIDTH)), par((1, SSD_WIDTH)),
                pl.BlockSpec((LANES, LANES), lambda b, s: (0, 0))]
    args = [ret, ssd, ssd, ssd, pw["conv_w"], pw["conv_b"], pw["dt_bias"], pw["a_log"], pw["ret_ld"],
            pw["ret_nw"], pw["d_lane"], pw["ssd_nw"], pw["g128"]]
    st_spec = pl.BlockSpec((1, 2, 2, LANES, LANES), lambda b, s: (b, 0, 0, 0, 0))
    if init is not None:
        in_specs += [st_spec, st_spec]
        args += list(init)
    out_specs = [pl.BlockSpec((chunk, RET_WIDTH), lambda b, s: (b * n + oidx(s), 0)),
                 pl.BlockSpec((chunk, SSD_WIDTH), lambda b, s: (b * n + oidx(s), 0))]
    out_shape = [jax.ShapeDtypeStruct((t, RET_WIDTH), BF16), jax.ShapeDtypeStruct((t, SSD_WIDTH), BF16)]
    if emit_state:
        out_specs += [st_spec, st_spec]
        out_shape += [jax.ShapeDtypeStruct((n_seq, 2, 2, LANES, LANES), F32)] * 2
    return pl.pallas_call(
        functools.partial(_scan_kernel, n=n, chunk=chunk, has_init=init is not None, emit_state=emit_state),
        grid=(n_seq, 2 * n), in_specs=in_specs, out_specs=out_specs, out_shape=out_shape,
        scratch_shapes=[pltpu.VMEM((seq_len, 512), F32), pltpu.VMEM((4, LANES, LANES), F32),
                        pltpu.VMEM((RET_HEADS, chunk, chunk), F32)],
        compiler_params=_cparams(2, 48), name="scan",
    )(*args)


def _proj_out_kernel(x_ref, att_ref, ro_ref, so_ref, mod_ref, wo_ref, lnw_ref, lnb_ref, rwh_ref, rwl_ref,
                     x1_ref, u2_ref, u2c_ref, lg_ref, rows_ref):
    mix = (_dot(att_ref[...], wo_ref[0:512, :]) + _dot(ro_ref[...], wo_ref[512:768, :])
           + _dot(so_ref[...], wo_ref[768:1024, :]))
    h = DEEPNORM_ALPHA * x_ref[...] + mod_ref[0, 2:3, :] * mix
    x1 = _layer_norm(h, lnw_ref[...], lnb_ref[...])
    x1_ref[...] = x1
    u2 = x1 * (1.0 + mod_ref[0, 4:5, :]) + mod_ref[0, 3:4, :]
    hi = u2.astype(BF16)
    u2_ref[...] = hi
    for c in range(OUT_CHUNKS):
        rows_ref[pl.ds(c, u2.shape[0], stride=OUT_CHUNKS), :] = u2[:, LANES * c:LANES * (c + 1)]
    u2c_ref[...] = rows_ref[...].astype(BF16)
    lo = (u2 - hi.astype(F32)).astype(BF16)
    rwh = rwh_ref[...]
    lg_ref[...] = _dot_nt(rwh, hi) + _dot_nt(rwh, lo) + _dot_nt(rwl_ref[...], hi)


def _proj_out(x, att, ro, so, modp, w_out_p, ln_w, ln_b, rwh, rwl, layer, seq_len, tt):
    t = x.shape[0]
    tiles_per_seq = seq_len // tt
    row = lambda i: (i, 0)
    lsel = lambda *w: pl.BlockSpec((None,) + w, lambda i: (layer,) + (0,) * len(w))
    in_specs = [pl.BlockSpec((tt, D_MODEL), row), pl.BlockSpec((tt, 512), row),
                pl.BlockSpec((tt, RET_WIDTH), row), pl.BlockSpec((tt, SSD_WIDTH), row),
                _mod_spec(modp, tiles_per_seq),
                lsel(D_MODEL, D_MODEL),
                pl.BlockSpec((None, None, 1, D_MODEL), lambda i: (layer, 0, 0, 0)),
                pl.BlockSpec((None, None, 1, D_MODEL), lambda i: (layer, 0, 0, 0)),
                lsel(N_EXPERTS, D_MODEL), lsel(N_EXPERTS, D_MODEL)]
    out_shape = [jax.ShapeDtypeStruct((t, D_MODEL), F32), jax.ShapeDtypeStruct((t, D_MODEL), BF16),
                 jax.ShapeDtypeStruct((t * OUT_CHUNKS, LANES), BF16), jax.ShapeDtypeStruct((N_EXPERTS, t), F32)]
    out_specs = [pl.BlockSpec((tt, D_MODEL), row), pl.BlockSpec((tt, D_MODEL), row),
                 pl.BlockSpec((tt * OUT_CHUNKS, LANES), row), pl.BlockSpec((N_EXPERTS, tt), lambda i: (0, i))]
    return pl.pallas_call(
        _proj_out_kernel, grid=(t // tt,), in_specs=in_specs, out_specs=out_specs, out_shape=out_shape,
        scratch_shapes=[pltpu.VMEM((tt * OUT_CHUNKS, LANES), F32)],
        compiler_params=_cparams(1, 48), name="proj_out",
    )(x, att, ro, so, modp, w_out_p, ln_w, ln_b, rwh, rwl)


def _route_kernel(lg_ref, bias_ref, idx_ref, rank_ref, wc_ref, cnt_ref, carry_ref):
    i = pl.program_id(0)

    @pl.when(i == 0)
    def _():
        carry_ref[...] = jnp.zeros(carry_ref.shape, F32)

    scores = jax.nn.sigmoid(lg_ref[...])
    sel = scores + bias_ref[...]
    n_e, tr = sel.shape
    neg = -jnp.inf
    sub = lax.broadcasted_iota(I32, (GROUP_SIZE, tr), 0).astype(F32)
    blocks = [sel[GROUP_SIZE * g:GROUP_SIZE * (g + 1), :] for g in range(ROUTER_GROUPS)]
    gscore = []
    for blk in blocks:
        m1 = blk.max(axis=0, keepdims=True)
        pos1 = jnp.min(jnp.where(blk == m1, sub, float(GROUP_SIZE)), axis=0, keepdims=True)
        m2 = jnp.max(jnp.where(sub == pos1, neg, blk), axis=0, keepdims=True)
        gscore.append(m1 + m2)
    masked = []
    for g in range(ROUTER_GROUPS):
        beat = jnp.zeros_like(gscore[g])
        for h in range(ROUTER_GROUPS):
            if h == g:
                continue
            wins = (gscore[h] >= gscore[g]) if h < g else (gscore[h] > gscore[g])
            beat = beat + wins.astype(F32)
        masked.append(jnp.where(beat < float(ROUTER_TOPK_GROUPS), blocks[g], neg))
    cur = jnp.concatenate(masked, axis=0)
    eio = lax.broadcasted_iota(I32, (n_e, tr), 0).astype(F32)
    chosen = jnp.zeros((n_e, tr), F32)
    picks = []
    for _ in range(TOP_K):
        m = cur.max(axis=0, keepdims=True)
        ik = jnp.min(jnp.where(cur == m, eio, float(n_e)), axis=0, keepdims=True)
        hit = eio == ik
        picks.append((ik, hit))
        cur = jnp.where(hit, neg, cur)
        chosen = jnp.where(hit, 1.0, chosen)
    picked = chosen * scores
    wc_ref[:, 0, :] = picked / jnp.sum(picked, axis=0, keepdims=True) * ROUTED_SCALE
    r_ = lax.broadcasted_iota(I32, (tr, tr), 0)
    c_ = lax.broadcasted_iota(I32, (tr, tr), 1)
    chosen_b = chosen.astype(BF16)
    before = _dot(chosen_b, (r_ < c_).astype(BF16)) + carry_ref[:, 0:1]
    for k, (ik, hit) in enumerate(picks):
        idx_ref[k:k + 1, :] = ik.astype(I32)
        rank_ref[k:k + 1, :] = jnp.sum(jnp.where(hit, before, 0.0), axis=0, keepdims=True).astype(I32)
    carry_ref[...] = carry_ref[...] + _dot(chosen_b, jnp.ones((tr, LANES), BF16))
    cnt_ref[...] = carry_ref[...]


def _route(lg_t, bias, layer, tr):
    t = lg_t.shape[1]
    col = lambda i: (0, i)
    return pl.pallas_call(
        _route_kernel, grid=(t // tr,),
        in_specs=[pl.BlockSpec((N_EXPERTS, tr), col),
                  pl.BlockSpec((None, N_EXPERTS, 1), lambda i: (layer, 0, 0))],
        out_specs=[pl.BlockSpec((TOP_K, tr), col), pl.BlockSpec((TOP_K, tr), col),
                   pl.BlockSpec((N_EXPERTS, 1, tr), lambda i: (0, 0, i)),
                   pl.BlockSpec((N_EXPERTS, LANES), lambda i: (0, 0))],
        out_shape=[jax.ShapeDtypeStruct((TOP_K, t), I32), jax.ShapeDtypeStruct((TOP_K, t), I32),
                   jax.ShapeDtypeStruct((N_EXPERTS, 1, t), F32), jax.ShapeDtypeStruct((N_EXPERTS, LANES), F32)],
        scratch_shapes=[pltpu.VMEM((N_EXPERTS, LANES), F32)],
        compiler_params=_cparams(1, 48), name="route",
    )(lg_t, bias)


def _slots_kernel(start_ref, idx_ref, rank_ref, slot_ref):
    n_e = start_ref.shape[0]
    tp = idx_ref.shape[1]
    eio = lax.broadcasted_iota(I32, (n_e, tp), 0)
    start = start_ref[...]
    for k in range(TOP_K):
        first = jnp.sum(jnp.where(eio == idx_ref[k:k + 1, :], start, 0.0), axis=0, keepdims=True)
        slot_ref[k:k + 1, :] = first.astype(I32) + rank_ref[k:k + 1, :]


def _plan_kernel(slot_ref, tok_ref, *, n_tok, blk, unroll):
    first_tok = lax.rem(pl.program_id(0) * blk, n_tok)

    def body(io, carry):
        i0 = io * unroll
        for u in range(unroll):
            tok_ref[slot_ref[i0 + u]] = first_tok + i0 + u
        return carry

    lax.fori_loop(0, blk // unroll, body, 0)


def _plan(start, idx_t, rank_t, tp):
    t = idx_t.shape[1]
    col = lambda i: (0, i)
    slots = pl.pallas_call(
        _slots_kernel, grid=(t // tp,),
        in_specs=[pl.BlockSpec((N_EXPERTS, 1), lambda i: (0, 0)),
                  pl.BlockSpec((TOP_K, tp), col), pl.BlockSpec((TOP_K, tp), col)],
        out_specs=pl.BlockSpec((TOP_K, tp), col),
        out_shape=jax.ShapeDtypeStruct((TOP_K, t), I32),
        compiler_params=_cparams(1, 32), name="slots",
    )(start.astype(F32).reshape(N_EXPERTS, 1), idx_t, rank_t)
    blk = min(t, PLAN_BLOCK)
    assert t % blk == 0
    return pl.pallas_call(
        functools.partial(_plan_kernel, n_tok=t, blk=blk, unroll=16), grid=(t * TOP_K // blk,),
        in_specs=[pl.BlockSpec((blk,), lambda i: (i,), memory_space=pltpu.SMEM)],
        out_specs=pl.BlockSpec(memory_space=pltpu.SMEM),
        out_shape=jax.ShapeDtypeStruct((t * TOP_K,), I32),
        compiler_params=_cparams(1, 32), name="plan",
    )(slots.reshape(t * TOP_K))


def _moe_kernel(start_ref, cnt_ref, tok_ref, wc_ref, x_hbm, wg_ref, wu_ref, wd_ref, out_hbm,
                xv_ref, acc_ref, xt_ref, y_ref, trow_ref, sem, *, n_tok, rows, sub, batch):
    e = pl.program_id(0)
    n_chunks = OUT_CHUNKS
    pair_rows = 2 * n_chunks
    stride = rows + SUBLANES

    @pl.when(e == 0)
    def _():
        cp = pltpu.make_async_copy(x_hbm, xv_ref.at[pl.ds(0, n_tok // 2)], sem)
        cp.start()
        acc_ref[...] = jnp.zeros(acc_ref.shape, F32)
        xt_ref[...] = jnp.zeros(xt_ref.shape, F32)
        xv_ref[n_tok // 2] = jnp.zeros((pair_rows, LANES), BF16)
        cp.wait()

    n = cnt_ref[e]
    first = start_ref[e]
    wg = wg_ref[...].astype(BF16)
    wu = wu_ref[...].astype(BF16)
    wd = wd_ref[...].astype(BF16)

    def block(b, carry):
        base = first + b * rows
        left = jnp.minimum(n - b * rows, rows)

        def gather_row(r, t):
            trow_ref[r] = t
            pair = xv_ref[t >> 1].astype(F32)
            xt_ref[pl.ds(r, n_chunks, stride=stride), :] = jnp.where(
                (t & 1) == 1, pair[n_chunks:pair_rows], pair[0:n_chunks])

        for s0 in range(0, rows, sub):
            @pl.when(s0 + sub <= left)
            def _():
                for r in range(s0, s0 + sub):
                    gather_row(r, tok_ref[base + r])

            @pl.when(jnp.logical_and(s0 < left, left < s0 + sub))
            def _():
                for r in range(s0, s0 + sub):
                    valid = r < left
                    gather_row(r, jnp.where(valid, tok_ref[jnp.where(valid, base + r, 0)], n_tok))

        n_sub = (left + sub - 1) // sub
        for m in range(1, rows // sub + 1):
            @pl.when(n_sub == m)
            def _():
                rm = m * sub
                xb = jnp.concatenate([xt_ref[pl.ds(c * stride, rm), :] for c in range(n_chunks)], axis=1).astype(BF16)
                h = (_silu(_dot(xb, wg)) * _dot(xb, wu)).astype(BF16)
                y = _dot(h, wd)
                for c in range(n_chunks):
                    y_ref[pl.ds(c * stride, rm), :] = y[:, LANES * c:LANES * (c + 1)]
        for s0 in range(0, rows, sub):
            @pl.when(s0 < left)
            def _():
                for r0 in range(s0, s0 + sub, batch):
                    pending = []
                    for r in range(r0, r0 + batch):
                        t = trow_ref[r]
                        wt = wc_ref[0, 0, jnp.minimum(t, n_tok - 1)]
                        pending.append((t, acc_ref[t] + wt * y_ref[pl.ds(r, n_chunks, stride=stride), :]))
                    for t, val in pending:
                        acc_ref[t] = val
        return carry

    lax.fori_loop(0, (n + rows - 1) // rows, block, 0)

    @pl.when(e == pl.num_programs(0) - 1)
    def _():
        cp = pltpu.make_async_copy(acc_ref.at[pl.ds(0, n_tok)], out_hbm, sem)
        cp.start()
        cp.wait()


def _moe(start, cnt, tok_sorted, wc, x_rows, w_gate, w_up, w_down, layer, n_tok, rows, vmem_mib):
    n_chunks = OUT_CHUNKS
    wspec = lambda a, b: pl.BlockSpec((None, None, a, b), lambda e, st, ct: (layer, e, 0, 0))
    grid_spec = pltpu.PrefetchScalarGridSpec(
        num_scalar_prefetch=2, grid=(N_EXPERTS,),
        in_specs=[pl.BlockSpec(memory_space=pltpu.SMEM),
                  pl.BlockSpec((1, 1, n_tok), lambda e, st, ct: (e, 0, 0), memory_space=pltpu.SMEM),
                  pl.BlockSpec(memory_space=pl.ANY),
                  wspec(D_MODEL, EXPERT_FF), wspec(D_MODEL, EXPERT_FF), wspec(EXPERT_FF, D_MODEL)],
        out_specs=pl.BlockSpec(memory_space=pl.ANY),
        scratch_shapes=[pltpu.VMEM((n_tok // 2 + 1, 2 * n_chunks, LANES), BF16),
                        pltpu.VMEM((n_tok + 1, n_chunks, LANES), F32),
                        pltpu.VMEM((n_chunks * (rows + SUBLANES), LANES), F32),
                        pltpu.VMEM((n_chunks * (rows + SUBLANES), LANES), F32),
                        pltpu.SMEM((rows,), I32),
                        pltpu.SemaphoreType.DMA(())])
    return pl.pallas_call(
        functools.partial(_moe_kernel, n_tok=n_tok, rows=rows, sub=64, batch=8),
        grid_spec=grid_spec, out_shape=jax.ShapeDtypeStruct((n_tok, n_chunks, LANES), F32),
        compiler_params=_cparams(1, vmem_mib), name="moe",
    )(start, cnt, tok_sorted, wc, x_rows.reshape(n_tok // 2, 2 * n_chunks, LANES), w_gate, w_up, w_down
      ).reshape(n_tok * n_chunks, LANES)


def _ffn_out_kernel(x1_ref, u2_ref, routed_ref, mod_ref, wgu_ref, wd_ref, lnw_ref, lnb_ref, o_ref):
    gu = _dot(u2_ref[...], wgu_ref[...])
    shared = _dot((_silu(gu[:, 0:SHARED_FF]) * gu[:, SHARED_FF:]).astype(BF16), wd_ref[...])
    tt = x1_ref.shape[0]
    routed = jnp.concatenate([routed_ref[pl.ds(c, tt, stride=OUT_CHUNKS), :] for c in range(OUT_CHUNKS)], axis=1)
    h = DEEPNORM_ALPHA * x1_ref[...] + mod_ref[0, 5:6, :] * (routed + shared)
    o_ref[...] = _layer_norm(h, lnw_ref[...], lnb_ref[...])


def _ffn_out(x1, u2, routed, modp, sh_gu, sh_d, ln_w, ln_b, layer, seq_len, tt):
    t = x1.shape[0]
    tiles_per_seq = seq_len // tt
    row = lambda i: (i, 0)
    lsel = lambda *w: pl.BlockSpec((None,) + w, lambda i: (layer,) + (0,) * len(w))
    ln_spec = pl.BlockSpec((None, None, 1, D_MODEL), lambda i: (layer, 1, 0, 0))
    return pl.pallas_call(
        _ffn_out_kernel, grid=(t // tt,),
        in_specs=[pl.BlockSpec((tt, D_MODEL), row), pl.BlockSpec((tt, D_MODEL), row),
                  pl.BlockSpec((tt * OUT_CHUNKS, LANES), row),
                  _mod_spec(modp, tiles_per_seq),
                  lsel(D_MODEL, 2 * SHARED_FF), lsel(SHARED_FF, D_MODEL), ln_spec, ln_spec],
        out_specs=pl.BlockSpec((tt, D_MODEL), row),
        out_shape=jax.ShapeDtypeStruct((t, D_MODEL), F32),
        compiler_params=_cparams(1, 48), name="ffn_out",
    )(x1, u2, routed, modp, sh_gu, sh_d, ln_w, ln_b)


def _q_perm():
    cols = []
    for j in range(ATT_HEADS // 2):
        for half in range(2):
            h = j + (ATT_HEADS // 2) * half
            cols.extend(range(h * HEAD_DIM, (h + 1) * HEAD_DIM))
    return np.asarray(cols, np.int32)


def _prepare_params(p):
    off = np.concatenate([[0], np.cumsum(IN_SIZES)])
    seg = lambda k: np.arange(off[k], off[k + 1], dtype=np.int32)
    qp = _q_perm()
    cols = np.concatenate([qp, seg(1), seg(2), seg(3), seg(4), seg(5), seg(6), seg(7), seg(9), seg(10), seg(8), seg(11)])
    w_in = jnp.take(p["w_in"], jnp.asarray(cols), axis=2)
    w_in = jnp.pad(w_in, ((0, 0), (0, 0), (0, IN_W_PAD - w_in.shape[2]))).astype(BF16)
    rows = np.concatenate([qp, np.arange(ATT_WIDTH, D_MODEL, dtype=np.int32)])
    w_out = jnp.take(p["w_out"], jnp.asarray(rows), axis=1).astype(BF16)
    rwt = jnp.swapaxes(p["router_w"], 1, 2)
    rwh = rwt.astype(BF16)
    rwl = (rwt - rwh.astype(F32)).astype(BF16)

    def lane_pad(a, fill=0.0):
        a = a.reshape(DEPTH, 1, -1)
        return jnp.pad(a, ((0, 0), (0, 0), (0, LANES - a.shape[2])), constant_values=fill)

    gi = np.arange(512) // HEAD_DIM
    g512 = jnp.asarray(gi[:, None] == gi[None, :], BF16)
    return {
        "w_in": w_in, "w_out": w_out, "rwh": rwh, "rwl": rwl,
        "qnw": jnp.tile(p["q_norm_w"], (1, ATT_HEADS)).reshape(DEPTH, 1, 512),
        "knw": jnp.tile(p["k_norm_w"], (1, ATT_KV_HEADS)).reshape(DEPTH, 1, LANES),
        "g512": g512, "g128": g512[0:LANES, 0:LANES],
        "conv_w": p["ssd_conv_w"], "conv_b": p["ssd_conv_b"].reshape(DEPTH, 1, CONV_CH),
        "dt_bias": lane_pad(p["ssd_dt_bias"]), "a_log": lane_pad(p["ssd_a_log"]),
        "ret_ld": lane_pad(p["ret_log_decay"], -1.0),
        "ret_nw": p["ret_norm_w"].reshape(DEPTH, 1, RET_WIDTH),
        "d_lane": jnp.repeat(p["ssd_d"], HEAD_DIM, axis=1).reshape(DEPTH, 1, SSD_WIDTH),
        "ssd_nw": p["ssd_norm_w"].reshape(DEPTH, 1, SSD_WIDTH),
        "ln_w": p["ln_w"].reshape(DEPTH, 2, 1, D_MODEL), "ln_b": p["ln_b"].reshape(DEPTH, 2, 1, D_MODEL),
        "router_bias": p["router_bias"].reshape(DEPTH, N_EXPERTS, 1),
        "sh_gu": jnp.concatenate([p["sh_w_gate"], p["sh_w_up"]], axis=2).astype(BF16),
        "sh_d": p["sh_w_down"].astype(BF16),
        "exp_w_gate": p["exp_w_gate"], "exp_w_up": p["exp_w_up"], "exp_w_down": p["exp_w_down"],
    }


def _rope_tables(n_tokens):
    t = jnp.arange(n_tokens, dtype=I32)
    rows = (t // GRID_W).astype(F32)
    cols = (t % GRID_W).astype(F32)
    inv = ROPE_THETA ** (-jnp.arange(ROPE_AXIS_FREQS, dtype=F32) / ROPE_AXIS_FREQS)
    ar = rows[:, None] * inv
    ac = cols[:, None] * inv
    cos = jnp.concatenate([jnp.cos(ar), jnp.cos(ar), jnp.cos(ac), jnp.cos(ac)], axis=1)
    sin = jnp.concatenate([-jnp.sin(ar), jnp.sin(ar), -jnp.sin(ac), jnp.sin(ac)], axis=1)
    return jnp.tile(cos, (1, 2)), jnp.tile(sin, (1, 2))


def _pair_states(st, kind):
    b = st.shape[0]
    out = jnp.zeros((b, 2, 2, LANES, LANES), F32)
    for pr in range(2):
        for tt in range(2):
            r0 = HEAD_DIM * (tt if kind == "ret" else pr)
            out = out.at[:, :, pr, r0:r0 + HEAD_DIM, HEAD_DIM * tt:HEAD_DIM * (tt + 1)].set(st[:, :, 2 * pr + tt])
    return out


def _unpair_states(m, kind):
    heads = []
    for pr in range(2):
        for tt in range(2):
            r0 = HEAD_DIM * (tt if kind == "ret" else pr)
            heads.append(m[:, :, pr, r0:r0 + HEAD_DIM, HEAD_DIM * tt:HEAD_DIM * (tt + 1)])
    return jnp.stack(heads, axis=2)


def _trunk_layer(x, modp, pw, layer, n_seq, seq_len, rope_tabs, cache, init, cfg):
    n_tok = x.shape[0]
    ctx = cache is None
    outs = _proj_in(x, modp, pw["w_in"], pw["qnw"], pw["knw"], pw["g512"], rope_tabs, layer, seq_len,
                    cfg["tt"], emit_cache=ctx)
    q, k, v, ret, ssd = outs[:5]
    att = _attention(q, k, v, cache, layer, n_seq, seq_len, cfg["tq"])
    sc = _scan(ret, ssd, pw, layer, init, n_seq, seq_len, cfg["chunk"], emit_state=ctx)
    x1, u2, u2_rows, lg_t = _proj_out(x, att, sc[0], sc[1], modp, pw["w_out"], pw["ln_w"], pw["ln_b"],
                                  pw["rwh"], pw["rwl"], layer, seq_len, cfg["tt"])
    idx_t, rank_t, wc, counts = _route(lg_t, pw["router_bias"], layer, cfg["tr"])
    cnt = counts[:, 0].astype(I32)
    start = jnp.cumsum(cnt) - cnt
    tok_sorted = _plan(start, idx_t, rank_t, cfg["tr"])
    routed = _moe(start, cnt, tok_sorted, wc, u2_rows, pw["exp_w_gate"], pw["exp_w_up"], pw["exp_w_down"],
                  layer, n_tok, cfg["rows"], cfg["moe_vmem"])
    x_new = _ffn_out(x1, u2, routed, modp, pw["sh_gu"], pw["sh_d"], pw["ln_w"], pw["ln_b"], layer, seq_len, cfg["tt"])
    new_ctx = (outs[5], outs[6], sc[2], sc[3]) if ctx else None
    return x_new, new_ctx


def kernel(x_prompt, x_sample, cache_k, cache_v, state_ret, state_ssd, c, c_ctx,
           w_mod, b_mod, w_in, q_norm_w, k_norm_w, ret_log_decay, ret_norm_w,
           ssd_conv_w, ssd_conv_b, ssd_dt_bias, ssd_a_log, ssd_d, ssd_norm_w, w_out,
           ln_w, ln_b, router_w, router_bias, exp_w_gate, exp_w_up, exp_w_down,
           sh_w_gate, sh_w_up, sh_w_down):
    pw = _prepare_params({
        "w_in": w_in, "q_norm_w": q_norm_w, "k_norm_w": k_norm_w, "ret_log_decay": ret_log_decay,
        "ret_norm_w": ret_norm_w, "ssd_conv_w": ssd_conv_w, "ssd_conv_b": ssd_conv_b, "ssd_dt_bias": ssd_dt_bias,
        "ssd_a_log": ssd_a_log, "ssd_d": ssd_d, "ssd_norm_w": ssd_norm_w, "w_out": w_out, "ln_w": ln_w, "ln_b": ln_b,
        "router_w": router_w, "router_bias": router_bias, "exp_w_gate": exp_w_gate, "exp_w_up": exp_w_up,
        "exp_w_down": exp_w_down, "sh_w_gate": sh_w_gate, "sh_w_up": sh_w_up, "sh_w_down": sh_w_down})
    nb, seq, d = x_prompt.shape
    nd, dseq, _ = x_sample.shape
    cond = jnp.concatenate([c_ctx[None], c, jnp.zeros((8 - 1 - nd, d), F32)], axis=0)
    mod = _modulation(cond, w_mod, b_mod)
    rope_tabs = _rope_tables(dseq)
    past = cache_k.shape[2]
    ck = cache_k.reshape(nd, DEPTH, past, LANES)
    cv = cache_v.reshape(nd, DEPTH, past, LANES)
    cfg_ctx = {"tt": 512, "tq": seq, "chunk": SCAN_CHUNK, "tr": 512, "rows": MOE_ROWS, "moe_vmem": MOE_VMEM_MIB}
    cfg_lat = {"tt": 512, "tq": 512, "chunk": SCAN_CHUNK, "tr": 512, "rows": MOE_ROWS, "moe_vmem": MOE_VMEM_MIB}

    xc = x_prompt.reshape(nb * seq, d)
    xs = x_sample.reshape(nd * dseq, d)
    ks, vs, srs, sss = [], [], [], []
    for l in range(DEPTH):
        mod_ctx = mod[l, 0:1].reshape(1, 6, d)
        mod_lat = mod[l, 1:1 + nd].reshape(nd, 6, d)
        xc, (k_l, v_l, sr_l, ss_l) = _trunk_layer(xc, mod_ctx, pw, l, nb, seq, None, None, None, cfg_ctx)
        ks.append(k_l.reshape(nb, seq, ATT_KV_HEADS, HEAD_DIM))
        vs.append(v_l.reshape(nb, seq, ATT_KV_HEADS, HEAD_DIM))
        srs.append(_unpair_states(sr_l, "ret"))
        sss.append(_unpair_states(ss_l, "ssd"))
        init = (_pair_states(state_ret[:, l], "ret"), _pair_states(state_ssd[:, l], "ssd"))
        xs, _ = _trunk_layer(xs, mod_lat, pw, l, nd, dseq, rope_tabs, (ck, cv), init, cfg_lat)
    return (xc.reshape(nb, seq, d), xs.reshape(nd, dseq, d), jnp.stack(ks, axis=1), jnp.stack(vs, axis=1),
            jnp.stack(srs, axis=1), jnp.stack(sss, axis=1))
```

```python
import functools

import numpy as np
import jax
import jax.numpy as jnp
from jax import lax
from jax.experimental import pallas as pl
from jax.experimental.pallas import tpu as pltpu

F32 = jnp.float32
BF16 = jnp.bfloat16
I32 = jnp.int32
HIGHEST = lax.Precision.HIGHEST
NT_DIMS = (((1,), (1,)), ((), ()))

D_MODEL = 1024
DEPTH = 4
GRID_W = 64
HEAD_DIM = 64
ATT_HEADS = 8
ATT_KV_HEADS = 2
ATT_WIDTH = ATT_HEADS * HEAD_DIM
ROPE_THETA = 10000.0
ROPE_AXIS_FREQS = HEAD_DIM // 4
RET_HEADS = 4
RET_DK = 64
RET_WIDTH = 256
SSD_HEADS = 4
SSD_WIDTH = 256
SSD_STATE = 64
N_EXPERTS = 256
TOP_K = 8
ROUTER_GROUPS = 8
ROUTER_TOPK_GROUPS = 4
GROUP_SIZE = N_EXPERTS // ROUTER_GROUPS
EXPERT_FF = 256
SHARED_FF = 256
ROUTED_SCALE = 2.5
DEEPNORM_ALPHA = (2 * DEPTH) ** 0.25
LOG2_E = 1.4426950408889634
IN_SIZES = (512, 128, 128, 256, 256, 256, 256, 256, 256, 128, 128, 8)

LANES = 128
SUBLANES = 8
MIB = 1024 * 1024

QKV_W = 768
RET_OFF, RET_W = 768, 1024
SSD_OFF, SSD_W = 1792, 896
IN_W_PAD = 2688
CONV_CH = 512

OUT_CHUNKS = D_MODEL // LANES
PLAN_BLOCK = 4096
SCAN_CHUNK = 256
MOE_ROWS = 320
MOE_VMEM_MIB = 62


def _cparams(n_grid, vmem_mib):
    return pltpu.CompilerParams(dimension_semantics=("arbitrary",) * n_grid,
                                vmem_limit_bytes=vmem_mib * MIB)


def _dot(a, b):
    return jnp.dot(a, b, preferred_element_type=F32)


def _dot_nt(a, b):
    return lax.dot_general(a, b, NT_DIMS, preferred_element_type=F32)


def _silu(x):
    return x * jax.nn.sigmoid(x)


def _softplus(x):
    return jnp.maximum(x, 0.0) + jnp.log1p(jnp.exp(-jnp.abs(x)))


def _mod_spec(modp, tiles_per_seq):
    if modp.shape[0] == 1:
        return pl.BlockSpec((1, 6, D_MODEL), lambda i: (0, 0, 0))
    return pl.BlockSpec((1, 6, D_MODEL), lambda i: (i // tiles_per_seq, 0, 0))


def _group_sum(x, g):
    hi = x.astype(BF16)
    lo = (x - hi.astype(F32)).astype(BF16)
    return _dot(hi, g) + _dot(lo, g)


def _swap16(x):
    n = x.shape[1]
    lane = lax.broadcasted_iota(I32, x.shape, 1)
    return jnp.where((lane & 16) == 0, pltpu.roll(x, n - 16, 1), pltpu.roll(x, 16, 1))


def _layer_norm(h, w, b):
    mu = jnp.mean(h, axis=-1, keepdims=True)
    d = h - mu
    var = jnp.mean(d * d, axis=-1, keepdims=True)
    return d * lax.rsqrt(var + 1e-5) * w + b


def _mod_kernel(cond_ref, w_ref, b_ref, o_ref):
    c = cond_ref[...]
    o_ref[0] = jnp.dot(_silu(c), w_ref[0], precision=HIGHEST, preferred_element_type=F32) + b_ref[0]


def _modulation(cond8, w_mod, b_mod):
    tn = 1536
    n_col = w_mod.shape[2] // tn
    return pl.pallas_call(
        _mod_kernel,
        grid=(DEPTH, n_col),
        in_specs=[pl.BlockSpec((8, D_MODEL), lambda l, j: (0, 0)),
                  pl.BlockSpec((1, D_MODEL, tn), lambda l, j: (l, 0, j)),
                  pl.BlockSpec((1, 1, tn), lambda l, j: (l, 0, j))],
        out_specs=pl.BlockSpec((1, 8, tn), lambda l, j: (l, 0, j)),
        out_shape=jax.ShapeDtypeStruct((DEPTH, 8, w_mod.shape[2]), F32),
        compiler_params=_cparams(2, 32),
        name="modulation",
    )(cond8, w_mod, b_mod.reshape(DEPTH, 1, -1))


def _proj_in_kernel(*refs, rope, emit_cache):
    x_ref, mod_ref, w_ref, qnw_ref, knw_ref, g_ref = refs[:6]
    pos = 6
    if rope:
        cos_ref, sin_ref = refs[pos:pos + 2]
        pos += 2
    q_ref, k_ref, v_ref, ret_ref, ssd_ref = refs[pos:pos + 5]
    pos += 5
    u = x_ref[...] * (1.0 + mod_ref[0, 1:2, :]) + mod_ref[0, 0:1, :]
    ub = u.astype(BF16)
    q = _dot(ub, w_ref[:, 0:512])
    k = _dot(ub, w_ref[:, 512:640])
    v = _dot(ub, w_ref[:, 640:768])
    g = g_ref[...]
    inv_hd = 1.0 / HEAD_DIM
    qn = q * lax.rsqrt(_group_sum(q * q, g) * inv_hd + 1e-6) * qnw_ref[...]
    kn = k * lax.rsqrt(_group_sum(k * k, g[0:LANES, 0:LANES]) * inv_hd + 1e-6) * knw_ref[...]
    if emit_cache:
        kf_ref, vf_ref = refs[pos:pos + 2]
        kf_ref[...] = kn
        vf_ref[...] = v
    if rope:
        c = cos_ref[...]
        s = sin_ref[...]
        kn = kn * c + _swap16(kn) * s
        qn = qn * jnp.concatenate([c] * 4, axis=1) + _swap16(qn) * jnp.concatenate([s] * 4, axis=1)
    q_ref[...] = (qn * (HEAD_DIM ** -0.5 * LOG2_E)).astype(BF16)
    k_ref[...] = kn.astype(BF16)
    v_ref[...] = v.astype(BF16)
    ret_ref[...] = _dot(ub, w_ref[:, RET_OFF:RET_OFF + RET_W])
    ssd_ref[...] = _dot(ub, w_ref[:, SSD_OFF:SSD_OFF + SSD_W])


def _proj_in(x, modp, w_in_p, qnw, knw, g512, rope_tabs, layer, seq_len, tt, emit_cache):
    t = x.shape[0]
    tiles_per_seq = seq_len // tt
    rope = rope_tabs is not None
    row = lambda i: (i, 0)
    const2 = lambda i: (0, 0)
    in_specs = [pl.BlockSpec((tt, D_MODEL), row),
                _mod_spec(modp, tiles_per_seq),
                pl.BlockSpec((None, D_MODEL, IN_W_PAD), lambda i: (layer, 0, 0)),
                pl.BlockSpec((None, 1, 512), lambda i: (layer, 0, 0)),
                pl.BlockSpec((None, 1, LANES), lambda i: (layer, 0, 0)),
                pl.BlockSpec((512, 512), const2)]
    args = [x, modp, w_in_p, qnw, knw, g512]
    if rope:
        tab = pl.BlockSpec((tt, LANES), lambda i: (i % tiles_per_seq, 0))
        in_specs += [tab, tab]
        args += list(rope_tabs)
    out_shape = [jax.ShapeDtypeStruct((t, 512), BF16), jax.ShapeDtypeStruct((t, LANES), BF16),
                 jax.ShapeDtypeStruct((t, LANES), BF16), jax.ShapeDtypeStruct((t, RET_W), F32),
                 jax.ShapeDtypeStruct((t, SSD_W), F32)]
    out_specs = [pl.BlockSpec((tt, 512), row), pl.BlockSpec((tt, LANES), row), pl.BlockSpec((tt, LANES), row),
                 pl.BlockSpec((tt, RET_W), row), pl.BlockSpec((tt, SSD_W), row)]
    if emit_cache:
        out_shape += [jax.ShapeDtypeStruct((t, LANES), F32)] * 2
        out_specs += [pl.BlockSpec((tt, LANES), row)] * 2
    return pl.pallas_call(
        functools.partial(_proj_in_kernel, rope=rope, emit_cache=emit_cache),
        grid=(t // tt,), in_specs=in_specs, out_specs=out_specs, out_shape=out_shape,
        compiler_params=_cparams(1, 48), name="proj_in",
    )(*args)


def _attn_kernel(*refs, with_cache, group):
    q_ref, k_ref, v_ref = refs[:3]
    n_in = 5 if with_cache else 3
    o_ref = refs[n_in]
    s_refs = refs[n_in + 1:n_in + 3]
    p_refs = refs[n_in + 3:n_in + 5]
    tq = q_ref.shape[0]
    lane = lax.broadcasted_iota(I32, (1, LANES), 1)
    first = lane < HEAD_DIM
    parts = [(k_ref[...], v_ref[...])]
    if with_cache:
        parts.append((refs[3][...].astype(BF16), refs[4][...].astype(BF16)))
    bounds = np.cumsum([0] + [kk.shape[0] for kk, _ in parts])
    zero = jnp.zeros((), BF16)
    kmask = [[jnp.where(first, kk, zero), jnp.where(first, zero, kk)] for kk, _ in parts]
    one = jnp.ones((), BF16)
    vaug = [[jnp.where(first, vv, one), jnp.where(first, one, vv)] for _, vv in parts]
    head = 0
    for j in range(ATT_HEADS // 2):
        qb = q_ref[:, LANES * j:LANES * (j + 1)]
        outs = []
        for g in range(ATT_KV_HEADS):
            s_ref, p_ref = s_refs[head % 2], p_refs[head % 2]
            head += 1
            for p, km in enumerate(kmask):
                s_ref[:, bounds[p]:bounds[p + 1]] = _dot_nt(qb, km[g])
            for r0 in range(0, tq, group):
                s = s_ref[r0:r0 + group, :]
                p_ref[r0:r0 + group, :] = jnp.exp2(s - s.max(axis=-1, keepdims=True)).astype(BF16)
            acc = None
            for p, va in enumerate(vaug):
                o = _dot(p_ref[:, bounds[p]:bounds[p + 1]], va[g])
                acc = o if acc is None else acc + o
            outs.append(acc / pltpu.roll(acc, HEAD_DIM, 1))
        o_ref[:, LANES * j:LANES * (j + 1)] = jnp.where(first, outs[0], outs[1]).astype(BF16)


def _attention(q, k, v, cache, layer, n_seq, seq_len, tq):
    t = q.shape[0]
    nq = seq_len // tq
    in_specs = [pl.BlockSpec((tq, 512), lambda b, i: (b * nq + i, 0)),
                pl.BlockSpec((seq_len, LANES), lambda b, i: (b, 0)),
                pl.BlockSpec((seq_len, LANES), lambda b, i: (b, 0))]
    args = [q, k, v]
    n_keys = seq_len
    if cache is not None:
        past = cache[0].shape[2]
        n_keys += past
        cspec = pl.BlockSpec((None, None, past, LANES), lambda b, i: (b, layer, 0, 0))
        in_specs += [cspec, cspec]
        args += list(cache)
    return pl.pallas_call(
        functools.partial(_attn_kernel, with_cache=cache is not None, group=2 * SUBLANES),
        grid=(n_seq, nq), in_specs=in_specs,
        out_specs=pl.BlockSpec((tq, 512), lambda b, i: (b * nq + i, 0)),
        out_shape=jax.ShapeDtypeStruct((t, 512), BF16),
        scratch_shapes=[pltpu.VMEM((tq, n_keys), F32)] * 2 + [pltpu.VMEM((tq, n_keys), BF16)] * 2,
        compiler_params=_cparams(2, 56), name="attention",
    )(*args)


def _scan_kernel(*refs, n, chunk, has_init, emit_state):
    (ret_ref, ssd_ref, hp_ref, hn_ref, cw_ref, cb_ref, dtb_ref, alog_ref, rld_ref,
     rnw_ref, dl_ref, snw_ref, g_ref) = refs[:13]
    pos = 13
    if has_init:
        s0r_ref, s0s_ref = refs[pos:pos + 2]
        pos += 2
    reto_ref, ssdo_ref = refs[pos:pos + 2]
    pos += 2
    if emit_state:
        str_ref, sts_ref = refs[pos:pos + 2]
        pos += 2
    of_ref, st_ref = refs[pos:pos + 2]

    C = chunk
    s = pl.program_id(1)
    bwd = s >= n
    c = jnp.where(bwd, 2 * n - 1 - s, s)

    def init_state(direction):
        if has_init:
            st_ref[0:2] = s0s_ref[0, direction]
            st_ref[2:4] = s0r_ref[0, direction]
        else:
            st_ref[...] = jnp.zeros(st_ref.shape, F32)

    @pl.when(s == 0)
    def _():
        init_state(0)

    @pl.when(s == n)
    def _():
        init_state(1)

    lane = lax.broadcasted_iota(I32, (1, LANES), 1)
    first = lane < HEAD_DIM
    ret = ret_ref[...]
    ssd = ssd_ref[...]

    xin = ssd[:, 0:CONV_CH]
    rowi = lax.broadcasted_iota(I32, (C, 1), 0)
    prev_row = jnp.where(c == 0, 0.0, hp_ref[SUBLANES - 1:SUBLANES, :])
    next_row = jnp.where(c == n - 1, 0.0, hn_ref[0:1, :])
    xp = jnp.where(rowi == 0, prev_row, pltpu.roll(xin, 1, 0))
    xn = jnp.where(rowi == C - 1, next_row, pltpu.roll(xin, C - 1, 0))
    xbc = _silu(xp * cw_ref[0:1, :] + xin * cw_ref[1:2, :] + xn * cw_ref[2:3, :] + cb_ref[...])
    sx = xbc[:, 0:256]
    sb = xbc[:, 256:384]
    sc = xbc[:, 384:512]

    def this_dir(a):
        return jnp.where(bwd, pltpu.roll(a, LANES - SSD_HEADS, 1), a)

    dt_all = _softplus(ssd[:, 768:896] + dtb_ref[...])
    la_ssd = this_dir(dt_all * (-jnp.exp(alog_ref[...])))
    dt_dir = this_dir(dt_all)
    lg = jnp.broadcast_to(jnp.log1p(-jnp.exp(rld_ref[...])), (SUBLANES, LANES))
    lg_dir = pltpu.roll(this_dir(lg), SSD_HEADS, 1)[0:1, :]
    la = jnp.where(lane < 4, la_ssd, jnp.where(lane < 8, lg_dir, 0.0))
    ii = lax.broadcasted_iota(I32, (C, C), 0)
    jj = lax.broadcasted_iota(I32, (C, C), 1)
    msk = jnp.where(bwd, jj - ii, ii - jj) >= 0
    cum = jnp.dot(msk.astype(F32), la, precision=HIGHEST, preferred_element_type=F32)
    last = jnp.where(bwd, cum[0:1, :], cum[C - 1:C, :])
    acol = jnp.where(lane < 8, cum, jnp.where(lane < 12, pltpu.roll(dt_dir, 8, 1), 0.0))
    arow = acol.T

    def decay(h, scale):
        diff = cum[:, h:h + 1] - arow[h:h + 1, :]
        return jnp.where(msk, jnp.exp(jnp.where(msk, diff, 0.0)), 0.0) * scale

    def lanes2(a0, a1):
        return jnp.where(first, a0, a1)

    ii2 = lax.broadcasted_iota(I32, (LANES, LANES), 0)
    jj2 = lax.broadcasted_iota(I32, (LANES, LANES), 1)
    blockdiag = ((ii2 < HEAD_DIM) == (jj2 < HEAD_DIM)).astype(F32)

    o_pairs = []
    for kind in ("ssd", "ret"):
        for p in range(2):
            slot = p if kind == "ssd" else 2 + p
            heads = (2 * p, 2 * p + 1)
            hl = [h if kind == "ssd" else 4 + h for h in heads]
            if kind == "ssd":
                grp = first if p == 0 else jnp.logical_not(first)
                qm = jnp.where(grp, sc, 0.0).astype(BF16)
                km = jnp.where(grp, sb, 0.0)
                v128 = sx[:, LANES * p:LANES * (p + 1)]
                sg = _dot_nt(qm, sb.astype(BF16))
                scores = [sg, sg]
                dec = [decay(h, arow[8 + h:9 + h, :]) for h in heads]
                cs = [acol[:, 8 + h:9 + h] for h in heads]
                q_inter = qm
            else:
                q128 = ret[:, LANES * p:LANES * (p + 1)]
                km = ret[:, 256 + LANES * p:256 + LANES * (p + 1)]
                v128 = ret[:, 512 + LANES * p:512 + LANES * (p + 1)]
                kb = km.astype(BF16)
                scores = [_dot_nt(jnp.where(first, q128, 0.0).astype(BF16), kb),
                          _dot_nt(jnp.where(first, 0.0, q128).astype(BF16), kb)]
                dec = [decay(SSD_HEADS + h, RET_DK ** -0.5) for h in heads]
                cs = [RET_DK ** -0.5] * 2
                q_inter = q128.astype(BF16)
            vb = v128.astype(BF16)
            intra = [_dot((scores[t] * dec[t]).astype(BF16), vb) for t in range(2)]
            state = st_ref[slot]
            ecum = lanes2(jnp.exp(cum[:, hl[0]:hl[0] + 1]), jnp.exp(cum[:, hl[1]:hl[1] + 1]))
            o_pairs.append(lanes2(intra[0], intra[1]) + _dot(q_inter, state.astype(BF16)) * ecum)
            wv = lanes2(jnp.exp(last[:, hl[0]:hl[0] + 1] - cum[:, hl[0]:hl[0] + 1]) * cs[0],
                        jnp.exp(last[:, hl[1]:hl[1] + 1] - cum[:, hl[1]:hl[1] + 1]) * cs[1])
            snew = _dot(km.T.astype(BF16), (v128 * wv).astype(BF16))
            if kind == "ret":
                snew = snew * blockdiag
            elast = lanes2(jnp.exp(last[:, hl[0]:hl[0] + 1]), jnp.exp(last[:, hl[1]:hl[1] + 1]))
            st_ref[slot] = elast * state + snew
    o_all = jnp.concatenate(o_pairs, axis=1)
    rows = pl.ds(pl.multiple_of(c * C, C), C)

    @pl.when(jnp.logical_not(bwd))
    def _():
        of_ref[rows, :] = o_all

    @pl.when(bwd)
    def _():
        tot = of_ref[rows, :] + o_all
        g = g_ref[...]
        inv = 1.0 / HEAD_DIM
        for p in range(2):
            o = tot[:, 256 + LANES * p:256 + LANES * (p + 1)]
            d = o - _group_sum(o, g) * inv
            var = _group_sum(d * d, g) * inv
            rn = d * lax.rsqrt(var + 1e-5) * rnw_ref[:, LANES * p:LANES * (p + 1)]
            gate = ret[:, 768 + LANES * p:768 + LANES * (p + 1)]
            reto_ref[:, LANES * p:LANES * (p + 1)] = (rn * _silu(gate)).astype(BF16)
        y = (tot[:, 0:256] + dl_ref[...] * sx) * _silu(ssd[:, 512:768])
        ms = jnp.mean(y * y, axis=-1, keepdims=True)
        ssdo_ref[...] = (y * lax.rsqrt(ms + 1e-6) * snw_ref[...]).astype(BF16)

    if emit_state:
        @pl.when(s == n - 1)
        def _():
            sts_ref[0, 0] = st_ref[0:2]
            str_ref[0, 0] = st_ref[2:4]

        @pl.when(s == 2 * n - 1)
        def _():
            sts_ref[0, 1] = st_ref[0:2]
            str_ref[0, 1] = st_ref[2:4]


def _scan(ret, ssd, pw, layer, init, n_seq, seq_len, chunk, emit_state):
    t = ret.shape[0]
    n = seq_len // chunk
    rows8 = t // SUBLANES

    def cidx(s):
        return jnp.where(s >= n, 2 * n - 1 - s, s)

    def oidx(s):
        return jnp.where(s >= n, 2 * n - 1 - s, n - 1)

    par = lambda w: pl.BlockSpec((None,) + w, lambda b, s: (layer,) + (0,) * len(w))
    in_specs = [pl.BlockSpec((chunk, RET_W), lambda b, s: (b * n + cidx(s), 0)),
                pl.BlockSpec((chunk, SSD_W), lambda b, s: (b * n + cidx(s), 0)),
                pl.BlockSpec((SUBLANES, CONV_CH),
                             lambda b, s: (jnp.maximum((b * seq_len + cidx(s) * chunk) // SUBLANES - 1, 0), 0)),
                pl.BlockSpec((SUBLANES, CONV_CH),
                             lambda b, s: (jnp.minimum((b * seq_len + (cidx(s) + 1) * chunk) // SUBLANES, rows8 - 1), 0)),
                par((3, CONV_CH)), par((1, CONV_CH)), par((1, LANES)), par((1, LANES)), par((1, LANES)),
                par((1, RET_WIDTH)), par((1, SSD_WIDTH)), par((1, SSD_WIDTH)),
                pl.BlockSpec((LANES, LANES), lambda b, s: (0, 0))]
    args = [ret, ssd, ssd, ssd, pw["conv_w"], pw["conv_b"], pw["dt_bias"], pw["a_log"], pw["ret_ld"],
            pw["ret_nw"], pw["d_lane"], pw["ssd_nw"], pw["g128"]]
    st_spec = pl.BlockSpec((1, 2, 2, LANES, LANES), lambda b, s: (b, 0, 0, 0, 0))
    if init is not None:
        in_specs += [st_spec, st_spec]
        args += list(init)
    out_specs = [pl.BlockSpec((chunk, RET_WIDTH), lambda b, s: (b * n + oidx(s), 0)),
                 pl.BlockSpec((chunk, SSD_WIDTH), lambda b, s: (b * n + oidx(s), 0))]
    out_shape = [jax.ShapeDtypeStruct((t, RET_WIDTH), BF16), jax.ShapeDtypeStruct((t, SSD_WIDTH), BF16)]
    if emit_state:
        out_specs += [st_spec, st_spec]
        out_shape += [jax.ShapeDtypeStruct((n_seq, 2, 2, LANES, LANES), F32)] * 2
    return pl.pallas_call(
        functools.partial(_scan_kernel, n=n, chunk=chunk, has_init=init is not None, emit_state=emit_state),
        grid=(n_seq, 2 * n), in_specs=in_specs, out_specs=out_specs, out_shape=out_shape,
        scratch_shapes=[pltpu.VMEM((seq_len, 512), F32), pltpu.VMEM((4, LANES, LANES), F32)],
        compiler_params=_cparams(2, 48), name="scan",
    )(*args)


def _proj_out_kernel(x_ref, att_ref, ro_ref, so_ref, mod_ref, wo_ref, lnw_ref, lnb_ref, rwh_ref, rwl_ref,
                     x1_ref, u2_ref, u2c_ref, lg_ref, rows_ref):
    mix = (_dot(att_ref[...], wo_ref[0:512, :]) + _dot(ro_ref[...], wo_ref[512:768, :])
           + _dot(so_ref[...], wo_ref[768:1024, :]))
    h = DEEPNORM_ALPHA * x_ref[...] + mod_ref[0, 2:3, :] * mix
    x1 = _layer_norm(h, lnw_ref[...], lnb_ref[...])
    x1_ref[...] = x1
    u2 = x1 * (1.0 + mod_ref[0, 4:5, :]) + mod_ref[0, 3:4, :]
    hi = u2.astype(BF16)
    u2_ref[...] = hi
    for c in range(OUT_CHUNKS):
        rows_ref[pl.ds(c, u2.shape[0], stride=OUT_CHUNKS), :] = u2[:, LANES * c:LANES * (c + 1)]
    u2c_ref[...] = rows_ref[...].astype(BF16)
    lo = (u2 - hi.astype(F32)).astype(BF16)
    rwh = rwh_ref[...]
    lg_ref[...] = _dot_nt(rwh, hi) + _dot_nt(rwh, lo) + _dot_nt(rwl_ref[...], hi)


def _proj_out(x, att, ro, so, modp, w_out_p, ln_w, ln_b, rwh, rwl, layer, seq_len, tt):
    t = x.shape[0]
    tiles_per_seq = seq_len // tt
    row = lambda i: (i, 0)
    lsel = lambda *w: pl.BlockSpec((None,) + w, lambda i: (layer,) + (0,) * len(w))
    in_specs = [pl.BlockSpec((tt, D_MODEL), row), pl.BlockSpec((tt, 512), row),
                pl.BlockSpec((tt, RET_WIDTH), row), pl.BlockSpec((tt, SSD_WIDTH), row),
                _mod_spec(modp, tiles_per_seq),
                lsel(D_MODEL, D_MODEL),
                pl.BlockSpec((None, None, 1, D_MODEL), lambda i: (layer, 0, 0, 0)),
                pl.BlockSpec((None, None, 1, D_MODEL), lambda i: (layer, 0, 0, 0)),
                lsel(N_EXPERTS, D_MODEL), lsel(N_EXPERTS, D_MODEL)]
    out_shape = [jax.ShapeDtypeStruct((t, D_MODEL), F32), jax.ShapeDtypeStruct((t, D_MODEL), BF16),
                 jax.ShapeDtypeStruct((t * OUT_CHUNKS, LANES), BF16), jax.ShapeDtypeStruct((N_EXPERTS, t), F32)]
    out_specs = [pl.BlockSpec((tt, D_MODEL), row), pl.BlockSpec((tt, D_MODEL), row),
                 pl.BlockSpec((tt * OUT_CHUNKS, LANES), row), pl.BlockSpec((N_EXPERTS, tt), lambda i: (0, i))]
    return pl.pallas_call(
        _proj_out_kernel, grid=(t // tt,), in_specs=in_specs, out_specs=out_specs, out_shape=out_shape,
        scratch_shapes=[pltpu.VMEM((tt * OUT_CHUNKS, LANES), F32)],
        compiler_params=_cparams(1, 48), name="proj_out",
    )(x, att, ro, so, modp, w_out_p, ln_w, ln_b, rwh, rwl)


def _route_kernel(lg_ref, bias_ref, idx_ref, rank_ref, wc_ref, cnt_ref, carry_ref):
    i = pl.program_id(0)

    @pl.when(i == 0)
    def _():
        carry_ref[...] = jnp.zeros(carry_ref.shape, F32)

    scores = jax.nn.sigmoid(lg_ref[...])
    sel = scores + bias_ref[...]
    n_e, tr = sel.shape
    neg = -jnp.inf
    sub = lax.broadcasted_iota(I32, (GROUP_SIZE, tr), 0).astype(F32)
    blocks = [sel[GROUP_SIZE * g:GROUP_SIZE * (g + 1), :] for g in range(ROUTER_GROUPS)]
    gscore = []
    for blk in blocks:
        m1 = blk.max(axis=0, keepdims=True)
        pos1 = jnp.min(jnp.where(blk == m1, sub, float(GROUP_SIZE)), axis=0, keepdims=True)
        m2 = jnp.max(jnp.where(sub == pos1, neg, blk), axis=0, keepdims=True)
        gscore.append(m1 + m2)
    masked = []
    for g in range(ROUTER_GROUPS):
        beat = jnp.zeros_like(gscore[g])
        for h in range(ROUTER_GROUPS):
            if h == g:
                continue
            wins = (gscore[h] >= gscore[g]) if h < g else (gscore[h] > gscore[g])
            beat = beat + wins.astype(F32)
        masked.append(jnp.where(beat < float(ROUTER_TOPK_GROUPS), blocks[g], neg))
    cur = jnp.concatenate(masked, axis=0)
    eio = lax.broadcasted_iota(I32, (n_e, tr), 0).astype(F32)
    chosen = jnp.zeros((n_e, tr), F32)
    picks = []
    for _ in range(TOP_K):
        m = cur.max(axis=0, keepdims=True)
        ik = jnp.min(jnp.where(cur == m, eio, float(n_e)), axis=0, keepdims=True)
        hit = eio == ik
        picks.append((ik, hit))
        cur = jnp.where(hit, neg, cur)
        chosen = jnp.where(hit, 1.0, chosen)
    picked = chosen * scores
    wc_ref[:, 0, :] = picked / jnp.sum(picked, axis=0, keepdims=True) * ROUTED_SCALE
    r_ = lax.broadcasted_iota(I32, (tr, tr), 0)
    c_ = lax.broadcasted_iota(I32, (tr, tr), 1)
    chosen_b = chosen.astype(BF16)
    before = _dot(chosen_b, (r_ < c_).astype(BF16)) + carry_ref[:, 0:1]
    for k, (ik, hit) in enumerate(picks):
        idx_ref[k:k + 1, :] = ik.astype(I32)
        rank_ref[k:k + 1, :] = jnp.sum(jnp.where(hit, before, 0.0), axis=0, keepdims=True).astype(I32)
    carry_ref[...] = carry_ref[...] + _dot(chosen_b, jnp.ones((tr, LANES), BF16))
    cnt_ref[...] = carry_ref[...]


def _route(lg_t, bias, layer, tr):
    t = lg_t.shape[1]
    col = lambda i: (0, i)
    return pl.pallas_call(
        _route_kernel, grid=(t // tr,),
        in_specs=[pl.BlockSpec((N_EXPERTS, tr), col),
                  pl.BlockSpec((None, N_EXPERTS, 1), lambda i: (layer, 0, 0))],
        out_specs=[pl.BlockSpec((TOP_K, tr), col), pl.BlockSpec((TOP_K, tr), col),
                   pl.BlockSpec((N_EXPERTS, 1, tr), lambda i: (0, 0, i)),
                   pl.BlockSpec((N_EXPERTS, LANES), lambda i: (0, 0))],
        out_shape=[jax.ShapeDtypeStruct((TOP_K, t), I32), jax.ShapeDtypeStruct((TOP_K, t), I32),
                   jax.ShapeDtypeStruct((N_EXPERTS, 1, t), F32), jax.ShapeDtypeStruct((N_EXPERTS, LANES), F32)],
        scratch_shapes=[pltpu.VMEM((N_EXPERTS, LANES), F32)],
        compiler_params=_cparams(1, 48), name="route",
    )(lg_t, bias)


def _slots_kernel(start_ref, idx_ref, rank_ref, slot_ref):
    n_e = start_ref.shape[0]
    tp = idx_ref.shape[1]
    eio = lax.broadcasted_iota(I32, (n_e, tp), 0)
    start = start_ref[...]
    for k in range(TOP_K):
        first = jnp.sum(jnp.where(eio == idx_ref[k:k + 1, :], start, 0.0), axis=0, keepdims=True)
        slot_ref[k:k + 1, :] = first.astype(I32) + rank_ref[k:k + 1, :]


def _plan_kernel(slot_ref, tok_ref, *, n_tok, blk, unroll):
    first_tok = lax.rem(pl.program_id(0) * blk, n_tok)

    def body(io, carry):
        i0 = io * unroll
        for u in range(unroll):
            tok_ref[slot_ref[i0 + u]] = first_tok + i0 + u
        return carry

    lax.fori_loop(0, blk // unroll, body, 0)


def _plan(start, idx_t, rank_t, tp):
    t = idx_t.shape[1]
    col = lambda i: (0, i)
    slots = pl.pallas_call(
        _slots_kernel, grid=(t // tp,),
        in_specs=[pl.BlockSpec((N_EXPERTS, 1), lambda i: (0, 0)),
                  pl.BlockSpec((TOP_K, tp), col), pl.BlockSpec((TOP_K, tp), col)],
        out_specs=pl.BlockSpec((TOP_K, tp), col),
        out_shape=jax.ShapeDtypeStruct((TOP_K, t), I32),
        compiler_params=_cparams(1, 32), name="slots",
    )(start.astype(F32).reshape(N_EXPERTS, 1), idx_t, rank_t)
    blk = min(t, PLAN_BLOCK)
    assert t % blk == 0
    return pl.pallas_call(
        functools.partial(_plan_kernel, n_tok=t, blk=blk, unroll=16), grid=(t * TOP_K // blk,),
        in_specs=[pl.BlockSpec((blk,), lambda i: (i,), memory_space=pltpu.SMEM)],
        out_specs=pl.BlockSpec(memory_space=pltpu.SMEM),
        out_shape=jax.ShapeDtypeStruct((t * TOP_K,), I32),
        compiler_params=_cparams(1, 32), name="plan",
    )(slots.reshape(t * TOP_K))


def _moe_kernel(start_ref, cnt_ref, tok_ref, wc_ref, x_hbm, wg_ref, wu_ref, wd_ref, out_hbm,
                xv_ref, acc_ref, xt_ref, y_ref, trow_ref, sem, *, n_tok, rows, sub, batch):
    e = pl.program_id(0)
    n_chunks = OUT_CHUNKS
    pair_rows = 2 * n_chunks
    stride = rows + SUBLANES

    @pl.when(e == 0)
    def _():
        cp = pltpu.make_async_copy(x_hbm, xv_ref.at[pl.ds(0, n_tok // 2)], sem)
        cp.start()
        acc_ref[...] = jnp.zeros(acc_ref.shape, F32)
        xt_ref[...] = jnp.zeros(xt_ref.shape, F32)
        xv_ref[n_tok // 2] = jnp.zeros((pair_rows, LANES), BF16)
        cp.wait()

    n = cnt_ref[e]
    first = start_ref[e]
    wg = wg_ref[...].astype(BF16)
    wu = wu_ref[...].astype(BF16)
    wd = wd_ref[...].astype(BF16)

    def block(b, carry):
        base = first + b * rows
        left = jnp.minimum(n - b * rows, rows)

        def gather_row(r, t):
            trow_ref[r] = t
            pair = xv_ref[t >> 1].astype(F32)
            xt_ref[pl.ds(r, n_chunks, stride=stride), :] = jnp.where(
                (t & 1) == 1, pair[n_chunks:pair_rows], pair[0:n_chunks])

        for s0 in range(0, rows, sub):
            @pl.when(s0 + sub <= left)
            def _():
                for r in range(s0, s0 + sub):
                    gather_row(r, tok_ref[base + r])

            @pl.when(jnp.logical_and(s0 < left, left < s0 + sub))
            def _():
                for r in range(s0, s0 + sub):
                    valid = r < left
                    gather_row(r, jnp.where(valid, tok_ref[jnp.where(valid, base + r, 0)], n_tok))

        n_sub = (left + sub - 1) // sub
        for m in range(1, rows // sub + 1):
            @pl.when(n_sub == m)
            def _():
                rm = m * sub
                xb = jnp.concatenate([xt_ref[pl.ds(c * stride, rm), :] for c in range(n_chunks)], axis=1).astype(BF16)
                h = (_silu(_dot(xb, wg)) * _dot(xb, wu)).astype(BF16)
                y = _dot(h, wd)
                for c in range(n_chunks):
                    y_ref[pl.ds(c * stride, rm), :] = y[:, LANES * c:LANES * (c + 1)]
        for s0 in range(0, rows, sub):
            @pl.when(s0 < left)
            def _():
                for r0 in range(s0, s0 + sub, batch):
                    pending = []
                    for r in range(r0, r0 + batch):
                        t = trow_ref[r]
                        wt = wc_ref[0, 0, jnp.minimum(t, n_tok - 1)]
                        pending.append((t, acc_ref[t] + wt * y_ref[pl.ds(r, n_chunks, stride=stride), :]))
                    for t, val in pending:
                        acc_ref[t] = val
        return carry

    lax.fori_loop(0, (n + rows - 1) // rows, block, 0)

    @pl.when(e == pl.num_programs(0) - 1)
    def _():
        cp = pltpu.make_async_copy(acc_ref.at[pl.ds(0, n_tok)], out_hbm, sem)
        cp.start()
        cp.wait()


def _moe(start, cnt, tok_sorted, wc, x_rows, w_gate, w_up, w_down, layer, n_tok, rows, vmem_mib):
    n_chunks = OUT_CHUNKS
    wspec = lambda a, b: pl.BlockSpec((None, None, a, b), lambda e, st, ct: (layer, e, 0, 0))
    grid_spec = pltpu.PrefetchScalarGridSpec(
        num_scalar_prefetch=2, grid=(N_EXPERTS,),
        in_specs=[pl.BlockSpec(memory_space=pltpu.SMEM),
                  pl.BlockSpec((1, 1, n_tok), lambda e, st, ct: (e, 0, 0), memory_space=pltpu.SMEM),
                  pl.BlockSpec(memory_space=pl.ANY),
                  wspec(D_MODEL, EXPERT_FF), wspec(D_MODEL, EXPERT_FF), wspec(EXPERT_FF, D_MODEL)],
        out_specs=pl.BlockSpec(memory_space=pl.ANY),
        scratch_shapes=[pltpu.VMEM((n_tok // 2 + 1, 2 * n_chunks, LANES), BF16),
                        pltpu.VMEM((n_tok + 1, n_chunks, LANES), F32),
                        pltpu.VMEM((n_chunks * (rows + SUBLANES), LANES), F32),
                        pltpu.VMEM((n_chunks * (rows + SUBLANES), LANES), F32),
                        pltpu.SMEM((rows,), I32),
                        pltpu.SemaphoreType.DMA(())])
    return pl.pallas_call(
        functools.partial(_moe_kernel, n_tok=n_tok, rows=rows, sub=64, batch=8),
        grid_spec=grid_spec, out_shape=jax.ShapeDtypeStruct((n_tok, n_chunks, LANES), F32),
        compiler_params=_cparams(1, vmem_mib), name="moe",
    )(start, cnt, tok_sorted, wc, x_rows.reshape(n_tok // 2, 2 * n_chunks, LANES), w_gate, w_up, w_down
      ).reshape(n_tok * n_chunks, LANES)


def _ffn_out_kernel(x1_ref, u2_ref, routed_ref, mod_ref, wgu_ref, wd_ref, lnw_ref, lnb_ref, o_ref):
    gu = _dot(u2_ref[...], wgu_ref[...])
    shared = _dot((_silu(gu[:, 0:SHARED_FF]) * gu[:, SHARED_FF:]).astype(BF16), wd_ref[...])
    tt = x1_ref.shape[0]
    routed = jnp.concatenate([routed_ref[pl.ds(c, tt, stride=OUT_CHUNKS), :] for c in range(OUT_CHUNKS)], axis=1)
    h = DEEPNORM_ALPHA * x1_ref[...] + mod_ref[0, 5:6, :] * (routed + shared)
    o_ref[...] = _layer_norm(h, lnw_ref[...], lnb_ref[...])


def _ffn_out(x1, u2, routed, modp, sh_gu, sh_d, ln_w, ln_b, layer, seq_len, tt):
    t = x1.shape[0]
    tiles_per_seq = seq_len // tt
    row = lambda i: (i, 0)
    lsel = lambda *w: pl.BlockSpec((None,) + w, lambda i: (layer,) + (0,) * len(w))
    ln_spec = pl.BlockSpec((None, None, 1, D_MODEL), lambda i: (layer, 1, 0, 0))
    return pl.pallas_call(
        _ffn_out_kernel, grid=(t // tt,),
        in_specs=[pl.BlockSpec((tt, D_MODEL), row), pl.BlockSpec((tt, D_MODEL), row),
                  pl.BlockSpec((tt * OUT_CHUNKS, LANES), row),
                  _mod_spec(modp, tiles_per_seq),
                  lsel(D_MODEL, 2 * SHARED_FF), lsel(SHARED_FF, D_MODEL), ln_spec, ln_spec],
        out_specs=pl.BlockSpec((tt, D_MODEL), row),
        out_shape=jax.ShapeDtypeStruct((t, D_MODEL), F32),
        compiler_params=_cparams(1, 48), name="ffn_out",
    )(x1, u2, routed, modp, sh_gu, sh_d, ln_w, ln_b)


def _q_perm():
    cols = []
    for j in range(ATT_HEADS // 2):
        for half in range(2):
            h = j + (ATT_HEADS // 2) * half
            cols.extend(range(h * HEAD_DIM, (h + 1) * HEAD_DIM))
    return np.asarray(cols, np.int32)


def _prepare_params(p):
    off = np.concatenate([[0], np.cumsum(IN_SIZES)])
    seg = lambda k: np.arange(off[k], off[k + 1], dtype=np.int32)
    qp = _q_perm()
    cols = np.concatenate([qp, seg(1), seg(2), seg(3), seg(4), seg(5), seg(6), seg(7), seg(9), seg(10), seg(8), seg(11)])
    w_in = jnp.take(p["w_in"], jnp.asarray(cols), axis=2)
    w_in = jnp.pad(w_in, ((0, 0), (0, 0), (0, IN_W_PAD - w_in.shape[2]))).astype(BF16)
    rows = np.concatenate([qp, np.arange(ATT_WIDTH, D_MODEL, dtype=np.int32)])
    w_out = jnp.take(p["w_out"], jnp.asarray(rows), axis=1).astype(BF16)
    rwt = jnp.swapaxes(p["router_w"], 1, 2)
    rwh = rwt.astype(BF16)
    rwl = (rwt - rwh.astype(F32)).astype(BF16)

    def lane_pad(a, fill=0.0):
        a = a.reshape(DEPTH, 1, -1)
        return jnp.pad(a, ((0, 0), (0, 0), (0, LANES - a.shape[2])), constant_values=fill)

    gi = np.arange(512) // HEAD_DIM
    g512 = jnp.asarray(gi[:, None] == gi[None, :], BF16)
    return {
        "w_in": w_in, "w_out": w_out, "rwh": rwh, "rwl": rwl,
        "qnw": jnp.tile(p["q_norm_w"], (1, ATT_HEADS)).reshape(DEPTH, 1, 512),
        "knw": jnp.tile(p["k_norm_w"], (1, ATT_KV_HEADS)).reshape(DEPTH, 1, LANES),
        "g512": g512, "g128": g512[0:LANES, 0:LANES],
        "conv_w": p["ssd_conv_w"], "conv_b": p["ssd_conv_b"].reshape(DEPTH, 1, CONV_CH),
        "dt_bias": lane_pad(p["ssd_dt_bias"]), "a_log": lane_pad(p["ssd_a_log"]),
        "ret_ld": lane_pad(p["ret_log_decay"], -1.0),
        "ret_nw": p["ret_norm_w"].reshape(DEPTH, 1, RET_WIDTH),
        "d_lane": jnp.repeat(p["ssd_d"], HEAD_DIM, axis=1).reshape(DEPTH, 1, SSD_WIDTH),
        "ssd_nw": p["ssd_norm_w"].reshape(DEPTH, 1, SSD_WIDTH),
        "ln_w": p["ln_w"].reshape(DEPTH, 2, 1, D_MODEL), "ln_b": p["ln_b"].reshape(DEPTH, 2, 1, D_MODEL),
        "router_bias": p["router_bias"].reshape(DEPTH, N_EXPERTS, 1),
        "sh_gu": jnp.concatenate([p["sh_w_gate"], p["sh_w_up"]], axis=2).astype(BF16),
        "sh_d": p["sh_w_down"].astype(BF16),
        "exp_w_gate": p["exp_w_gate"], "exp_w_up": p["exp_w_up"], "exp_w_down": p["exp_w_down"],
    }


def _rope_tables(n_tokens):
    t = jnp.arange(n_tokens, dtype=I32)
    rows = (t // GRID_W).astype(F32)
    cols = (t % GRID_W).astype(F32)
    inv = ROPE_THETA ** (-jnp.arange(ROPE_AXIS_FREQS, dtype=F32) / ROPE_AXIS_FREQS)
    ar = rows[:, None] * inv
    ac = cols[:, None] * inv
    cos = jnp.concatenate([jnp.cos(ar), jnp.cos(ar), jnp.cos(ac), jnp.cos(ac)], axis=1)
    sin = jnp.concatenate([-jnp.sin(ar), jnp.sin(ar), -jnp.sin(ac), jnp.sin(ac)], axis=1)
    return jnp.tile(cos, (1, 2)), jnp.tile(sin, (1, 2))


def _pair_states(st, kind):
    b = st.shape[0]
    out = jnp.zeros((b, 2, 2, LANES, LANES), F32)
    for pr in range(2):
        for tt in range(2):
            r0 = HEAD_DIM * (tt if kind == "ret" else pr)
            out = out.at[:, :, pr, r0:r0 + HEAD_DIM, HEAD_DIM * tt:HEAD_DIM * (tt + 1)].set(st[:, :, 2 * pr + tt])
    return out


def _unpair_states(m, kind):
    heads = []
    for pr in range(2):
        for tt in range(2):
            r0 = HEAD_DIM * (tt if kind == "ret" else pr)
            heads.append(m[:, :, pr, r0:r0 + HEAD_DIM, HEAD_DIM * tt:HEAD_DIM * (tt + 1)])
    return jnp.stack(heads, axis=2)


def _trunk_layer(x, modp, pw, layer, n_seq, seq_len, rope_tabs, cache, init, cfg):
    n_tok = x.shape[0]
    ctx = cache is None
    outs = _proj_in(x, modp, pw["w_in"], pw["qnw"], pw["knw"], pw["g512"], rope_tabs, layer, seq_len,
                    cfg["tt"], emit_cache=ctx)
    q, k, v, ret, ssd = outs[:5]
    att = _attention(q, k, v, cache, layer, n_seq, seq_len, cfg["tq"])
    sc = _scan(ret, ssd, pw, layer, init, n_seq, seq_len, cfg["chunk"], emit_state=ctx)
    x1, u2, u2_rows, lg_t = _proj_out(x, att, sc[0], sc[1], modp, pw["w_out"], pw["ln_w"], pw["ln_b"],
                                  pw["rwh"], pw["rwl"], layer, seq_len, cfg["tt"])
    idx_t, rank_t, wc, counts = _route(lg_t, pw["router_bias"], layer, cfg["tr"])
    cnt = counts[:, 0].astype(I32)
    start = jnp.cumsum(cnt) - cnt
    tok_sorted = _plan(start, idx_t, rank_t, cfg["tr"])
    routed = _moe(start, cnt, tok_sorted, wc, u2_rows, pw["exp_w_gate"], pw["exp_w_up"], pw["exp_w_down"],
                  layer, n_tok, cfg["rows"], cfg["moe_vmem"])
    x_new = _ffn_out(x1, u2, routed, modp, pw["sh_gu"], pw["sh_d"], pw["ln_w"], pw["ln_b"], layer, seq_len, cfg["tt"])
    new_ctx = (outs[5], outs[6], sc[2], sc[3]) if ctx else None
    return x_new, new_ctx


def kernel(x_prompt, x_sample, cache_k, cache_v, state_ret, state_ssd, c, c_ctx,
           w_mod, b_mod, w_in, q_norm_w, k_norm_w, ret_log_decay, ret_norm_w,
           ssd_conv_w, ssd_conv_b, ssd_dt_bias, ssd_a_log, ssd_d, ssd_norm_w, w_out,
           ln_w, ln_b, router_w, router_bias, exp_w_gate, exp_w_up, exp_w_down,
           sh_w_gate, sh_w_up, sh_w_down):
    pw = _prepare_params({
        "w_in": w_in, "q_norm_w": q_norm_w, "k_norm_w": k_norm_w, "ret_log_decay": ret_log_decay,
        "ret_norm_w": ret_norm_w, "ssd_conv_w": ssd_conv_w, "ssd_conv_b": ssd_conv_b, "ssd_dt_bias": ssd_dt_bias,
        "ssd_a_log": ssd_a_log, "ssd_d": ssd_d, "ssd_norm_w": ssd_norm_w, "w_out": w_out, "ln_w": ln_w, "ln_b": ln_b,
        "router_w": router_w, "router_bias": router_bias, "exp_w_gate": exp_w_gate, "exp_w_up": exp_w_up,
        "exp_w_down": exp_w_down, "sh_w_gate": sh_w_gate, "sh_w_up": sh_w_up, "sh_w_down": sh_w_down})
    nb, seq, d = x_prompt.shape
    nd, dseq, _ = x_sample.shape
    cond = jnp.concatenate([c_ctx[None], c, jnp.zeros((8 - 1 - nd, d), F32)], axis=0)
    mod = _modulation(cond, w_mod, b_mod)
    rope_tabs = _rope_tables(dseq)
    past = cache_k.shape[2]
    ck = cache_k.reshape(nd, DEPTH, past, LANES)
    cv = cache_v.reshape(nd, DEPTH, past, LANES)
    cfg_ctx = {"tt": 512, "tq": seq, "chunk": SCAN_CHUNK, "tr": 512, "rows": MOE_ROWS, "moe_vmem": MOE_VMEM_MIB}
    cfg_lat = {"tt": 512, "tq": 512, "chunk": SCAN_CHUNK, "tr": 512, "rows": MOE_ROWS, "moe_vmem": MOE_VMEM_MIB}

    xc = x_prompt.reshape(nb * seq, d)
    xs = x_sample.reshape(nd * dseq, d)
    ks, vs, srs, sss = [], [], [], []
    for l in range(DEPTH):
        mod_ctx = mod[l, 0:1].reshape(1, 6, d)
        mod_lat = mod[l, 1:1 + nd].reshape(nd, 6, d)
        xc, (k_l, v_l, sr_l, ss_l) = _trunk_layer(xc, mod_ctx, pw, l, nb, seq, None, None, None, cfg_ctx)
        ks.append(k_l.reshape(nb, seq, ATT_KV_HEADS, HEAD_DIM))
        vs.append(v_l.reshape(nb, seq, ATT_KV_HEADS, HEAD_DIM))
        srs.append(_unpair_states(sr_l, "ret"))
        sss.append(_unpair_states(ss_l, "ssd"))
        init = (_pair_states(state_ret[:, l], "ret"), _pair_states(state_ssd[:, l], "ssd"))
        xs, _ = _trunk_layer(xs, mod_lat, pw, l, nd, dseq, rope_tabs, (ck, cv), init, cfg_lat)
    return (xc.reshape(nb, seq, d), xs.reshape(nd, dseq, d), jnp.stack(ks, axis=1), jnp.stack(vs, axis=1),
            jnp.stack(srs, axis=1), jnp.stack(sss, axis=1))
```
